```python
import math
import jax, jax.numpy as jnp
from jax import lax
import numpy as np

D_MODEL = 1024
BATCH = 16
SEQ = 2048
DEPTH = 1
DEC_BATCH = 32
DEC_SEQ = 1
PAST_LEN = 16384
PAGE_SIZE = 128

D_MIX = D_MODEL
D_ATT = D_MIX // 2
D_SSM = D_MIX - D_ATT
HEAD_DIM = 64
ATT_HEADS = D_ATT // HEAD_DIM
SSM_GROUP = 16
SSM_GROUPS = D_SSM // SSM_GROUP
SSM_STATE = 64
Q_BLOCK = 128
EPS = 1e-6
FGATE_BIAS_MIN = 4.0
FGATE_BIAS_MAX = 12.0
POOL_NUM = 5
POOL_DEN = 4
D_IN_PROJ = 4 * D_ATT + ATT_HEADS + 2 * D_SSM

kernel_name = "fox_s5_parallel_hybrid_step"


def _rmsnorm(x, g):
    xf = x.astype(jnp.float32)
    y = xf * lax.rsqrt(jnp.mean(xf * xf, axis=-1, keepdims=True) + EPS)
    return y.astype(x.dtype) * g


def _pre(x, c, g_norm, w_ada, b_ada, w_in, b_fgate):
    n, l, _ = x.shape
    mod = jax.nn.silu(c) @ w_ada + b_ada
    shift, scale, gate = jnp.split(mod, 3, axis=-1)
    h = _rmsnorm(x, g_norm) * (1 + scale[:, None]) + shift[:, None]
    p = h @ w_in
    sizes = (D_ATT, D_ATT, D_ATT, D_ATT, ATT_HEADS, D_SSM, D_SSM)
    idx = [int(s) for s in np.cumsum(sizes)[:-1]]
    q, k, v, z_att, fl, u, z_ssm = jnp.split(p, idx, axis=-1)
    heads = lambda t: t.reshape(n, l, ATT_HEADS, HEAD_DIM)
    logf = jax.nn.log_sigmoid((fl + b_fgate).astype(jnp.float32))
    return (gate, heads(q), heads(k), heads(v), z_att, logf,
            u.reshape(n, l, SSM_GROUPS, SSM_GROUP), z_ssm)


def _fox_prompt(q, k, v, logf):
    n, s, h, dh = q.shape
    nb = s // Q_BLOCK
    scale = HEAD_DIM ** -0.5
    cum = jnp.cumsum(logf, axis=1)
    cum_k = cum.transpose(0, 2, 1)
    qb = q.reshape(n, nb, Q_BLOCK, h, dh).swapaxes(0, 1)
    cb = cum.reshape(n, nb, Q_BLOCK, h).swapaxes(0, 1)
    qpos = jnp.arange(s).reshape(nb, Q_BLOCK)
    kpos = jnp.arange(s)

    def one_block(args):
        qi, ci, pi = args
        sc = jnp.einsum('nqhd,nkhd->nhqk', qi, k, preferred_element_type=jnp.float32) * scale
        bias = ci.transpose(0, 2, 1)[..., None] - cum_k[:, :, None, :]
        mask = kpos[None, None, None, :] <= pi[None, None, :, None]
        pr = jax.nn.softmax(jnp.where(mask, sc + bias, -jnp.inf), axis=-1)
        return jnp.einsum('nhqk,nkhd->nqhd', pr.astype(v.dtype), v)

    o = lax.map(one_block, (qb, cb, qpos))
    return o.swapaxes(0, 1).reshape(n, s, h, dh)


def _fox_sample(q, k_new, v_new, logf_new, k_past, v_past, logf_past):
    t = q.shape[1]
    scale = HEAD_DIM ** -0.5
    lp = logf_past.astype(jnp.float32)
    suffix = lax.cumsum(lp, axis=1, reverse=True) - lp
    cnew = jnp.cumsum(logf_new, axis=1).transpose(0, 2, 1)
    s_past = (jnp.einsum('nthd,nphd->nhtp', q, k_past, preferred_element_type=jnp.float32) * scale
              + suffix.transpose(0, 2, 1)[:, :, None, :] + cnew[..., None])
    s_new = (jnp.einsum('nthd,nkhd->nhtk', q, k_new, preferred_element_type=jnp.float32) * scale
             + cnew[..., None] - cnew[:, :, None, :])
    ti = jnp.arange(t)
    s_new = jnp.where(ti[None, :] <= ti[:, None], s_new, -jnp.inf)
    pr = jax.nn.softmax(jnp.concatenate([s_past, s_new], axis=-1), axis=-1)
    n_past = k_past.shape[1]
    pr = pr.astype(v_new.dtype)
    return (jnp.einsum('nhtp,nphd->nthd', pr[..., :n_past], v_past)
            + jnp.einsum('nhtk,nkhd->nthd', pr[..., n_past:], v_new))


def _s5_scan(u, h0_re, h0_im, a_re, a_im, log_dt, b_re, b_im, c_re, c_im, d_skip):
    f32 = jnp.float32
    lam = lax.complex(a_re.astype(f32), a_im.astype(f32))
    dt = jnp.exp(log_dt.astype(f32))[:, None]
    a_bar = jnp.exp(lam * dt)
    b_bar = ((a_bar - 1.0) / lam)[:, :, None] * lax.complex(b_re.astype(f32), b_im.astype(f32))
    uf = u.astype(f32)
    bu = jnp.einsum('gph,nlgh->nlgp', b_bar, uf.astype(jnp.complex64))
    h0 = lax.complex(h0_re.astype(f32), h0_im.astype(f32))
    bu = bu.at[:, 0].add(a_bar * h0)
    a = jnp.broadcast_to(a_bar, bu.shape)

    def combine(e1, e2):
        a1, b1 = e1
        a2, b2 = e2
        return a1 * a2, a2 * b1 + b2

    _, h = lax.associative_scan(combine, (a, bu), axis=1)
    c_mat = lax.complex(c_re.astype(f32), c_im.astype(f32))
    y = jnp.real(jnp.einsum('ghp,nlgp->nlgh', c_mat, h)) + d_skip.astype(f32) * uf
    h_last = h[:, -1]
    return y, jnp.real(h_last), jnp.imag(h_last)


def _post(x, gate, o_att, z_att, y_ssm, z_ssm, w_glu, b_glu, w_out):
    n, l, _ = x.shape
    att = o_att.reshape(n, l, D_ATT) * jax.nn.silu(z_att)
    g = jax.nn.gelu(y_ssm.reshape(n, l, D_SSM).astype(x.dtype))
    ssm = g * jax.nn.sigmoid(g @ w_glu + b_glu) * jax.nn.silu(z_ssm)
    mix = jnp.concatenate([att, ssm], axis=-1)
    return x + gate[:, None] * (mix @ w_out)


def setup_inputs(seed: int = 0) -> dict:
    key = jax.random.key(seed)
    ks = jax.random.split(key, 32)
    f32 = jnp.float32
    nrm = lambda k, s: jax.random.normal(k, s, f32)
    n_pages = PAST_LEN // PAGE_SIZE
    n_pool = (DEC_BATCH * n_pages * POOL_NUM) // POOL_DEN
    page_table = jax.random.permutation(ks[0], n_pool)[:DEC_BATCH * n_pages]
    page_table = page_table.reshape(DEC_BATCH, n_pages).astype(jnp.int32)
    fg = jnp.linspace(FGATE_BIAS_MIN, FGATE_BIAS_MAX, ATT_HEADS, dtype=f32)
    a_im = (math.pi * jnp.arange(SSM_STATE, dtype=f32))[None, None, :] + 0.01 * nrm(ks[13], (DEPTH, SSM_GROUPS, SSM_STATE))
    return {
        "x_prompt": nrm(ks[1], (BATCH, SEQ, D_MODEL)),
        "x_sample": nrm(ks[2], (DEC_BATCH, DEC_SEQ, D_MODEL)),
        "c_prompt": nrm(ks[3], (BATCH, D_MODEL)),
        "c_sample": nrm(ks[4], (DEC_BATCH, D_MODEL)),
        "cache_k": nrm(ks[5], (DEPTH, n_pool, PAGE_SIZE, ATT_HEADS, HEAD_DIM)),
        "cache_v": nrm(ks[6], (DEPTH, n_pool, PAGE_SIZE, ATT_HEADS, HEAD_DIM)),
        "cache_logf": jax.nn.log_sigmoid(fg + nrm(ks[7], (DEPTH, n_pool, PAGE_SIZE, ATT_HEADS))),
        "state_ssm_re": 0.1 * nrm(ks[8], (DEPTH, DEC_BATCH, SSM_GROUPS, SSM_STATE)),
        "state_ssm_im": 0.1 * nrm(ks[9], (DEPTH, DEC_BATCH, SSM_GROUPS, SSM_STATE)),
        "page_table": page_table,
        "g_norm": 1.0 + 0.02 * nrm(ks[10], (DEPTH, D_MODEL)),
        "w_ada": 0.5 * D_MODEL ** -0.5 * nrm(ks[11], (DEPTH, D_MODEL, 3 * D_MODEL)),
        "b_ada": 0.02 * nrm(ks[12], (DEPTH, 3 * D_MODEL)),
        "w_in": D_MODEL ** -0.5 * nrm(ks[14], (DEPTH, D_MODEL, D_IN_PROJ)),
        "b_fgate": fg[None, :] + 0.1 * nrm(ks[15], (DEPTH, ATT_HEADS)),
        "a_re": -0.5 + 0.01 * nrm(ks[16], (DEPTH, SSM_GROUPS, SSM_STATE)),
        "a_im": a_im,
        "log_dt": jax.random.uniform(ks[17], (DEPTH, SSM_GROUPS), f32, math.log(1e-3), math.log(1e-1)),
        "b_re": (2 * SSM_GROUP) ** -0.5 * nrm(ks[18], (DEPTH, SSM_GROUPS, SSM_STATE, SSM_GROUP)),
        "b_im": (2 * SSM_GROUP) ** -0.5 * nrm(ks[19], (DEPTH, SSM_GROUPS, SSM_STATE, SSM_GROUP)),
        "c_re": (2 * SSM_STATE) ** -0.5 * nrm(ks[20], (DEPTH, SSM_GROUPS, SSM_GROUP, SSM_STATE)),
        "c_im": (2 * SSM_STATE) ** -0.5 * nrm(ks[21], (DEPTH, SSM_GROUPS, SSM_GROUP, SSM_STATE)),
        "d_skip": nrm(ks[22], (DEPTH, SSM_GROUPS, SSM_GROUP)),
        "w_glu": D_SSM ** -0.5 * nrm(ks[23], (DEPTH, D_SSM, D_SSM)),
        "b_glu": 0.02 * nrm(ks[24], (DEPTH, D_SSM)),
        "w_out": D_MIX ** -0.5 * nrm(ks[25], (DEPTH, D_MIX, D_MODEL)),
        "g_final": 1.0 + 0.02 * nrm(ks[26], (D_MODEL,)),
    }


def reference(x_prompt, x_sample, c_prompt, c_sample, cache_k, cache_v, cache_logf,
              state_ssm_re, state_ssm_im, page_table, g_norm, w_ada, b_ada, w_in, b_fgate,
              a_re, a_im, log_dt, b_re, b_im, c_re, c_im, d_skip, w_glu, b_glu, w_out, g_final):
    xp, xs = x_prompt, x_sample
    nb_p = xp.shape[0]
    nb_s = xs.shape[0]
    kp_l, vp_l, lfp_l, hrp_l, hip_l = [], [], [], [], []
    ks_l, vs_l, lfs_l, hrs_l, his_l = [], [], [], [], []
    for l in range(DEPTH):
        ssm_w = (a_re[l], a_im[l], log_dt[l], b_re[l], b_im[l], c_re[l], c_im[l], d_skip[l])
        gate, q, k, v, z_att, logf, u, z_ssm = _pre(xp, c_prompt, g_norm[l], w_ada[l], b_ada[l], w_in[l], b_fgate[l])
        o_att = _fox_prompt(q, k, v, logf)
        h0 = jnp.zeros((nb_p, SSM_GROUPS, SSM_STATE), jnp.float32)
        y_ssm, h_re, h_im = _s5_scan(u, h0, h0, *ssm_w)
        xp = _post(xp, gate, o_att, z_att, y_ssm, z_ssm, w_glu[l], b_glu[l], w_out[l])
        kp_l.append(k); vp_l.append(v); lfp_l.append(logf); hrp_l.append(h_re); hip_l.append(h_im)
        gate, q, k, v, z_att, logf, u, z_ssm = _pre(xs, c_sample, g_norm[l], w_ada[l], b_ada[l], w_in[l], b_fgate[l])
        k_past = cache_k[l][page_table].reshape(nb_s, -1, ATT_HEADS, HEAD_DIM)
        v_past = cache_v[l][page_table].reshape(nb_s, -1, ATT_HEADS, HEAD_DIM)
        lf_past = cache_logf[l][page_table].reshape(nb_s, -1, ATT_HEADS)
        o_att = _fox_sample(q, k, v, logf, k_past, v_past, lf_past)
        y_ssm, h_re, h_im = _s5_scan(u, state_ssm_re[l], state_ssm_im[l], *ssm_w)
        xs = _post(xs, gate, o_att, z_att, y_ssm, z_ssm, w_glu[l], b_glu[l], w_out[l])
        ks_l.append(k); vs_l.append(v); lfs_l.append(logf); hrs_l.append(h_re); his_l.append(h_im)
    y_prompt = _rmsnorm(xp, g_final)
    y_sample = _rmsnorm(xs, g_final)
    return (y_prompt, y_sample,
            jnp.stack(kp_l), jnp.stack(vp_l), jnp.stack(lfp_l), jnp.stack(hrp_l), jnp.stack(hip_l),
            jnp.stack(ks_l), jnp.stack(vs_l), jnp.stack(lfs_l), jnp.stack(hrs_l), jnp.stack(his_l))
```

```python
import functools
import math

import jax
import jax.numpy as jnp
from jax import lax
from jax.experimental import pallas as pl
from jax.experimental.pallas import tpu as pltpu

f32 = jnp.float32
bf16 = jnp.bfloat16

HEAD_DIM = 64
ATT_HEADS = 8
D_ATT = ATT_HEADS * HEAD_DIM
SSM_GROUP = 16
SSM_GROUPS = 32
SSM_STATE = 64
D_SSM = SSM_GROUP * SSM_GROUPS
N_STATE = SSM_GROUPS * SSM_STATE
EPS = 1e-6
QK_SCALE = HEAD_DIM ** -0.5
LANES = 128
NT_ROWS = 2 * D_ATT + 16
VMEM_LIMIT = 56 * 1024 * 1024

_NT = (((1,), (1,)), ((), ()))
_HI = lax.Precision.HIGHEST


def _silu(x):
    return x * jax.nn.sigmoid(x)


def _log_sigmoid(x):
    return jnp.minimum(x, 0.0) - jnp.log1p(jnp.exp(-jnp.abs(x)))


def _modulated_norm(x, g, scale, shift):
    y = x * lax.rsqrt(jnp.mean(x * x, axis=-1, keepdims=True) + EPS)
    return (y * g) * (1.0 + scale) + shift


def _cparams(sem, vmem=VMEM_LIMIT):
    return pltpu.CompilerParams(dimension_semantics=sem, vmem_limit_bytes=vmem)


def _ada_kernel(c_ref, w_ref, b_ref, o_ref):
    s = _silu(c_ref[...]).astype(bf16)
    o_ref[...] = jnp.dot(s, w_ref[...].astype(bf16), preferred_element_type=f32) + b_ref[...]


def _ada_call(c_all, w_ada, b_ada):
    r, d = c_all.shape
    n_out = w_ada.shape[1]
    tn = 768
    return pl.pallas_call(
        _ada_kernel,
        grid=(n_out // tn,),
        in_specs=[pl.BlockSpec((r, d), lambda j: (0, 0)),
                  pl.BlockSpec((d, tn), lambda j: (0, j)),
                  pl.BlockSpec((1, tn), lambda j: (0, j))],
        out_specs=pl.BlockSpec((r, tn), lambda j: (0, j)),
        out_shape=jax.ShapeDtypeStruct((r, n_out), f32),
        compiler_params=_cparams(("arbitrary",)),
        name="ada",
    )(c_all, w_ada, b_ada)


def _ssm_param_kernel(are_ref, aim_ref, ldt_ref, bre_ref, bim_ref,
                      abr_ref, abi_ref, bbr_ref, bbi_ref):
    lr, li = are_ref[...], aim_ref[...]
    dt = jnp.exp(ldt_ref[...])
    mag = jnp.exp(lr * dt)
    abr = mag * jnp.cos(li * dt)
    abi = mag * jnp.sin(li * dt)
    xr, xi = abr - 1.0, abi
    den = lr * lr + li * li
    cr = (xr * lr + xi * li) / den
    ci = (xi * lr - xr * li) / den
    br, bi = bre_ref[...], bim_ref[...]
    abr_ref[...] = abr
    abi_ref[...] = abi
    bbr_ref[...] = cr * br - ci * bi
    bbi_ref[...] = cr * bi + ci * br


def _ssm_param_call(a_re, a_im, log_dt, b_re, b_im):
    g, p = a_re.shape
    hc = b_re.shape[-1]
    rep = lambda a: jnp.broadcast_to(a[:, None, :], (g, hc, p)).reshape(g * hc, p)
    ldt = jnp.broadcast_to(log_dt[:, None, None], (g, hc, p)).reshape(g * hc, p)
    bt = lambda b: jnp.swapaxes(b, 1, 2).reshape(g * hc, p)
    shp = jax.ShapeDtypeStruct((g * hc, p), f32)
    abr, abi, bbr, bbi = pl.pallas_call(
        _ssm_param_kernel, out_shape=[shp] * 4, name="ssm_par",
    )(rep(a_re), rep(a_im), ldt, bt(b_re), bt(b_im))
    first = lambda a: a.reshape(g, hc, p)[:, 0, :]
    return first(abr), first(abi), bbr.reshape(g, hc, p), bbi.reshape(g, hc, p)


def _block_diag(blocks):
    g, r, c = blocks.shape
    eye = jnp.eye(g, dtype=blocks.dtype)
    return (blocks[:, :, None, :] * eye[:, None, :, None]).reshape(g * r, g * c)


def _pre_prompt_kernel(x_ref, shift_ref, scale_ref, g_ref, wnn_ref, wnt_ref, bf_ref,
                       q_ref, zatt_ref, u_ref, zssm_ref, kt_ref, vt_ref, lft_ref):
    h = _modulated_norm(x_ref[...], g_ref[...], scale_ref[...], shift_ref[...])
    hb = h.astype(bf16)
    p = jnp.dot(hb, wnn_ref[...], preferred_element_type=f32)
    q_ref[...] = (p[:, :D_ATT] * QK_SCALE).astype(bf16)
    zatt_ref[...] = p[:, D_ATT:2 * D_ATT]
    for j in range(D_SSM // LANES):
        u_ref[j] = p[:, 2 * D_ATT + j * LANES:2 * D_ATT + (j + 1) * LANES]
    zssm_ref[...] = p[:, 2 * D_ATT + D_SSM:]
    pt = lax.dot_general(wnt_ref[...], hb, _NT, preferred_element_type=f32)
    kt_ref[...] = pt[:D_ATT]
    vt_ref[...] = pt[D_ATT:2 * D_ATT]
    lft_ref[...] = _log_sigmoid(pt[2 * D_ATT:2 * D_ATT + ATT_HEADS] + bf_ref[...])


def _pre_prompt_call(x, shift, scale, g_norm, w_nn, w_nt, b_f, tm):
    n, l, d = x.shape
    row = lambda n_, i: (n_, i, 0)
    col = lambda n_, i: (n_, 0, i)
    const2 = lambda n_, i: (0, 0)
    out_shape = [
        jax.ShapeDtypeStruct((n, l, D_ATT), bf16),
        jax.ShapeDtypeStruct((n, l, D_ATT), f32),
        jax.ShapeDtypeStruct((D_SSM // LANES, n, l, LANES), f32),
        jax.ShapeDtypeStruct((n, l, D_SSM), f32),
        jax.ShapeDtypeStruct((n, D_ATT, l), f32),
        jax.ShapeDtypeStruct((n, D_ATT, l), f32),
        jax.ShapeDtypeStruct((n, ATT_HEADS, l), f32),
    ]
    out_specs = [
        pl.BlockSpec((None, tm, D_ATT), row),
        pl.BlockSpec((None, tm, D_ATT), row),
        pl.BlockSpec((D_SSM // LANES, None, tm, LANES), lambda n_, i: (0, n_, i, 0)),
        pl.BlockSpec((None, tm, D_SSM), row),
        pl.BlockSpec((None, D_ATT, tm), col),
        pl.BlockSpec((None, D_ATT, tm), col),
        pl.BlockSpec((None, ATT_HEADS, tm), col),
    ]
    in_specs = [
        pl.BlockSpec((None, tm, d), row),
        pl.BlockSpec((None, 1, d), lambda n_, i: (n_, 0, 0)),
        pl.BlockSpec((None, 1, d), lambda n_, i: (n_, 0, 0)),
        pl.BlockSpec((1, d), const2),
        pl.BlockSpec(w_nn.shape, const2),
        pl.BlockSpec(w_nt.shape, const2),
        pl.BlockSpec((ATT_HEADS, tm), const2),
    ]
    return pl.pallas_call(
        _pre_prompt_kernel, grid=(n, l // tm), in_specs=in_specs, out_specs=out_specs,
        out_shape=out_shape, compiler_params=_cparams(("arbitrary", "arbitrary")), name="pre",
    )(x, shift, scale, g_norm, w_nn, w_nt, jnp.broadcast_to(b_f[:, None], (ATT_HEADS, tm)))


def _pre_sample_kernel(x_ref, shift_ref, scale_ref, g_ref, w_ref, bf_ref, seg_ref,
                       q_ref, k_ref, v_ref, zatt_ref, u_ref, zssm_ref, lf_ref, snew_ref):
    h = _modulated_norm(x_ref[...], g_ref[...], scale_ref[...], shift_ref[...])
    p = jnp.dot(h.astype(bf16), w_ref[...], preferred_element_type=f32)
    q = p[:, :D_ATT] * QK_SCALE
    k = p[:, D_ATT:2 * D_ATT]
    q_ref[...] = q
    k_ref[...] = k
    v_ref[...] = p[:, 2 * D_ATT:3 * D_ATT]
    zatt_ref[...] = p[:, 3 * D_ATT:4 * D_ATT]
    u_ref[...] = p[:, 4 * D_ATT:4 * D_ATT + D_SSM]
    zssm_ref[...] = p[:, 4 * D_ATT + D_SSM:4 * D_ATT + 2 * D_SSM]
    lf_ref[...] = _log_sigmoid(p[:, 4 * D_ATT + 2 * D_SSM:] + bf_ref[...])
    snew_ref[...] = jnp.dot(q * k, seg_ref[...], preferred_element_type=f32, precision=_HI)


def _pre_sample_call(x, shift, scale, g_norm, w_all, b_f_pad, seg):
    r, d = x.shape
    shp = lambda c: jax.ShapeDtypeStruct((r, c), f32)
    return pl.pallas_call(
        _pre_sample_kernel,
        out_shape=[shp(D_ATT), shp(D_ATT), shp(D_ATT), shp(D_ATT), shp(D_SSM), shp(D_SSM), shp(LANES), shp(LANES)],
        compiler_params=pltpu.CompilerParams(vmem_limit_bytes=VMEM_LIMIT), name="pre_s",
    )(x, shift, scale, g_norm, w_all, b_f_pad, seg)


def _att_kernel(q_ref, kt_ref, vt_ref, lft_ref, z_ref, o_ref, ktb, vtb, cum, *, tq, seq):
    qi = pl.program_id(1)

    @pl.when(qi == 0)
    def _():
        ktb[...] = kt_ref[...].astype(bf16)
        vtb[...] = vt_ref[...].astype(bf16)
        c = lft_ref[...]
        lane = lax.broadcasted_iota(jnp.int32, c.shape, 1)
        d = 1
        while d < seq:
            c = c + jnp.where(lane >= d, pltpu.roll(c, d, axis=1), 0.0)
            d *= 2
        cum[...] = jnp.zeros_like(cum)
        cum[0:ATT_HEADS, :] = c

    q0 = pl.multiple_of(qi * tq, tq)
    cq_all = cum[:, pl.ds(q0, tq)].T
    lane_q = lax.broadcasted_iota(jnp.int32, (tq, LANES), 1)
    row = lax.broadcasted_iota(jnp.int32, (tq, tq), 0)
    col = lax.broadcasted_iota(jnp.int32, (tq, tq), 1)
    causal = col <= row

    for pr in range(ATT_HEADS // 2):
        rows = slice(pr * LANES, (pr + 1) * LANES)
        qp = q_ref[:, rows].astype(f32)
        outs = []
        for hh in range(2):
            h = 2 * pr + hh
            in_head = (lane_q >= hh * HEAD_DIM) & (lane_q < (hh + 1) * HEAD_DIM)
            qm = jnp.where(in_head, qp, 0.0).astype(bf16)
            cq = cq_all[:, h:h + 1]

            def block(kb, carry, masked):
                m, l, acc = carry
                k0 = pl.multiple_of(kb * tq, tq)
                s = jnp.dot(qm, ktb[rows, pl.ds(k0, tq)], preferred_element_type=f32)
                s = s + (cq - cum[h:h + 1, pl.ds(k0, tq)])
                if masked:
                    s = jnp.where(causal, s, -jnp.inf)
                m_new = jnp.maximum(m, jnp.max(s, axis=-1, keepdims=True))
                alpha = jnp.exp(m - m_new)
                p = jnp.exp(s - m_new)
                l = alpha * l + jnp.sum(p, axis=-1, keepdims=True)
                pv = lax.dot_general(p.astype(bf16), vtb[rows, pl.ds(k0, tq)], _NT,
                                     preferred_element_type=f32)
                return m_new, l, alpha * acc + pv

            init = (jnp.full((tq, 1), -jnp.inf, f32), jnp.zeros((tq, 1), f32), jnp.zeros((tq, LANES), f32))
            carry = lax.fori_loop(0, qi, functools.partial(block, masked=False), init)
            _, l, acc = block(qi, carry, True)
            outs.append(acc * (1.0 / l))
        o = jnp.where(lane_q < HEAD_DIM, outs[0], outs[1])
        o_ref[:, rows] = (o * _silu(z_ref[:, rows])).astype(bf16)


def _att_call(q, kt, vt, lft, z_att, tq):
    n, l, _ = q.shape
    row = lambda n_, i: (n_, i, 0)
    whole = lambda n_, i: (n_, 0, 0)
    return pl.pallas_call(
        functools.partial(_att_kernel, tq=tq, seq=l),
        grid=(n, l // tq),
        in_specs=[pl.BlockSpec((None, tq, D_ATT), row),
                  pl.BlockSpec((None, D_ATT, l), whole),
                  pl.BlockSpec((None, D_ATT, l), whole),
                  pl.BlockSpec((None, ATT_HEADS, l), whole),
                  pl.BlockSpec((None, tq, D_ATT), row)],
        out_specs=pl.BlockSpec((None, tq, D_ATT), row),
        out_shape=jax.ShapeDtypeStruct((n, l, D_ATT), bf16),
        scratch_shapes=[pltpu.VMEM((D_ATT, l), bf16), pltpu.VMEM((D_ATT, l), bf16),
                        pltpu.VMEM((LANES, l), f32)],
        compiler_params=_cparams(("arbitrary", "arbitrary")), name="att",
    )(q, kt, vt, lft, z_att)


def _ssm_kernel(u_ref, h0r_ref, h0i_ref, ar_ref, ai_ref, bre_ref, bim_ref, cre_ref, cim_ref, d_ref,
                y_ref, hr_ref, hi_ref,
                utn, bu_re, bu_im, hs_re, hs_im, st_re, st_im, yscr, perm, *, nb, tl, time_major, chunk):
    i = pl.program_id(0)
    nj = D_SSM // LANES

    @pl.when(i == 0)
    def _():
        st_re[...] = h0r_ref[...]
        st_im[...] = h0i_ref[...]

    for j in range(nj):
        cols = slice(j * LANES, (j + 1) * LANES)
        if time_major:
            utn[:, cols] = u_ref[j].reshape(tl * nb, LANES)
        else:
            for b in range(nb):
                perm[j, b * tl:(b + 1) * tl, :] = u_ref[j, b]
            for t in range(tl):
                utn[t * nb:(t + 1) * nb, cols] = perm[j, pl.ds(t, nb, stride=tl), :]

    ub = utn[...].astype(bf16)
    bu_re[...] = jnp.dot(ub, bre_ref[...], preferred_element_type=f32)
    bu_im[...] = jnp.dot(ub, bim_ref[...], preferred_element_type=f32)

    for c in range(N_STATE // chunk):
        cols = slice(c * chunk, (c + 1) * chunk)
        ar = jnp.broadcast_to(ar_ref[:, cols], (nb, chunk))
        ai = jnp.broadcast_to(ai_ref[:, cols], (nb, chunk))

        def step(t, carry):
            hr, hi = carry
            r0 = pl.multiple_of(t * nb, nb)
            nr = ar * hr - ai * hi + bu_re[pl.ds(r0, nb), cols]
            ni = ar * hi + ai * hr + bu_im[pl.ds(r0, nb), cols]
            hs_re[pl.ds(r0, nb), cols] = nr.astype(bf16)
            hs_im[pl.ds(r0, nb), cols] = ni.astype(bf16)
            return nr, ni

        hr, hi = lax.fori_loop(0, tl, step, (st_re[:, cols], st_im[:, cols]))
        st_re[:, cols] = hr
        st_im[:, cols] = hi

    y = (jnp.dot(hs_re[...], cre_ref[...], preferred_element_type=f32)
         - jnp.dot(hs_im[...], cim_ref[...], preferred_element_type=f32)
         + d_ref[...] * utn[...])
    if time_major:
        for j in range(nj):
            y_ref[j] = y[:, j * LANES:(j + 1) * LANES].reshape(tl, nb, LANES)
    else:
        yscr[...] = y
        for j in range(nj):
            for t in range(tl):
                perm[j, pl.ds(t, nb, stride=tl), :] = yscr[t * nb:(t + 1) * nb, j * LANES:(j + 1) * LANES]
            for b in range(nb):
                y_ref[j, b] = perm[j, b * tl:(b + 1) * tl, :]

    @pl.when(i == pl.num_programs(0) - 1)
    def _():
        hr_ref[...] = st_re[...]
        hi_ref[...] = st_im[...]


def _ssm_call(u4, h0_re, h0_im, ab_re, ab_im, b_re, b_im, c_re, c_im, d_skip, tl, time_major):
    nj = u4.shape[0]
    if time_major:
        l, nb = u4.shape[1], u4.shape[2]
        ublock = pl.BlockSpec((nj, tl, nb, LANES), lambda i: (0, i, 0, 0))
    else:
        nb, l = u4.shape[1], u4.shape[2]
        ublock = pl.BlockSpec((nj, nb, tl, LANES), lambda i: (0, 0, i, 0))
    rows = tl * nb
    const = lambda i: (0, 0)
    full = lambda a: pl.BlockSpec(a.shape, const)
    return pl.pallas_call(
        functools.partial(_ssm_kernel, nb=nb, tl=tl, time_major=time_major, chunk=512),
        grid=(l // tl,),
        in_specs=[ublock, full(h0_re), full(h0_im), full(ab_re), full(ab_im),
                  full(b_re), full(b_im), full(c_re), full(c_im), full(d_skip)],
        out_specs=[ublock, full(h0_re), full(h0_im)],
        out_shape=[jax.ShapeDtypeStruct(u4.shape, f32),
                   jax.ShapeDtypeStruct(h0_re.shape, f32), jax.ShapeDtypeStruct(h0_im.shape, f32)],
        scratch_shapes=[pltpu.VMEM((rows, D_SSM), f32),
                        pltpu.VMEM((rows, N_STATE), f32), pltpu.VMEM((rows, N_STATE), f32),
                        pltpu.VMEM((rows, N_STATE), bf16), pltpu.VMEM((rows, N_STATE), bf16),
                        pltpu.VMEM((nb, N_STATE), f32), pltpu.VMEM((nb, N_STATE), f32),
                        pltpu.VMEM((rows, D_SSM), f32),
                        pltpu.VMEM((nj, rows, LANES), f32)],
        compiler_params=_cparams(("arbitrary",)), name="ssm",
    )(u4, h0_re, h0_im, ab_re, ab_im, b_re, b_im, c_re, c_im, d_skip)


def _dec_kernel(pt_ref, qb_ref, snew_ref, lfnew_ref, vnew_ref, tri_ref, ones_ref,
                kc_ref, vc_ref, lfc_ref, o_ref,
                buf, lfbuf, sc, bias, acc, sem, lfsem, *, n_pages, cpp, n_seq):
    n = pl.program_id(0)
    nc = n_pages // cpp

    def page_copy(src_ref, nn, pg, slot, j):
        return pltpu.make_async_copy(src_ref.at[pt_ref[nn, pg]], buf.at[slot, j], sem.at[slot])

    def start_chunk(src_ref, nn, c, slot):
        for j in range(cpp):
            page_copy(src_ref, nn, c * cpp + j, slot, j).start()

    def wait_chunk(src_ref, slot):
        for j in range(cpp):
            pltpu.make_async_copy(src_ref.at[0], buf.at[slot, j], sem.at[slot]).wait()

    def lf_copy(nn, pg, slot):
        return pltpu.make_async_copy(lfc_ref.at[pt_ref[nn, pg]], lfbuf.at[slot, pg], lfsem.at[slot])

    def start_lf(nn, slot):
        def body(pg, _):
            lf_copy(nn, pg, slot).start()
            return 0
        lax.fori_loop(0, n_pages, body, 0)

    @pl.when(n == 0)
    def _():
        start_lf(0, 0)
        start_chunk(kc_ref, 0, 0, 0)

    lslot = n % 2

    @pl.when(n + 1 < n_seq)
    def _():
        start_lf(n + 1, 1 - lslot)

    def wait_lf(pg, _):
        pltpu.make_async_copy(lfc_ref.at[0], lfbuf.at[lslot, pg], lfsem.at[lslot]).wait()
        return 0
    lax.fori_loop(0, n_pages, wait_lf, 0)

    lf2 = lfbuf[lslot].reshape(n_pages * ATT_HEADS, LANES)
    inpage = jnp.dot(lf2, tri_ref[...], preferred_element_type=f32, precision=_HI)
    total = jnp.dot(lf2, ones_ref[...], preferred_element_type=f32, precision=_HI)
    bias[...] = inpage.reshape(n_pages, ATT_HEADS, LANES)
    sc[...] = total.reshape(n_pages, ATT_HEADS, LANES)
    lfnew = lfnew_ref[...]

    def suffix(t, run):
        pg = n_pages - 1 - t
        bias[pg] = bias[pg] + run + lfnew
        return run + sc[pg]
    lax.fori_loop(0, n_pages, suffix, jnp.zeros((ATT_HEADS, LANES), f32))

    def k_chunk(c, _):
        slot = c % 2

        @pl.when(c + 1 < nc)
        def _():
            start_chunk(kc_ref, n, c + 1, 1 - slot)

        @pl.when(c + 1 == nc)
        def _():
            start_chunk(vc_ref, n, 0, 1 - slot)

        wait_chunk(kc_ref, slot)
        for h in range(ATT_HEADS):
            qb = qb_ref[h]

            def page(j, _):
                s = jnp.sum(buf[slot, j, h] * qb, axis=0, keepdims=True)
                sc[c * cpp + j, pl.ds(h, 1), :] = s
                return 0
            lax.fori_loop(0, cpp, page, 0)
        return 0
    lax.fori_loop(0, nc, k_chunk, 0)

    s_all = sc[...] + bias[...]
    snew = snew_ref[...]
    m = jnp.max(jnp.max(s_all, axis=0), axis=-1, keepdims=True)
    m = jnp.maximum(m, snew)
    p_all = jnp.exp(s_all - m[None])
    sc[...] = p_all
    p_new = jnp.exp(snew - m)
    denom = jnp.sum(jnp.sum(p_all, axis=0), axis=-1, keepdims=True) + p_new

    acc[...] = jnp.zeros_like(acc)

    def v_chunk(c, _):
        slot = (nc + c) % 2

        @pl.when(c + 1 < nc)
        def _():
            start_chunk(vc_ref, n, c + 1, 1 - slot)

        @pl.when((c + 1 == nc) & (n + 1 < n_seq))
        def _():
            start_chunk(kc_ref, n + 1, 0, 1 - slot)

        wait_chunk(vc_ref, slot)
        for h in range(ATT_HEADS):
            def page(j, a):
                pr = sc[c * cpp + j, pl.ds(h, 1), :]
                return a + buf[slot, j, h] * jnp.broadcast_to(pr, (HEAD_DIM, LANES))
            acc[h] = lax.fori_loop(0, cpp, page, acc[h])
        return 0
    lax.fori_loop(0, nc, v_chunk, 0)

    ones_row = jnp.ones((ATT_HEADS, LANES), f32)
    head_row = lax.broadcasted_iota(jnp.int32, (ATT_HEADS, HEAD_DIM), 0)
    past = jnp.zeros((ATT_HEADS, HEAD_DIM), f32)
    for h in range(ATT_HEADS):
        r = lax.dot_general(ones_row, acc[h], _NT, preferred_element_type=f32, precision=_HI)
        past = jnp.where(head_row == h, r, past)
    o_ref[...] = (past + p_new[:, :HEAD_DIM] * vnew_ref[...]) * (1.0 / denom[:, :HEAD_DIM])


def _dec_call(page_table, qb, snew_b, lfnew_b, vnew, kc, vc, lfc, cpp):
    ns, n_pages = page_table.shape
    lane = jnp.arange(LANES)
    tri = (lane[:, None] > lane[None, :]).astype(f32)
    ones = jnp.ones((LANES, LANES), f32)
    per_n3 = lambda n_, pt: (n_, 0, 0)
    grid_spec = pltpu.PrefetchScalarGridSpec(
        num_scalar_prefetch=1,
        grid=(ns,),
        in_specs=[pl.BlockSpec((None, ATT_HEADS, HEAD_DIM, LANES), lambda n_, pt: (n_, 0, 0, 0)),
                  pl.BlockSpec((None, ATT_HEADS, LANES), per_n3),
                  pl.BlockSpec((None, ATT_HEADS, LANES), per_n3),
                  pl.BlockSpec((None, ATT_HEADS, HEAD_DIM), per_n3),
                  pl.BlockSpec((LANES, LANES), lambda n_, pt: (0, 0)),
                  pl.BlockSpec((LANES, LANES), lambda n_, pt: (0, 0)),
                  pl.BlockSpec(memory_space=pl.ANY),
                  pl.BlockSpec(memory_space=pl.ANY),
                  pl.BlockSpec(memory_space=pl.ANY)],
        out_specs=pl.BlockSpec((None, ATT_HEADS, HEAD_DIM), per_n3),
        scratch_shapes=[pltpu.VMEM((2, cpp, ATT_HEADS, HEAD_DIM, LANES), f32),
                        pltpu.VMEM((2, n_pages, ATT_HEADS, LANES), f32),
                        pltpu.VMEM((n_pages, ATT_HEADS, LANES), f32),
                        pltpu.VMEM((n_pages, ATT_HEADS, LANES), f32),
                        pltpu.VMEM((ATT_HEADS, HEAD_DIM, LANES), f32),
                        pltpu.SemaphoreType.DMA((2,)),
                        pltpu.SemaphoreType.DMA((2,))],
    )
    return pl.pallas_call(
        functools.partial(_dec_kernel, n_pages=n_pages, cpp=cpp, n_seq=ns),
        grid_spec=grid_spec,
        out_shape=jax.ShapeDtypeStruct((ns, ATT_HEADS, HEAD_DIM), f32),
        compiler_params=_cparams(("arbitrary",)), name="dec",
    )(page_table, qb, snew_b, lfnew_b, vnew, tri, ones, kc, vc, lfc)


def _post_kernel(*refs, gate_att):
    if gate_att:
        x_ref, att_ref, zatt_ref, y_ref, zssm_ref, gate_ref, wglu_ref, bglu_ref, wout_ref, gf_ref, o_ref = refs
        att = att_ref[...] * _silu(zatt_ref[...])
    else:
        x_ref, att_ref, y_ref, zssm_ref, gate_ref, wglu_ref, bglu_ref, wout_ref, gf_ref, o_ref = refs
        att = att_ref[...]
    y = jnp.concatenate([y_ref[j] for j in range(D_SSM // LANES)], axis=-1)
    g = jax.nn.gelu(y)
    glu = jax.nn.sigmoid(jnp.dot(g.astype(bf16), wglu_ref[...], preferred_element_type=f32) + bglu_ref[...])
    ssm = g * glu * _silu(zssm_ref[...])
    mix = jnp.concatenate([att.astype(bf16), ssm.astype(bf16)], axis=-1)
    x = x_ref[...]
    xo = x + gate_ref[...] * jnp.dot(mix, wout_ref[...], preferred_element_type=f32)
    o_ref[...] = xo * lax.rsqrt(jnp.mean(xo * xo, axis=-1, keepdims=True) + EPS) * gf_ref[...]


def _post_call(x, att, z_att, y4, z_ssm, gate, w_glu, b_glu, w_out, g_final, tm):
    n, l, d = x.shape
    row = lambda n_, i: (n_, i, 0)
    const2 = lambda n_, i: (0, 0)
    gate_spec = (pl.BlockSpec((None, 1, d), lambda n_, i: (n_, 0, 0)) if gate.shape[1] == 1
                 else pl.BlockSpec((None, tm, d), row))
    ins = [x, att] + ([z_att] if z_att is not None else []) + [y4, z_ssm, gate, w_glu, b_glu, w_out, g_final]
    in_specs = ([pl.BlockSpec((None, tm, d), row), pl.BlockSpec((None, tm, D_ATT), row)]
                + ([pl.BlockSpec((None, tm, D_ATT), row)] if z_att is not None else [])
                + [pl.BlockSpec((D_SSM // LANES, None, tm, LANES), lambda n_, i: (0, n_, i, 0)),
                   pl.BlockSpec((None, tm, D_SSM), row), gate_spec,
                   pl.BlockSpec(w_glu.shape, const2), pl.BlockSpec((1, D_SSM), const2),
                   pl.BlockSpec(w_out.shape, const2), pl.BlockSpec((1, d), const2)])
    return pl.pallas_call(
        functools.partial(_post_kernel, gate_att=z_att is not None),
        grid=(n, l // tm), in_specs=in_specs,
        out_specs=pl.BlockSpec((None, tm, d), row),
        out_shape=jax.ShapeDtypeStruct((n, l, d), f32),
        compiler_params=_cparams(("arbitrary", "arbitrary")), name="post",
    )(*ins)


def kernel(x_prompt, x_sample, c_prompt, c_sample, cache_k, cache_v, cache_logf, state_ssm_re, state_ssm_im,
           page_table, g_norm, w_ada, b_ada, w_in, b_fgate, a_re, a_im, log_dt, b_re, b_im, c_re, c_im, d_skip,
           w_glu, b_glu, w_out, g_final):
    n, l, d = x_prompt.shape
    ns = x_sample.shape[0]
    depth = w_in.shape[0]
    assert depth == 1 and x_sample.shape[1] == 1
    lyr = 0
    xs = x_sample.reshape(ns, d)

    splits = [0, D_ATT, 2 * D_ATT, 3 * D_ATT, 4 * D_ATT, 4 * D_ATT + ATT_HEADS,
              4 * D_ATT + ATT_HEADS + D_SSM, 4 * D_ATT + ATT_HEADS + 2 * D_SSM]
    w_t = jnp.swapaxes(w_in[lyr], 0, 1)
    wq, wk, wv, wz, wf, wu, wzs = [w_t[splits[i]:splits[i + 1]] for i in range(7)]
    w_nn = jnp.concatenate([wq, wz, wu, wzs], axis=0).T.astype(bf16)
    w_nt = jnp.concatenate([wk, wv, wf, jnp.zeros((NT_ROWS - 2 * D_ATT - ATT_HEADS, d), f32)], axis=0).astype(bf16)
    w_all = jnp.concatenate([wq, wk, wv, wz, wu, wzs, wf, jnp.zeros((LANES - ATT_HEADS, d), f32)], axis=0).T.astype(bf16)
    b_f = b_fgate[lyr]
    b_f_pad = jnp.concatenate([b_f, jnp.zeros((LANES - ATT_HEADS,), f32)])[None, :]
    seg = (jnp.arange(D_ATT)[:, None] // HEAD_DIM == jnp.arange(LANES)[None, :]).astype(f32)
    gn = g_norm[lyr][None, :]
    gf = g_final[None, :]
    w_glu_b = w_glu[lyr].astype(bf16)
    w_out_b = w_out[lyr].astype(bf16)
    b_glu2 = b_glu[lyr][None, :]

    mod = _ada_call(jnp.concatenate([c_prompt, c_sample], axis=0), w_ada[lyr], b_ada[lyr][None, :])
    shift_p, scale_p, gate_p = [mod[:n, k * d:(k + 1) * d].reshape(n, 1, d) for k in range(3)]
    shift_s, scale_s, gate_s = [mod[n:, k * d:(k + 1) * d] for k in range(3)]

    ab_re, ab_im, bb_re, bb_im = _ssm_param_call(a_re[lyr], a_im[lyr], log_dt[lyr], b_re[lyr], b_im[lyr])
    ab_re_row = ab_re.reshape(1, N_STATE)
    ab_im_row = ab_im.reshape(1, N_STATE)
    bd_re = _block_diag(bb_re).astype(bf16)
    bd_im = _block_diag(bb_im).astype(bf16)
    cd_re = _block_diag(jnp.swapaxes(c_re[lyr], 1, 2)).astype(bf16)
    cd_im = _block_diag(jnp.swapaxes(c_im[lyr], 1, 2)).astype(bf16)
    d_row = d_skip[lyr].reshape(1, D_SSM)

    tm = min(512, l)
    q, z_att, u4, z_ssm, kt, vt, lft = _pre_prompt_call(x_prompt, shift_p, scale_p, gn, w_nn, w_nt, b_f, tm)
    att = _att_call(q, kt, vt, lft, z_att, min(256, l))
    zeros_state = jnp.zeros((n, N_STATE), f32)
    y4, hp_re, hp_im = _ssm_call(u4, zeros_state, zeros_state, ab_re_row, ab_im_row, bd_re, bd_im, cd_re, cd_im,
                                 d_row, tl=min(32, l), time_major=False)
    y_prompt = _post_call(x_prompt, att, None, y4, z_ssm, gate_p, w_glu_b, b_glu2, w_out_b, gf, tm)

    heads_t = lambda t: jnp.transpose(t.reshape(n, ATT_HEADS, HEAD_DIM, l), (0, 3, 1, 2))[None]
    k_prompt = heads_t(kt)
    v_prompt = heads_t(vt)
    logf_prompt = jnp.transpose(lft, (0, 2, 1))[None]

    qs, k_s, v_s, zatt_s, u_s, zssm_s, lf_s, snew = _pre_sample_call(xs, shift_s, scale_s, gn, w_all, b_f_pad, seg)
    qb = jnp.broadcast_to(qs.reshape(ns, ATT_HEADS, HEAD_DIM, 1), (ns, ATT_HEADS, HEAD_DIM, LANES))
    snew_b = jnp.broadcast_to(snew[:, :ATT_HEADS, None], (ns, ATT_HEADS, LANES))
    lfnew_b = jnp.broadcast_to(lf_s[:, :ATT_HEADS, None], (ns, ATT_HEADS, LANES))
    kc = jnp.transpose(cache_k[lyr], (0, 2, 3, 1))
    vc = jnp.transpose(cache_v[lyr], (0, 2, 3, 1))
    lfc = jnp.transpose(cache_logf[lyr], (0, 2, 1))
    n_pages = page_table.shape[1]
    att_s = _dec_call(page_table, qb, snew_b, lfnew_b, v_s.reshape(ns, ATT_HEADS, HEAD_DIM), kc, vc, lfc,
                      cpp=min(16, n_pages // 2))

    u4s = jnp.transpose(u_s.reshape(1, ns, D_SSM // LANES, LANES), (2, 0, 1, 3))
    h0r = state_ssm_re[lyr].reshape(ns, N_STATE)
    h0i = state_ssm_im[lyr].reshape(ns, N_STATE)
    y4s, hs_re, hs_im = _ssm_call(u4s, h0r, h0i, ab_re_row, ab_im_row, bd_re, bd_im, cd_re, cd_im, d_row,
                                  tl=1, time_major=True)
    y_sample = _post_call(xs[None], att_s.reshape(1, ns, D_ATT), zatt_s[None], y4s, zssm_s[None], gate_s[None],
                          w_glu_b, b_glu2, w_out_b, gf, ns)

    st = lambda a, b: a.reshape(1, b, SSM_GROUPS, SSM_STATE)
    return (y_prompt, y_sample.reshape(ns, 1, d),
            k_prompt, v_prompt, logf_prompt, st(hp_re, n), st(hp_im, n),
            k_s.reshape(1, ns, 1, ATT_HEADS, HEAD_DIM), v_s.reshape(1, ns, 1, ATT_HEADS, HEAD_DIM),
            lf_s[:, :ATT_HEADS].reshape(1, ns, 1, ATT_HEADS), st(hs_re, ns), st(hs_im, ns))
```

```python
import functools
import math

import jax
import jax.numpy as jnp
from jax import lax
from jax.experimental import pallas as pl
from jax.experimental.pallas import tpu as pltpu

f32 = jnp.float32
bf16 = jnp.bfloat16

HEAD_DIM = 64
ATT_HEADS = 8
D_ATT = ATT_HEADS * HEAD_DIM
SSM_GROUP = 16
SSM_GROUPS = 32
SSM_STATE = 64
D_SSM = SSM_GROUP * SSM_GROUPS
N_STATE = SSM_GROUPS * SSM_STATE
EPS = 1e-6
QK_SCALE = HEAD_DIM ** -0.5
LANES = 128
NT_ROWS = 2 * D_ATT + 16
VMEM_LIMIT = 56 * 1024 * 1024

_NT = (((1,), (1,)), ((), ()))
_HI = lax.Precision.HIGHEST


def _silu(x):
    return x * jax.nn.sigmoid(x)


def _log_sigmoid(x):
    return jnp.minimum(x, 0.0) - jnp.log1p(jnp.exp(-jnp.abs(x)))


def _modulated_norm(x, g, scale, shift):
    y = x * lax.rsqrt(jnp.mean(x * x, axis=-1, keepdims=True) + EPS)
    return (y * g) * (1.0 + scale) + shift


def _cparams(sem, vmem=VMEM_LIMIT):
    return pltpu.CompilerParams(dimension_semantics=sem, vmem_limit_bytes=vmem)


def _ada_kernel(c_ref, w_ref, b_ref, o_ref):
    s = _silu(c_ref[...]).astype(bf16)
    o_ref[...] = jnp.dot(s, w_ref[...].astype(bf16), preferred_element_type=f32) + b_ref[...]


def _ada_call(c_all, w_ada, b_ada):
    r, d = c_all.shape
    n_out = w_ada.shape[1]
    tn = 768
    return pl.pallas_call(
        _ada_kernel,
        grid=(n_out // tn,),
        in_specs=[pl.BlockSpec((r, d), lambda j: (0, 0)),
                  pl.BlockSpec((d, tn), lambda j: (0, j)),
                  pl.BlockSpec((1, tn), lambda j: (0, j))],
        out_specs=pl.BlockSpec((r, tn), lambda j: (0, j)),
        out_shape=jax.ShapeDtypeStruct((r, n_out), f32),
        compiler_params=_cparams(("arbitrary",)),
        name="ada",
    )(c_all, w_ada, b_ada)


def _ssm_param_kernel(are_ref, aim_ref, ldt_ref, bre_ref, bim_ref,
                      abr_ref, abi_ref, bbr_ref, bbi_ref):
    lr, li = are_ref[...], aim_ref[...]
    dt = jnp.exp(ldt_ref[...])
    mag = jnp.exp(lr * dt)
    abr = mag * jnp.cos(li * dt)
    abi = mag * jnp.sin(li * dt)
    xr, xi = abr - 1.0, abi
    den = lr * lr + li * li
    cr = (xr * lr + xi * li) / den
    ci = (xi * lr - xr * li) / den
    br, bi = bre_ref[...], bim_ref[...]
    abr_ref[...] = abr
    abi_ref[...] = abi
    bbr_ref[...] = cr * br - ci * bi
    bbi_ref[...] = cr * bi + ci * br


def _ssm_param_call(a_re, a_im, log_dt, b_re, b_im):
    g, p = a_re.shape
    hc = b_re.shape[-1]
    rep = lambda a: jnp.broadcast_to(a[:, None, :], (g, hc, p)).reshape(g * hc, p)
    ldt = jnp.broadcast_to(log_dt[:, None, None], (g, hc, p)).reshape(g * hc, p)
    bt = lambda b: jnp.swapaxes(b, 1, 2).reshape(g * hc, p)
    shp = jax.ShapeDtypeStruct((g * hc, p), f32)
    abr, abi, bbr, bbi = pl.pallas_call(
        _ssm_param_kernel, out_shape=[shp] * 4, name="ssm_par",
    )(rep(a_re), rep(a_im), ldt, bt(b_re), bt(b_im))
    first = lambda a: a.reshape(g, hc, p)[:, 0, :]
    return first(abr), first(abi), bbr.reshape(g, hc, p), bbi.reshape(g, hc, p)


def _block_diag(blocks):
    g, r, c = blocks.shape
    eye = jnp.eye(g, dtype=blocks.dtype)
    return (blocks[:, :, None, :] * eye[:, None, :, None]).reshape(g * r, g * c)


def _pre_prompt_kernel(x_ref, shift_ref, scale_ref, g_ref, wnn_ref, wnt_ref, bf_ref,
                       q_ref, zatt_ref, u_ref, zssm_ref, kt_ref, vt_ref, lft_ref):
    h = _modulated_norm(x_ref[...], g_ref[...], scale_ref[...], shift_ref[...])
    hb = h.astype(bf16)
    p = jnp.dot(hb, wnn_ref[...], preferred_element_type=f32)
    q_ref[...] = (p[:, :D_ATT] * QK_SCALE).astype(bf16)
    zatt_ref[...] = p[:, D_ATT:2 * D_ATT]
    for j in range(D_SSM // LANES):
        u_ref[j] = p[:, 2 * D_ATT + j * LANES:2 * D_ATT + (j + 1) * LANES]
    zssm_ref[...] = p[:, 2 * D_ATT + D_SSM:]
    pt = lax.dot_general(wnt_ref[...], hb, _NT, preferred_element_type=f32)
    kt_ref[...] = pt[:D_ATT]
    vt_ref[...] = pt[D_ATT:2 * D_ATT]
    lft_ref[...] = _log_sigmoid(pt[2 * D_ATT:2 * D_ATT + ATT_HEADS] + bf_ref[...])


def _pre_prompt_call(x, shift, scale, g_norm, w_nn, w_nt, b_f, tm):
    n, l, d = x.shape
    row = lambda n_, i: (n_, i, 0)
    col = lambda n_, i: (n_, 0, i)
    const2 = lambda n_, i: (0, 0)
    out_shape = [
        jax.ShapeDtypeStruct((n, l, D_ATT), bf16),
        jax.ShapeDtypeStruct((n, l, D_ATT), f32),
        jax.ShapeDtypeStruct((D_SSM // LANES, n, l, LANES), f32),
        jax.ShapeDtypeStruct((n, l, D_SSM), f32),
        jax.ShapeDtypeStruct((n, D_ATT, l), f32),
        jax.ShapeDtypeStruct((n, D_ATT, l), f32),
        jax.ShapeDtypeStruct((n, ATT_HEADS, l), f32),
    ]
    out_specs = [
        pl.BlockSpec((None, tm, D_ATT), row),
        pl.BlockSpec((None, tm, D_ATT), row),
        pl.BlockSpec((D_SSM // LANES, None, tm, LANES), lambda n_, i: (0, n_, i, 0)),
        pl.BlockSpec((None, tm, D_SSM), row),
        pl.BlockSpec((None, D_ATT, tm), col),
        pl.BlockSpec((None, D_ATT, tm), col),
        pl.BlockSpec((None, ATT_HEADS, tm), col),
    ]
    in_specs = [
        pl.BlockSpec((None, tm, d), row),
        pl.BlockSpec((None, 1, d), lambda n_, i: (n_, 0, 0)),
        pl.BlockSpec((None, 1, d), lambda n_, i: (n_, 0, 0)),
        pl.BlockSpec((1, d), const2),
        pl.BlockSpec(w_nn.shape, const2),
        pl.BlockSpec(w_nt.shape, const2),
        pl.BlockSpec((ATT_HEADS, tm), const2),
    ]
    return pl.pallas_call(
        _pre_prompt_kernel, grid=(n, l // tm), in_specs=in_specs, out_specs=out_specs,
        out_shape=out_shape, compiler_params=_cparams(("arbitrary", "arbitrary")), name="pre",
    )(x, shift, scale, g_norm, w_nn, w_nt, jnp.broadcast_to(b_f[:, None], (ATT_HEADS, tm)))


def _pre_sample_kernel(x_ref, shift_ref, scale_ref, g_ref, w_ref, bf_ref, seg_ref,
                       q_ref, k_ref, v_ref, zatt_ref, u_ref, zssm_ref, lf_ref, snew_ref):
    h = _modulated_norm(x_ref[...], g_ref[...], scale_ref[...], shift_ref[...])
    p = jnp.dot(h.astype(bf16), w_ref[...], preferred_element_type=f32)
    q = p[:, :D_ATT] * QK_SCALE
    k = p[:, D_ATT:2 * D_ATT]
    q_ref[...] = q
    k_ref[...] = k
    v_ref[...] = p[:, 2 * D_ATT:3 * D_ATT]
    zatt_ref[...] = p[:, 3 * D_ATT:4 * D_ATT]
    u_ref[...] = p[:, 4 * D_ATT:4 * D_ATT + D_SSM]
    zssm_ref[...] = p[:, 4 * D_ATT + D_SSM:4 * D_ATT + 2 * D_SSM]
    lf_ref[...] = _log_sigmoid(p[:, 4 * D_ATT + 2 * D_SSM:] + bf_ref[...])
    snew_ref[...] = jnp.dot(q * k, seg_ref[...], preferred_element_type=f32, precision=_HI)


def _pre_sample_call(x, shift, scale, g_norm, w_all, b_f_pad, seg):
    r, d = x.shape
    shp = lambda c: jax.ShapeDtypeStruct((r, c), f32)
    return pl.pallas_call(
        _pre_sample_kernel,
        out_shape=[shp(D_ATT), shp(D_ATT), shp(D_ATT), shp(D_ATT), shp(D_SSM), shp(D_SSM), shp(LANES), shp(LANES)],
        compiler_params=pltpu.CompilerParams(vmem_limit_bytes=VMEM_LIMIT), name="pre_s",
    )(x, shift, scale, g_norm, w_all, b_f_pad, seg)


def _split3(x):
    hi = x.astype(bf16).astype(f32)
    r = x - hi
    mid = r.astype(bf16).astype(f32)
    return hi, mid, r - mid


def _att_kernel(q_ref, kt_ref, vt_ref, lft_ref, z_ref, o_ref, ka, va, cum, qa, m_s, acc_s, *, tq, seq):
    qi = pl.program_id(1)
    aug = HEAD_DIM

    @pl.when(qi == 0)
    def _():
        c = lft_ref[...]
        lane = lax.broadcasted_iota(jnp.int32, c.shape, 1)
        d = 1
        while d < seq:
            c = c + jnp.where(lane >= d, pltpu.roll(c, d, axis=1), 0.0)
            d *= 2
        cum[...] = jnp.zeros_like(cum)
        cum[0:ATT_HEADS, :] = c
        r16 = lax.broadcasted_iota(jnp.int32, (16, seq), 0)
        for h in range(ATT_HEADS):
            ka[h, 0:HEAD_DIM, :] = kt_ref[h * HEAD_DIM:(h + 1) * HEAD_DIM, :].astype(bf16)
            va[h, 0:HEAD_DIM, :] = vt_ref[h * HEAD_DIM:(h + 1) * HEAD_DIM, :].astype(bf16)
            hi, mid, lo = _split3(-c[h:h + 1, :])
            ext = jnp.where(r16 == 0, hi, jnp.where(r16 == 1, mid, jnp.where(r16 == 2, lo,
                            jnp.where(r16 < 6, 1.0, 0.0))))
            ka[h, aug:aug + 16, :] = ext.astype(bf16)
            ka[h, aug + 16:, :] = jnp.zeros((LANES - aug - 16, seq), bf16)
            va[h, HEAD_DIM:, :] = jnp.ones((LANES - HEAD_DIM, seq), bf16)

    q0 = pl.multiple_of(qi * tq, tq)
    cq_all = cum[:, pl.ds(q0, tq)].T
    lane_q = lax.broadcasted_iota(jnp.int32, (tq, LANES), 1)
    for pr in range(ATT_HEADS // 2):
        qp = q_ref[:, pr * LANES:(pr + 1) * LANES].astype(f32)
        for hh in range(2):
            h = 2 * pr + hh
            qh = qp if hh == 0 else pltpu.roll(qp, HEAD_DIM, axis=1)
            hi, mid, lo = _split3(jnp.broadcast_to(cq_all[:, h:h + 1], (tq, LANES)))
            ext = jnp.where(lane_q < aug + 3, 1.0, jnp.where(lane_q == aug + 3, hi, jnp.where(
                lane_q == aug + 4, mid, jnp.where(lane_q == aug + 5, lo, 0.0))))
            qa[h] = jnp.where(lane_q < HEAD_DIM, qh, ext).astype(bf16)

    row = lax.broadcasted_iota(jnp.int32, (tq, tq), 0)
    col = lax.broadcasted_iota(jnp.int32, (tq, tq), 1)
    causal = col <= row

    def block(kb, diagonal):
        k0 = pl.multiple_of(kb * tq, tq)
        for h in range(ATT_HEADS):
            s = jnp.dot(qa[h], ka[h, :, pl.ds(k0, tq)], preferred_element_type=f32)
            if diagonal:
                s = jnp.where(causal, s, -jnp.inf)
            rowmax = jnp.broadcast_to(jnp.max(s, axis=-1, keepdims=True), (tq, LANES))
            m_new = rowmax if diagonal else jnp.maximum(m_s[h], rowmax)
            p = [jnp.exp(s[:, j * LANES:(j + 1) * LANES] - m_new) for j in range(tq // LANES)]
            p = jnp.concatenate(p, axis=1)
            pv = lax.dot_general(p.astype(bf16), va[h, :, pl.ds(k0, tq)], _NT, preferred_element_type=f32)
            acc_s[h] = pv if diagonal else jnp.exp(m_s[h] - m_new) * acc_s[h] + pv
            m_s[h] = m_new

    def off_diagonal(kb, carry):
        block(kb, False)
        return carry

    block(qi, True)
    lax.fori_loop(0, qi, off_diagonal, 0)

    for pr in range(ATT_HEADS // 2):
        a0, a1 = acc_s[2 * pr], acc_s[2 * pr + 1]
        o0 = a0 * pltpu.roll(1.0 / a0, HEAD_DIM, axis=1)
        o1 = pltpu.roll(a1, HEAD_DIM, axis=1) * (1.0 / a1)
        o = jnp.where(lane_q < HEAD_DIM, o0, o1)
        cols = slice(pr * LANES, (pr + 1) * LANES)
        o_ref[:, cols] = (o * _silu(z_ref[:, cols])).astype(bf16)


def _att_call(q, kt, vt, lft, z_att, tq):
    n, l, _ = q.shape
    row = lambda n_, i: (n_, i, 0)
    whole = lambda n_, i: (n_, 0, 0)
    return pl.pallas_call(
        functools.partial(_att_kernel, tq=tq, seq=l),
        grid=(n, l // tq),
        in_specs=[pl.BlockSpec((None, tq, D_ATT), row),
                  pl.BlockSpec((None, D_ATT, l), whole),
                  pl.BlockSpec((None, D_ATT, l), whole),
                  pl.BlockSpec((None, ATT_HEADS, l), whole),
                  pl.BlockSpec((None, tq, D_ATT), row)],
        out_specs=pl.BlockSpec((None, tq, D_ATT), row),
        out_shape=jax.ShapeDtypeStruct((n, l, D_ATT), bf16),
        scratch_shapes=[pltpu.VMEM((ATT_HEADS, LANES, l), bf16), pltpu.VMEM((ATT_HEADS, LANES, l), bf16),
                        pltpu.VMEM((LANES, l), f32),
                        pltpu.VMEM((ATT_HEADS, tq, LANES), bf16),
                        pltpu.VMEM((ATT_HEADS, tq, LANES), f32),
                        pltpu.VMEM((ATT_HEADS, tq, LANES), f32)],
        compiler_params=_cparams(("arbitrary", "arbitrary")), name="att",
    )(q, kt, vt, lft, z_att)


def _ssm_kernel(u_ref, h0r_ref, h0i_ref, ar_ref, ai_ref, bre_ref, bim_ref, cre_ref, cim_ref, d_ref,
                y_ref, hr_ref, hi_ref,
                utn, bu_re, bu_im, hs_re, hs_im, st_re, st_im, yscr, perm, *, nb, tl, time_major, chunk):
    i = pl.program_id(0)
    nj = D_SSM // LANES

    @pl.when(i == 0)
    def _():
        st_re[...] = h0r_ref[...]
        st_im[...] = h0i_ref[...]

    for j in range(nj):
        cols = slice(j * LANES, (j + 1) * LANES)
        if time_major:
            utn[:, cols] = u_ref[j].reshape(tl * nb, LANES)
        else:
            for b in range(nb):
                perm[j, b * tl:(b + 1) * tl, :] = u_ref[j, b]
            for t in range(tl):
                utn[t * nb:(t + 1) * nb, cols] = perm[j, pl.ds(t, nb, stride=tl), :]

    ub = utn[...].astype(bf16)
    bu_re[...] = jnp.dot(ub, bre_ref[...], preferred_element_type=f32)
    bu_im[...] = jnp.dot(ub, bim_ref[...], preferred_element_type=f32)

    for c in range(N_STATE // chunk):
        cols = slice(c * chunk, (c + 1) * chunk)
        ar = jnp.broadcast_to(ar_ref[:, cols], (nb, chunk))
        ai = jnp.broadcast_to(ai_ref[:, cols], (nb, chunk))

        def step(t, carry):
            hr, hi = carry
            r0 = pl.multiple_of(t * nb, nb)
            nr = ar * hr - ai * hi + bu_re[pl.ds(r0, nb), cols]
            ni = ar * hi + ai * hr + bu_im[pl.ds(r0, nb), cols]
            hs_re[pl.ds(r0, nb), cols] = nr.astype(bf16)
            hs_im[pl.ds(r0, nb), cols] = ni.astype(bf16)
            return nr, ni

        hr, hi = lax.fori_loop(0, tl, step, (st_re[:, cols], st_im[:, cols]))
        st_re[:, cols] = hr
        st_im[:, cols] = hi

    y = (jnp.dot(hs_re[...], cre_ref[...], preferred_element_type=f32)
         - jnp.dot(hs_im[...], cim_ref[...], preferred_element_type=f32)
         + d_ref[...] * utn[...])
    if time_major:
        for j in range(nj):
            y_ref[j] = y[:, j * LANES:(j + 1) * LANES].reshape(tl, nb, LANES)
    else:
        yscr[...] = y
        for j in range(nj):
            for t in range(tl):
                perm[j, pl.ds(t, nb, stride=tl), :] = yscr[t * nb:(t + 1) * nb, j * LANES:(j + 1) * LANES]
            for b in range(nb):
                y_ref[j, b] = perm[j, b * tl:(b + 1) * tl, :]

    @pl.when(i == pl.num_programs(0) - 1)
    def _():
        hr_ref[...] = st_re[...]
        hi_ref[...] = st_im[...]


def _ssm_call(u4, h0_re, h0_im, ab_re, ab_im, b_re, b_im, c_re, c_im, d_skip, tl, time_major):
    nj = u4.shape[0]
    if time_major:
        l, nb = u4.shape[1], u4.shape[2]
        ublock = pl.BlockSpec((nj, tl, nb, LANES), lambda i: (0, i, 0, 0))
    else:
        nb, l = u4.shape[1], u4.shape[2]
        ublock = pl.BlockSpec((nj, nb, tl, LANES), lambda i: (0, 0, i, 0))
    rows = tl * nb
    const = lambda i: (0, 0)
    full = lambda a: pl.BlockSpec(a.shape, const)
    return pl.pallas_call(
        functools.partial(_ssm_kernel, nb=nb, tl=tl, time_major=time_major, chunk=512),
        grid=(l // tl,),
        in_specs=[ublock, full(h0_re), full(h0_im), full(ab_re), full(ab_im),
                  full(b_re), full(b_im), full(c_re), full(c_im), full(d_skip)],
        out_specs=[ublock, full(h0_re), full(h0_im)],
        out_shape=[jax.ShapeDtypeStruct(u4.shape, f32),
                   jax.ShapeDtypeStruct(h0_re.shape, f32), jax.ShapeDtypeStruct(h0_im.shape, f32)],
        scratch_shapes=[pltpu.VMEM((rows, D_SSM), f32),
                        pltpu.VMEM((rows, N_STATE), f32), pltpu.VMEM((rows, N_STATE), f32),
                        pltpu.VMEM((rows, N_STATE), bf16), pltpu.VMEM((rows, N_STATE), bf16),
                        pltpu.VMEM((nb, N_STATE), f32), pltpu.VMEM((nb, N_STATE), f32),
                        pltpu.VMEM((rows, D_SSM), f32),
                        pltpu.VMEM((nj, rows, LANES), f32)],
        compiler_params=_cparams(("arbitrary",)), name="ssm",
    )(u4, h0_re, h0_im, ab_re, ab_im, b_re, b_im, c_re, c_im, d_skip)


def _dec_kernel(pt_ref, qb_ref, snew_ref, lfnew_ref, vnew_ref, tri_ref, ones_ref,
                kc_ref, vc_ref, lfc_ref, o_ref,
                buf, lfbuf, sc, bias, acc, sem, lfsem, *, n_pages, cpp, n_seq, n_slots):
    n = pl.program_id(0)
    nc = n_pages // cpp
    per_token = 2 * nc
    ahead = n_slots - 1

    def start_chunk(src_ref, nn, c, slot):
        for j in range(cpp):
            pltpu.make_async_copy(src_ref.at[pt_ref[nn, c * cpp + j]], buf.at[slot, j], sem.at[slot]).start()

    def wait_chunk(src_ref, slot):
        for j in range(cpp):
            pltpu.make_async_copy(src_ref.at[0], buf.at[slot, j], sem.at[slot]).wait()

    def request(g):
        nn, lc, slot = g // per_token, g % per_token, g % n_slots

        @pl.when((g < n_seq * per_token) & (lc < nc))
        def _():
            start_chunk(kc_ref, nn, lc, slot)

        @pl.when((g < n_seq * per_token) & (lc >= nc))
        def _():
            start_chunk(vc_ref, nn, lc - nc, slot)

    def lf_copy(nn, pg, slot):
        return pltpu.make_async_copy(lfc_ref.at[pt_ref[nn, pg]], lfbuf.at[slot, pg], lfsem.at[slot])

    def start_lf(nn, slot):
        def body(pg, _):
            lf_copy(nn, pg, slot).start()
            return 0
        lax.fori_loop(0, n_pages, body, 0)

    @pl.when(n == 0)
    def _():
        start_lf(0, 0)
        for g in range(ahead):
            request(jnp.int32(g))

    lslot = n % 2

    @pl.when(n + 1 < n_seq)
    def _():
        start_lf(n + 1, 1 - lslot)

    def wait_lf(pg, _):
        pltpu.make_async_copy(lfc_ref.at[0], lfbuf.at[lslot, pg], lfsem.at[lslot]).wait()
        return 0
    lax.fori_loop(0, n_pages, wait_lf, 0)

    lf2 = lfbuf[lslot].reshape(n_pages * ATT_HEADS, LANES)
    inpage = jnp.dot(lf2, tri_ref[...], preferred_element_type=f32, precision=_HI)
    total = jnp.dot(lf2, ones_ref[...], preferred_element_type=f32, precision=_HI)
    bias[...] = inpage.reshape(n_pages, ATT_HEADS, LANES)
    sc[...] = total.reshape(n_pages, ATT_HEADS, LANES)
    lfnew = lfnew_ref[...]

    def suffix(t, run):
        pg = n_pages - 1 - t
        bias[pg] = bias[pg] + run + lfnew
        return run + sc[pg]
    lax.fori_loop(0, n_pages, suffix, jnp.zeros((ATT_HEADS, LANES), f32))

    head_tile = lax.broadcasted_iota(jnp.int32, (ATT_HEADS, LANES), 0)

    def k_chunk(c, _):
        g = n * per_token + c
        slot = g % n_slots
        request(g + ahead)
        wait_chunk(kc_ref, slot)

        def page(j, _):
            tile = jnp.zeros((ATT_HEADS, LANES), f32)
            for h in range(ATT_HEADS):
                s = jnp.sum(buf[slot, j, h] * qb_ref[h], axis=0, keepdims=True)
                tile = jnp.where(head_tile == h, s, tile)
            sc[c * cpp + j] = tile
            return 0
        lax.fori_loop(0, cpp, page, 0)
        return 0
    lax.fori_loop(0, nc, k_chunk, 0)

    s_all = sc[...] + bias[...]
    snew = snew_ref[...]
    m = jnp.max(jnp.max(s_all, axis=0), axis=-1, keepdims=True)
    m = jnp.maximum(m, snew)
    p_all = jnp.exp(s_all - m[None])
    sc[...] = p_all
    p_new = jnp.exp(snew - m)
    denom = jnp.sum(jnp.sum(p_all, axis=0), axis=-1, keepdims=True) + p_new

    acc[...] = jnp.zeros_like(acc)

    def v_chunk(c, _):
        g = n * per_token + nc + c
        slot = g % n_slots
        request(g + ahead)
        wait_chunk(vc_ref, slot)
        for h in range(ATT_HEADS):
            def page(j, a):
                pr = sc[c * cpp + j, pl.ds(h, 1), :]
                return a + buf[slot, j, h] * jnp.broadcast_to(pr, (HEAD_DIM, LANES))
            acc[h] = lax.fori_loop(0, cpp, page, acc[h])
        return 0
    lax.fori_loop(0, nc, v_chunk, 0)

    ones_row = jnp.ones((ATT_HEADS, LANES), f32)
    head_row = lax.broadcasted_iota(jnp.int32, (ATT_HEADS, HEAD_DIM), 0)
    past = jnp.zeros((ATT_HEADS, HEAD_DIM), f32)
    for h in range(ATT_HEADS):
        r = lax.dot_general(ones_row, acc[h], _NT, preferred_element_type=f32, precision=_HI)
        past = jnp.where(head_row == h, r, past)
    o_ref[...] = (past + p_new[:, :HEAD_DIM] * vnew_ref[...]) * (1.0 / denom[:, :HEAD_DIM])


def _dec_call(page_table, qb, snew_b, lfnew_b, vnew, kc, vc, lfc, cpp, n_slots=3):
    ns, n_pages = page_table.shape
    lane = jnp.arange(LANES)
    tri = (lane[:, None] > lane[None, :]).astype(f32)
    ones = jnp.ones((LANES, LANES), f32)
    per_n3 = lambda n_, pt: (n_, 0, 0)
    grid_spec = pltpu.PrefetchScalarGridSpec(
        num_scalar_prefetch=1,
        grid=(ns,),
        in_specs=[pl.BlockSpec((None, ATT_HEADS, HEAD_DIM, LANES), lambda n_, pt: (n_, 0, 0, 0)),
                  pl.BlockSpec((None, ATT_HEADS, LANES), per_n3),
                  pl.BlockSpec((None, ATT_HEADS, LANES), per_n3),
                  pl.BlockSpec((None, ATT_HEADS, HEAD_DIM), per_n3),
                  pl.BlockSpec((LANES, LANES), lambda n_, pt: (0, 0)),
                  pl.BlockSpec((LANES, LANES), lambda n_, pt: (0, 0)),
                  pl.BlockSpec(memory_space=pl.ANY),
                  pl.BlockSpec(memory_space=pl.ANY),
                  pl.BlockSpec(memory_space=pl.ANY)],
        out_specs=pl.BlockSpec((None, ATT_HEADS, HEAD_DIM), per_n3),
        scratch_shapes=[pltpu.VMEM((n_slots, cpp, ATT_HEADS, HEAD_DIM, LANES), f32),
                        pltpu.VMEM((2, n_pages, ATT_HEADS, LANES), f32),
                        pltpu.VMEM((n_pages, ATT_HEADS, LANES), f32),
                        pltpu.VMEM((n_pages, ATT_HEADS, LANES), f32),
                        pltpu.VMEM((ATT_HEADS, HEAD_DIM, LANES), f32),
                        pltpu.SemaphoreType.DMA((n_slots,)),
                        pltpu.SemaphoreType.DMA((2,))],
    )
    return pl.pallas_call(
        functools.partial(_dec_kernel, n_pages=n_pages, cpp=cpp, n_seq=ns, n_slots=n_slots),
        grid_spec=grid_spec,
        out_shape=jax.ShapeDtypeStruct((ns, ATT_HEADS, HEAD_DIM), f32),
        compiler_params=_cparams(("arbitrary",)), name="dec",
    )(page_table, qb, snew_b, lfnew_b, vnew, tri, ones, kc, vc, lfc)


def _post_kernel(*refs, gate_att):
    if gate_att:
        x_ref, att_ref, zatt_ref, y_ref, zssm_ref, gate_ref, wglu_ref, bglu_ref, wout_ref, gf_ref, o_ref = refs
        att = att_ref[...] * _silu(zatt_ref[...])
    else:
        x_ref, att_ref, y_ref, zssm_ref, gate_ref, wglu_ref, bglu_ref, wout_ref, gf_ref, o_ref = refs
        att = att_ref[...]
    y = jnp.concatenate([y_ref[j] for j in range(D_SSM // LANES)], axis=-1)
    g = jax.nn.gelu(y)
    glu = jax.nn.sigmoid(jnp.dot(g.astype(bf16), wglu_ref[...], preferred_element_type=f32) + bglu_ref[...])
    ssm = g * glu * _silu(zssm_ref[...])
    mix = jnp.concatenate([att.astype(bf16), ssm.astype(bf16)], axis=-1)
    x = x_ref[...]
    xo = x + gate_ref[...] * jnp.dot(mix, wout_ref[...], preferred_element_type=f32)
    o_ref[...] = xo * lax.rsqrt(jnp.mean(xo * xo, axis=-1, keepdims=True) + EPS) * gf_ref[...]


def _post_call(x, att, z_att, y4, z_ssm, gate, w_glu, b_glu, w_out, g_final, tm):
    n, l, d = x.shape
    row = lambda n_, i: (n_, i, 0)
    const2 = lambda n_, i: (0, 0)
    gate_spec = (pl.BlockSpec((None, 1, d), lambda n_, i: (n_, 0, 0)) if gate.shape[1] == 1
                 else pl.BlockSpec((None, tm, d), row))
    ins = [x, att] + ([z_att] if z_att is not None else []) + [y4, z_ssm, gate, w_glu, b_glu, w_out, g_final]
    in_specs = ([pl.BlockSpec((None, tm, d), row), pl.BlockSpec((None, tm, D_ATT), row)]
                + ([pl.BlockSpec((None, tm, D_ATT), row)] if z_att is not None else [])
                + [pl.BlockSpec((D_SSM // LANES, None, tm, LANES), lambda n_, i: (0, n_, i, 0)),
                   pl.BlockSpec((None, tm, D_SSM), row), gate_spec,
                   pl.BlockSpec(w_glu.shape, const2), pl.BlockSpec((1, D_SSM), const2),
                   pl.BlockSpec(w_out.shape, const2), pl.BlockSpec((1, d), const2)])
    return pl.pallas_call(
        functools.partial(_post_kernel, gate_att=z_att is not None),
        grid=(n, l // tm), in_specs=in_specs,
        out_specs=pl.BlockSpec((None, tm, d), row),
        out_shape=jax.ShapeDtypeStruct((n, l, d), f32),
        compiler_params=_cparams(("arbitrary", "arbitrary")), name="post",
    )(*ins)


def kernel(x_prompt, x_sample, c_prompt, c_sample, cache_k, cache_v, cache_logf, state_ssm_re, state_ssm_im,
           page_table, g_norm, w_ada, b_ada, w_in, b_fgate, a_re, a_im, log_dt, b_re, b_im, c_re, c_im, d_skip,
           w_glu, b_glu, w_out, g_final):
    n, l, d = x_prompt.shape
    ns = x_sample.shape[0]
    depth = w_in.shape[0]
    assert depth == 1 and x_sample.shape[1] == 1
    lyr = 0
    xs = x_sample.reshape(ns, d)

    splits = [0, D_ATT, 2 * D_ATT, 3 * D_ATT, 4 * D_ATT, 4 * D_ATT + ATT_HEADS,
              4 * D_ATT + ATT_HEADS + D_SSM, 4 * D_ATT + ATT_HEADS + 2 * D_SSM]
    w_t = jnp.swapaxes(w_in[lyr], 0, 1)
    wq, wk, wv, wz, wf, wu, wzs = [w_t[splits[i]:splits[i + 1]] for i in range(7)]
    w_nn = jnp.concatenate([wq, wz, wu, wzs], axis=0).T.astype(bf16)
    w_nt = jnp.concatenate([wk, wv, wf, jnp.zeros((NT_ROWS - 2 * D_ATT - ATT_HEADS, d), f32)], axis=0).astype(bf16)
    w_all = jnp.concatenate([wq, wk, wv, wz, wu, wzs, wf, jnp.zeros((LANES - ATT_HEADS, d), f32)], axis=0).T.astype(bf16)
    b_f = b_fgate[lyr]
    b_f_pad = jnp.concatenate([b_f, jnp.zeros((LANES - ATT_HEADS,), f32)])[None, :]
    seg = (jnp.arange(D_ATT)[:, None] // HEAD_DIM == jnp.arange(LANES)[None, :]).astype(f32)
    gn = g_norm[lyr][None, :]
    gf = g_final[None, :]
    w_glu_b = w_glu[lyr].astype(bf16)
    w_out_b = w_out[lyr].astype(bf16)
    b_glu2 = b_glu[lyr][None, :]

    mod = _ada_call(jnp.concatenate([c_prompt, c_sample], axis=0), w_ada[lyr], b_ada[lyr][None, :])
    shift_p, scale_p, gate_p = [mod[:n, k * d:(k + 1) * d].reshape(n, 1, d) for k in range(3)]
    shift_s, scale_s, gate_s = [mod[n:, k * d:(k + 1) * d] for k in range(3)]

    ab_re, ab_im, bb_re, bb_im = _ssm_param_call(a_re[lyr], a_im[lyr], log_dt[lyr], b_re[lyr], b_im[lyr])
    ab_re_row = ab_re.reshape(1, N_STATE)
    ab_im_row = ab_im.reshape(1, N_STATE)
    bd_re = _block_diag(bb_re).astype(bf16)
    bd_im = _block_diag(bb_im).astype(bf16)
    cd_re = _block_diag(jnp.swapaxes(c_re[lyr], 1, 2)).astype(bf16)
    cd_im = _block_diag(jnp.swapaxes(c_im[lyr], 1, 2)).astype(bf16)
    d_row = d_skip[lyr].reshape(1, D_SSM)

    tm = min(512, l)
    q, z_att, u4, z_ssm, kt, vt, lft = _pre_prompt_call(x_prompt, shift_p, scale_p, gn, w_nn, w_nt, b_f, tm)
    att = _att_call(q, kt, vt, lft, z_att, min(256, l))
    zeros_state = jnp.zeros((n, N_STATE), f32)
    y4, hp_re, hp_im = _ssm_call(u4, zeros_state, zeros_state, ab_re_row, ab_im_row, bd_re, bd_im, cd_re, cd_im,
                                 d_row, tl=min(32, l), time_major=False)
    y_prompt = _post_call(x_prompt, att, None, y4, z_ssm, gate_p, w_glu_b, b_glu2, w_out_b, gf, tm)

    heads_t = lambda t: jnp.transpose(t.reshape(n, ATT_HEADS, HEAD_DIM, l), (0, 3, 1, 2))[None]
    k_prompt = heads_t(kt)
    v_prompt = heads_t(vt)
    logf_prompt = jnp.transpose(lft, (0, 2, 1))[None]

    qs, k_s, v_s, zatt_s, u_s, zssm_s, lf_s, snew = _pre_sample_call(xs, shift_s, scale_s, gn, w_all, b_f_pad, seg)
    qb = jnp.broadcast_to(qs.reshape(ns, ATT_HEADS, HEAD_DIM, 1), (ns, ATT_HEADS, HEAD_DIM, LANES))
    snew_b = jnp.broadcast_to(snew[:, :ATT_HEADS, None], (ns, ATT_HEADS, LANES))
    lfnew_b = jnp.broadcast_to(lf_s[:, :ATT_HEADS, None], (ns, ATT_HEADS, LANES))
    kc = jnp.transpose(cache_k[lyr], (0, 2, 3, 1))
    vc = jnp.transpose(cache_v[lyr], (0, 2, 3, 1))
    lfc = jnp.transpose(cache_logf[lyr], (0, 2, 1))
    n_pages = page_table.shape[1]
    att_s = _dec_call(page_table, qb, snew_b, lfnew_b, v_s.reshape(ns, ATT_HEADS, HEAD_DIM), kc, vc, lfc,
                      cpp=min(16, n_pages // 2))

    u4s = jnp.transpose(u_s.reshape(1, ns, D_SSM // LANES, LANES), (2, 0, 1, 3))
    h0r = state_ssm_re[lyr].reshape(ns, N_STATE)
    h0i = state_ssm_im[lyr].reshape(ns, N_STATE)
    y4s, hs_re, hs_im = _ssm_call(u4s, h0r, h0i, ab_re_row, ab_im_row, bd_re, bd_im, cd_re, cd_im, d_row,
                                  tl=1, time_major=True)
    y_sample = _post_call(xs[None], att_s.reshape(1, ns, D_ATT), zatt_s[None], y4s, zssm_s[None], gate_s[None],
                          w_glu_b, b_glu2, w_out_b, gf, ns)

    st = lambda a, b: a.reshape(1, b, SSM_GROUPS, SSM_STATE)
    return (y_prompt, y_sample.reshape(ns, 1, d),
            k_prompt, v_prompt, logf_prompt, st(hp_re, n), st(hp_im, n),
            k_s.reshape(1, ns, 1, ATT_HEADS, HEAD_DIM), v_s.reshape(1, ns, 1, ATT_HEADS, HEAD_DIM),
            lf_s[:, :ATT_HEADS].reshape(1, ns, 1, ATT_HEADS), st(hs_re, ns), st(hs_im, ns))
```

```python
import functools
import math

import jax
import jax.numpy as jnp
from jax import lax
from jax.experimental import pallas as pl
from jax.experimental.pallas import tpu as pltpu

f32 = jnp.float32
bf16 = jnp.bfloat16

HEAD_DIM = 64
ATT_HEADS = 8
D_ATT = ATT_HEADS * HEAD_DIM
SSM_GROUP = 16
SSM_GROUPS = 32
SSM_STATE = 64
D_SSM = SSM_GROUP * SSM_GROUPS
N_STATE = SSM_GROUPS * SSM_STATE
EPS = 1e-6
QK_SCALE = HEAD_DIM ** -0.5
LOG2E = 1.0 / math.log(2.0)
LANES = 128
NT_ROWS = 2 * D_ATT + 16
VMEM_LIMIT = 56 * 1024 * 1024

_NT = (((1,), (1,)), ((), ()))
_HI = lax.Precision.HIGHEST


def _silu(x):
    return x * jax.nn.sigmoid(x)


def _log_sigmoid(x):
    return jnp.minimum(x, 0.0) - jnp.log1p(jnp.exp(-jnp.abs(x)))


def _modulated_norm(x, g, scale, shift):
    y = x * lax.rsqrt(jnp.mean(x * x, axis=-1, keepdims=True) + EPS)
    return (y * g) * (1.0 + scale) + shift


def _cparams(sem, vmem=VMEM_LIMIT):
    return pltpu.CompilerParams(dimension_semantics=sem, vmem_limit_bytes=vmem)


def _ada_kernel(c_ref, w_ref, b_ref, o_ref):
    s = _silu(c_ref[...]).astype(bf16)
    o_ref[...] = jnp.dot(s, w_ref[...].astype(bf16), preferred_element_type=f32) + b_ref[...]


def _ada_call(c_all, w_ada, b_ada):
    r, d = c_all.shape
    n_out = w_ada.shape[1]
    tn = 768
    return pl.pallas_call(
        _ada_kernel,
        grid=(n_out // tn,),
        in_specs=[pl.BlockSpec((r, d), lambda j: (0, 0)),
                  pl.BlockSpec((d, tn), lambda j: (0, j)),
                  pl.BlockSpec((1, tn), lambda j: (0, j))],
        out_specs=pl.BlockSpec((r, tn), lambda j: (0, j)),
        out_shape=jax.ShapeDtypeStruct((r, n_out), f32),
        compiler_params=_cparams(("arbitrary",)),
        name="ada",
    )(c_all, w_ada, b_ada)


def _ssm_param_kernel(are_ref, aim_ref, ldt_ref, bre_ref, bim_ref,
                      abr_ref, abi_ref, bbr_ref, bbi_ref):
    lr, li = are_ref[...], aim_ref[...]
    dt = jnp.exp(ldt_ref[...])
    mag = jnp.exp(lr * dt)
    abr = mag * jnp.cos(li * dt)
    abi = mag * jnp.sin(li * dt)
    xr, xi = abr - 1.0, abi
    den = lr * lr + li * li
    cr = (xr * lr + xi * li) / den
    ci = (xi * lr - xr * li) / den
    br, bi = bre_ref[...], bim_ref[...]
    abr_ref[...] = abr
    abi_ref[...] = abi
    bbr_ref[...] = cr * br - ci * bi
    bbi_ref[...] = cr * bi + ci * br


def _ssm_param_call(a_re, a_im, log_dt, b_re, b_im):
    g, p = a_re.shape
    hc = b_re.shape[-1]
    rep = lambda a: jnp.broadcast_to(a[:, None, :], (g, hc, p)).reshape(g * hc, p)
    ldt = jnp.broadcast_to(log_dt[:, None, None], (g, hc, p)).reshape(g * hc, p)
    bt = lambda b: jnp.swapaxes(b, 1, 2).reshape(g * hc, p)
    shp = jax.ShapeDtypeStruct((g * hc, p), f32)
    abr, abi, bbr, bbi = pl.pallas_call(
        _ssm_param_kernel, out_shape=[shp] * 4, name="ssm_par",
    )(rep(a_re), rep(a_im), ldt, bt(b_re), bt(b_im))
    first = lambda a: a.reshape(g, hc, p)[:, 0, :]
    return first(abr), first(abi), bbr.reshape(g, hc, p), bbi.reshape(g, hc, p)


def _block_diag(blocks):
    g, r, c = blocks.shape
    eye = jnp.eye(g, dtype=blocks.dtype)
    return (blocks[:, :, None, :] * eye[:, None, :, None]).reshape(g * r, g * c)


def _pre_prompt_kernel(x_ref, shift_ref, scale_ref, g_ref, wnn_ref, wnt_ref, bf_ref,
                       q_ref, zatt_ref, u_ref, zssm_ref, kt_ref, vt_ref, lft_ref):
    h = _modulated_norm(x_ref[...], g_ref[...], scale_ref[...], shift_ref[...])
    hb = h.astype(bf16)
    p = jnp.dot(hb, wnn_ref[...], preferred_element_type=f32)
    q_ref[...] = (p[:, :D_ATT] * (QK_SCALE * LOG2E)).astype(bf16)
    zatt_ref[...] = p[:, D_ATT:2 * D_ATT]
    for j in range(D_SSM // LANES):
        u_ref[j] = p[:, 2 * D_ATT + j * LANES:2 * D_ATT + (j + 1) * LANES]
    zssm_ref[...] = p[:, 2 * D_ATT + D_SSM:]
    pt = lax.dot_general(wnt_ref[...], hb, _NT, preferred_element_type=f32)
    kt_ref[...] = pt[:D_ATT]
    vt_ref[...] = pt[D_ATT:2 * D_ATT]
    lft_ref[...] = _log_sigmoid(pt[2 * D_ATT:2 * D_ATT + ATT_HEADS] + bf_ref[...])


def _pre_prompt_call(x, shift, scale, g_norm, w_nn, w_nt, b_f, tm):
    n, l, d = x.shape
    row = lambda n_, i: (n_, i, 0)
    col = lambda n_, i: (n_, 0, i)
    const2 = lambda n_, i: (0, 0)
    out_shape = [
        jax.ShapeDtypeStruct((n, l, D_ATT), bf16),
        jax.ShapeDtypeStruct((n, l, D_ATT), f32),
        jax.ShapeDtypeStruct((D_SSM // LANES, n, l, LANES), f32),
        jax.ShapeDtypeStruct((n, l, D_SSM), f32),
        jax.ShapeDtypeStruct((n, D_ATT, l), f32),
        jax.ShapeDtypeStruct((n, D_ATT, l), f32),
        jax.ShapeDtypeStruct((n, ATT_HEADS, l), f32),
    ]
    out_specs = [
        pl.BlockSpec((None, tm, D_ATT), row),
        pl.BlockSpec((None, tm, D_ATT), row),
        pl.BlockSpec((D_SSM // LANES, None, tm, LANES), lambda n_, i: (0, n_, i, 0)),
        pl.BlockSpec((None, tm, D_SSM), row),
        pl.BlockSpec((None, D_ATT, tm), col),
        pl.BlockSpec((None, D_ATT, tm), col),
        pl.BlockSpec((None, ATT_HEADS, tm), col),
    ]
    in_specs = [
        pl.BlockSpec((None, tm, d), row),
        pl.BlockSpec((None, 1, d), lambda n_, i: (n_, 0, 0)),
        pl.BlockSpec((None, 1, d), lambda n_, i: (n_, 0, 0)),
        pl.BlockSpec((1, d), const2),
        pl.BlockSpec(w_nn.shape, const2),
        pl.BlockSpec(w_nt.shape, const2),
        pl.BlockSpec((ATT_HEADS, tm), const2),
    ]
    return pl.pallas_call(
        _pre_prompt_kernel, grid=(n, l // tm), in_specs=in_specs, out_specs=out_specs,
        out_shape=out_shape, compiler_params=_cparams(("arbitrary", "arbitrary")), name="pre",
    )(x, shift, scale, g_norm, w_nn, w_nt, jnp.broadcast_to(b_f[:, None], (ATT_HEADS, tm)))


def _pre_sample_kernel(x_ref, shift_ref, scale_ref, g_ref, w_ref, bf_ref, seg_ref,
                       q_ref, k_ref, v_ref, zatt_ref, u_ref, zssm_ref, lf_ref, snew_ref):
    h = _modulated_norm(x_ref[...], g_ref[...], scale_ref[...], shift_ref[...])
    p = jnp.dot(h.astype(bf16), w_ref[...], preferred_element_type=f32)
    q = p[:, :D_ATT] * QK_SCALE
    k = p[:, D_ATT:2 * D_ATT]
    q_ref[...] = q
    k_ref[...] = k
    v_ref[...] = p[:, 2 * D_ATT:3 * D_ATT]
    zatt_ref[...] = p[:, 3 * D_ATT:4 * D_ATT]
    u_ref[...] = p[:, 4 * D_ATT:4 * D_ATT + D_SSM]
    zssm_ref[...] = p[:, 4 * D_ATT + D_SSM:4 * D_ATT + 2 * D_SSM]
    lf_ref[...] = _log_sigmoid(p[:, 4 * D_ATT + 2 * D_SSM:] + bf_ref[...])
    snew_ref[...] = jnp.dot(q * k, seg_ref[...], preferred_element_type=f32, precision=_HI)


def _pre_sample_call(x, shift, scale, g_norm, w_all, b_f_pad, seg):
    r, d = x.shape
    shp = lambda c: jax.ShapeDtypeStruct((r, c), f32)
    return pl.pallas_call(
        _pre_sample_kernel,
        out_shape=[shp(D_ATT), shp(D_ATT), shp(D_ATT), shp(D_ATT), shp(D_SSM), shp(D_SSM), shp(LANES), shp(LANES)],
        compiler_params=pltpu.CompilerParams(vmem_limit_bytes=VMEM_LIMIT), name="pre_s",
    )(x, shift, scale, g_norm, w_all, b_f_pad, seg)


def _split3(x):
    hi = x.astype(bf16).astype(f32)
    r = x - hi
    mid = r.astype(bf16).astype(f32)
    return hi, mid, r - mid


def _att_kernel(q_ref, kt_ref, vt_ref, lft_ref, z_ref, o_ref, ka, va, cum, qa, m_s, acc_s, *, tq, seq):
    qi = pl.program_id(1)
    aug = HEAD_DIM

    @pl.when(qi == 0)
    def _():
        c = lft_ref[...]
        lane = lax.broadcasted_iota(jnp.int32, c.shape, 1)
        d = 1
        while d < seq:
            c = c + jnp.where(lane >= d, pltpu.roll(c, d, axis=1), 0.0)
            d *= 2
        c = c * LOG2E
        cum[...] = jnp.zeros_like(cum)
        r16 = lax.broadcasted_iota(jnp.int32, (16, seq), 0)
        for h in range(ATT_HEADS):
            ka[h, 0:HEAD_DIM, :] = kt_ref[h * HEAD_DIM:(h + 1) * HEAD_DIM, :].astype(bf16)
            va[h, 0:HEAD_DIM, :] = vt_ref[h * HEAD_DIM:(h + 1) * HEAD_DIM, :].astype(bf16)
            hi, mid, lo = _split3(c[h:h + 1, :])
            for j, part in enumerate((hi, mid, lo)):
                cum[3 * h + j:3 * h + j + 1, :] = part
            ext = jnp.where(r16 == 0, -hi, jnp.where(r16 == 1, -mid, jnp.where(r16 == 2, -lo,
                            jnp.where(r16 < 6, 1.0, 0.0))))
            ka[h, aug:aug + 16, :] = ext.astype(bf16)
            ka[h, aug + 16:, :] = jnp.zeros((LANES - aug - 16, seq), bf16)
            va[h, HEAD_DIM:, :] = jnp.ones((LANES - HEAD_DIM, seq), bf16)

    q0 = pl.multiple_of(qi * tq, tq)
    cq_parts = cum[:, pl.ds(q0, tq)].T
    lane_q = lax.broadcasted_iota(jnp.int32, (tq, LANES), 1)
    ones_lanes = jnp.where(lane_q < aug + 3, 1.0, 0.0)
    for pr in range(ATT_HEADS // 2):
        qp = q_ref[:, pr * LANES:(pr + 1) * LANES].astype(f32)
        for hh in range(2):
            h = 2 * pr + hh
            qh = qp if hh == 0 else pltpu.roll(qp, HEAD_DIM, axis=1)
            cq = pltpu.roll(cq_parts, aug + 3 - 3 * h, axis=1)
            ext = jnp.where((lane_q >= aug + 3) & (lane_q < aug + 6), cq, ones_lanes)
            qa[h] = jnp.where(lane_q < HEAD_DIM, qh, ext).astype(bf16)

    row = lax.broadcasted_iota(jnp.int32, (tq, tq), 0)
    col = lax.broadcasted_iota(jnp.int32, (tq, tq), 1)
    causal = col <= row

    def block(kb, diagonal):
        k0 = pl.multiple_of(kb * tq, tq)
        for h in range(ATT_HEADS):
            s = jnp.dot(qa[h], ka[h, :, pl.ds(k0, tq)], preferred_element_type=f32)
            if diagonal:
                s = jnp.where(causal, s, -jnp.inf)
            rowmax = jnp.broadcast_to(jnp.max(s, axis=-1, keepdims=True), (tq, LANES))
            m_new = rowmax if diagonal else jnp.maximum(m_s[h], rowmax)
            p = [jnp.exp2(s[:, j * LANES:(j + 1) * LANES] - m_new) for j in range(tq // LANES)]
            p = jnp.concatenate(p, axis=1)
            pv = lax.dot_general(p.astype(bf16), va[h, :, pl.ds(k0, tq)], _NT, preferred_element_type=f32)
            acc_s[h] = pv if diagonal else jnp.exp2(m_s[h] - m_new) * acc_s[h] + pv
            m_s[h] = m_new

    def off_diagonal(kb, carry):
        block(kb, False)
        return carry

    block(qi, True)
    lax.fori_loop(0, qi, off_diagonal, 0)

    for pr in range(ATT_HEADS // 2):
        a0, a1 = acc_s[2 * pr], acc_s[2 * pr + 1]
        o0 = a0 * pltpu.roll(1.0 / a0, HEAD_DIM, axis=1)
        o1 = pltpu.roll(a1, HEAD_DIM, axis=1) * (1.0 / a1)
        o = jnp.where(lane_q < HEAD_DIM, o0, o1)
        cols = slice(pr * LANES, (pr + 1) * LANES)
        o_ref[:, cols] = (o * _silu(z_ref[:, cols])).astype(bf16)


def _att_call(q, kt, vt, lft, z_att, tq):
    n, l, _ = q.shape
    row = lambda n_, i: (n_, i, 0)
    whole = lambda n_, i: (n_, 0, 0)
    return pl.pallas_call(
        functools.partial(_att_kernel, tq=tq, seq=l),
        grid=(n, l // tq),
        in_specs=[pl.BlockSpec((None, tq, D_ATT), row),
                  pl.BlockSpec((None, D_ATT, l), whole),
                  pl.BlockSpec((None, D_ATT, l), whole),
                  pl.BlockSpec((None, ATT_HEADS, l), whole),
                  pl.BlockSpec((None, tq, D_ATT), row)],
        out_specs=pl.BlockSpec((None, tq, D_ATT), row),
        out_shape=jax.ShapeDtypeStruct((n, l, D_ATT), bf16),
        scratch_shapes=[pltpu.VMEM((ATT_HEADS, LANES, l), bf16), pltpu.VMEM((ATT_HEADS, LANES, l), bf16),
                        pltpu.VMEM((LANES, l), f32),
                        pltpu.VMEM((ATT_HEADS, tq, LANES), bf16),
                        pltpu.VMEM((ATT_HEADS, tq, LANES), f32),
                        pltpu.VMEM((ATT_HEADS, tq, LANES), f32)],
        compiler_params=_cparams(("arbitrary", "arbitrary")), name="att",
    )(q, kt, vt, lft, z_att)


def _ssm_kernel(u_ref, h0r_ref, h0i_ref, ar_ref, ai_ref, bre_ref, bim_ref, cre_ref, cim_ref, d_ref,
                y_ref, hr_ref, hi_ref,
                utn, bu_re, bu_im, hs_re, hs_im, st_re, st_im, yscr, perm, *, nb, tl, time_major, chunk):
    i = pl.program_id(0)
    nj = D_SSM // LANES

    @pl.when(i == 0)
    def _():
        st_re[...] = h0r_ref[...]
        st_im[...] = h0i_ref[...]

    for j in range(nj):
        cols = slice(j * LANES, (j + 1) * LANES)
        if time_major:
            utn[:, cols] = u_ref[j].reshape(tl * nb, LANES)
        else:
            for b in range(nb):
                perm[j, b * tl:(b + 1) * tl, :] = u_ref[j, b]
            for t in range(tl):
                utn[t * nb:(t + 1) * nb, cols] = perm[j, pl.ds(t, nb, stride=tl), :]

    for c in range(nj):
        cols = slice(c * chunk, (c + 1) * chunk)
        ucols = slice(c * LANES, (c + 1) * LANES)
        ub = utn[:, ucols].astype(bf16)
        bu_re[:, cols] = jnp.dot(ub, bre_ref[c], preferred_element_type=f32)
        bu_im[:, cols] = jnp.dot(ub, bim_ref[c], preferred_element_type=f32)
        ar = jnp.broadcast_to(ar_ref[:, cols], (nb, chunk))
        ai = jnp.broadcast_to(ai_ref[:, cols], (nb, chunk))
        hr, hi = st_re[:, cols], st_im[:, cols]
        for t in range(tl):
            rows = slice(t * nb, (t + 1) * nb)
            hr, hi = (ar * hr - ai * hi + bu_re[rows, cols], ar * hi + ai * hr + bu_im[rows, cols])
            hs_re[rows, cols] = hr.astype(bf16)
            hs_im[rows, cols] = hi.astype(bf16)
        st_re[:, cols] = hr
        st_im[:, cols] = hi
        yscr[:, ucols] = (jnp.dot(hs_re[:, cols], cre_ref[c], preferred_element_type=f32)
                          - jnp.dot(hs_im[:, cols], cim_ref[c], preferred_element_type=f32)
                          + d_ref[:, ucols] * utn[:, ucols])
    if time_major:
        for j in range(nj):
            y_ref[j] = yscr[:, j * LANES:(j + 1) * LANES].reshape(tl, nb, LANES)
    else:
        for j in range(nj):
            for t in range(tl):
                perm[j, pl.ds(t, nb, stride=tl), :] = yscr[t * nb:(t + 1) * nb, j * LANES:(j + 1) * LANES]
            for b in range(nb):
                y_ref[j, b] = perm[j, b * tl:(b + 1) * tl, :]

    @pl.when(i == pl.num_programs(0) - 1)
    def _():
        hr_ref[...] = st_re[...]
        hi_ref[...] = st_im[...]


def _ssm_call(u4, h0_re, h0_im, ab_re, ab_im, b_re, b_im, c_re, c_im, d_skip, tl, time_major):
    nj = u4.shape[0]
    if time_major:
        l, nb = u4.shape[1], u4.shape[2]
        ublock = pl.BlockSpec((nj, tl, nb, LANES), lambda i: (0, i, 0, 0))
    else:
        nb, l = u4.shape[1], u4.shape[2]
        ublock = pl.BlockSpec((nj, nb, tl, LANES), lambda i: (0, 0, i, 0))
    rows = tl * nb
    full = lambda a: pl.BlockSpec(a.shape, lambda i, nd=a.ndim: (0,) * nd)
    return pl.pallas_call(
        functools.partial(_ssm_kernel, nb=nb, tl=tl, time_major=time_major, chunk=N_STATE // nj),
        grid=(l // tl,),
        in_specs=[ublock, full(h0_re), full(h0_im), full(ab_re), full(ab_im),
                  full(b_re), full(b_im), full(c_re), full(c_im), full(d_skip)],
        out_specs=[ublock, full(h0_re), full(h0_im)],
        out_shape=[jax.ShapeDtypeStruct(u4.shape, f32),
                   jax.ShapeDtypeStruct(h0_re.shape, f32), jax.ShapeDtypeStruct(h0_im.shape, f32)],
        scratch_shapes=[pltpu.VMEM((rows, D_SSM), f32),
                        pltpu.VMEM((rows, N_STATE), f32), pltpu.VMEM((rows, N_STATE), f32),
                        pltpu.VMEM((rows, N_STATE), bf16), pltpu.VMEM((rows, N_STATE), bf16),
                        pltpu.VMEM((nb, N_STATE), f32), pltpu.VMEM((nb, N_STATE), f32),
                        pltpu.VMEM((rows, D_SSM), f32),
                        pltpu.VMEM((nj, rows, LANES), f32)],
        compiler_params=_cparams(("arbitrary",)), name="ssm",
    )(u4, h0_re, h0_im, ab_re, ab_im, b_re, b_im, c_re, c_im, d_skip)


def _dec_kernel(pt_ref, qb_ref, snew_ref, lfnew_ref, vnew_ref, tri_ref, ones_ref,
                kc_ref, vc_ref, lfc_ref, o_ref,
                buf, lfbuf, sc, bias, acc, sem, lfsem, *, n_pages, cpp, n_seq, n_slots):
    n = pl.program_id(0)
    nc = n_pages // cpp
    per_token = 2 * nc
    ahead = n_slots - 1

    def start_chunk(src_ref, nn, c, slot):
        for j in range(cpp):
            pltpu.make_async_copy(src_ref.at[pt_ref[nn, c * cpp + j]], buf.at[slot, j], sem.at[slot]).start()

    def wait_chunk(src_ref, slot):
        for j in range(cpp):
            pltpu.make_async_copy(src_ref.at[0], buf.at[slot, j], sem.at[slot]).wait()

    def request(g):
        nn, lc, slot = g // per_token, g % per_token, g % n_slots

        @pl.when((g < n_seq * per_token) & (lc < nc))
        def _():
            start_chunk(kc_ref, nn, lc, slot)

        @pl.when((g < n_seq * per_token) & (lc >= nc))
        def _():
            start_chunk(vc_ref, nn, lc - nc, slot)

    def lf_copy(nn, pg, slot):
        return pltpu.make_async_copy(lfc_ref.at[pt_ref[nn, pg]], lfbuf.at[slot, pg], lfsem.at[slot])

    def start_lf(nn, slot):
        def body(pg, _):
            lf_copy(nn, pg, slot).start()
            return 0
        lax.fori_loop(0, n_pages, body, 0)

    @pl.when(n == 0)
    def _():
        start_lf(0, 0)
        for g in range(ahead):
            request(jnp.int32(g))

    lslot = n % 2

    @pl.when(n + 1 < n_seq)
    def _():
        start_lf(n + 1, 1 - lslot)

    def wait_lf(pg, _):
        pltpu.make_async_copy(lfc_ref.at[0], lfbuf.at[lslot, pg], lfsem.at[lslot]).wait()
        return 0
    lax.fori_loop(0, n_pages, wait_lf, 0)

    lf2 = lfbuf[lslot].reshape(n_pages * ATT_HEADS, LANES)
    inpage = jnp.dot(lf2, tri_ref[...], preferred_element_type=f32, precision=_HI)
    total = jnp.dot(lf2, ones_ref[...], preferred_element_type=f32, precision=_HI)
    bias[...] = inpage.reshape(n_pages, ATT_HEADS, LANES)
    sc[...] = total.reshape(n_pages, ATT_HEADS, LANES)
    lfnew = lfnew_ref[...]

    def suffix(t, run):
        pg = n_pages - 1 - t
        bias[pg] = bias[pg] + run + lfnew
        return run + sc[pg]
    lax.fori_loop(0, n_pages, suffix, jnp.zeros((ATT_HEADS, LANES), f32))

    head_tile = lax.broadcasted_iota(jnp.int32, (ATT_HEADS, LANES), 0)

    def k_chunk(c, _):
        g = n * per_token + c
        slot = g % n_slots
        request(g + ahead)
        wait_chunk(kc_ref, slot)

        def page(j, _):
            tile = jnp.zeros((ATT_HEADS, LANES), f32)
            for h in range(ATT_HEADS):
                s = jnp.sum(buf[slot, j, h] * qb_ref[h], axis=0, keepdims=True)
                tile = jnp.where(head_tile == h, s, tile)
            sc[c * cpp + j] = tile
            return 0
        lax.fori_loop(0, cpp, page, 0)
        return 0
    lax.fori_loop(0, nc, k_chunk, 0)

    s_all = sc[...] + bias[...]
    snew = snew_ref[...]
    m = jnp.max(jnp.max(s_all, axis=0), axis=-1, keepdims=True)
    m = jnp.maximum(m, snew)
    p_all = jnp.exp(s_all - m[None])
    sc[...] = p_all
    p_new = jnp.exp(snew - m)
    denom = jnp.sum(jnp.sum(p_all, axis=0), axis=-1, keepdims=True) + p_new

    acc[...] = jnp.zeros_like(acc)

    def v_chunk(c, _):
        g = n * per_token + nc + c
        slot = g % n_slots
        request(g + ahead)
        wait_chunk(vc_ref, slot)
        for h in range(ATT_HEADS):
            def page(j, a):
                pr = sc[c * cpp + j, pl.ds(h, 1), :]
                return a + buf[slot, j, h] * jnp.broadcast_to(pr, (HEAD_DIM, LANES))
            acc[h] = lax.fori_loop(0, cpp, page, acc[h])
        return 0
    lax.fori_loop(0, nc, v_chunk, 0)

    ones_row = jnp.ones((ATT_HEADS, LANES), f32)
    head_row = lax.broadcasted_iota(jnp.int32, (ATT_HEADS, HEAD_DIM), 0)
    past = jnp.zeros((ATT_HEADS, HEAD_DIM), f32)
    for h in range(ATT_HEADS):
        r = lax.dot_general(ones_row, acc[h], _NT, preferred_element_type=f32, precision=_HI)
        past = jnp.where(head_row == h, r, past)
    o_ref[...] = (past + p_new[:, :HEAD_DIM] * vnew_ref[...]) * (1.0 / denom[:, :HEAD_DIM])


def _dec_call(page_table, qb, snew_b, lfnew_b, vnew, kc, vc, lfc, cpp, n_slots=4):
    ns, n_pages = page_table.shape
    lane = jnp.arange(LANES)
    tri = (lane[:, None] > lane[None, :]).astype(f32)
    ones = jnp.ones((LANES, LANES), f32)
    per_n3 = lambda n_, pt: (n_, 0, 0)
    grid_spec = pltpu.PrefetchScalarGridSpec(
        num_scalar_prefetch=1,
        grid=(ns,),
        in_specs=[pl.BlockSpec((None, ATT_HEADS, HEAD_DIM, LANES), lambda n_, pt: (n_, 0, 0, 0)),
                  pl.BlockSpec((None, ATT_HEADS, LANES), per_n3),
                  pl.BlockSpec((None, ATT_HEADS, LANES), per_n3),
                  pl.BlockSpec((None, ATT_HEADS, HEAD_DIM), per_n3),
                  pl.BlockSpec((LANES, LANES), lambda n_, pt: (0, 0)),
                  pl.BlockSpec((LANES, LANES), lambda n_, pt: (0, 0)),
                  pl.BlockSpec(memory_space=pl.ANY),
                  pl.BlockSpec(memory_space=pl.ANY),
                  pl.BlockSpec(memory_space=pl.ANY)],
        out_specs=pl.BlockSpec((None, ATT_HEADS, HEAD_DIM), per_n3),
        scratch_shapes=[pltpu.VMEM((n_slots, cpp, ATT_HEADS, HEAD_DIM, LANES), f32),
                        pltpu.VMEM((2, n_pages, ATT_HEADS, LANES), f32),
                        pltpu.VMEM((n_pages, ATT_HEADS, LANES), f32),
                        pltpu.VMEM((n_pages, ATT_HEADS, LANES), f32),
                        pltpu.VMEM((ATT_HEADS, HEAD_DIM, LANES), f32),
                        pltpu.SemaphoreType.DMA((n_slots,)),
                        pltpu.SemaphoreType.DMA((2,))],
    )
    return pl.pallas_call(
        functools.partial(_dec_kernel, n_pages=n_pages, cpp=cpp, n_seq=ns, n_slots=n_slots),
        grid_spec=grid_spec,
        out_shape=jax.ShapeDtypeStruct((ns, ATT_HEADS, HEAD_DIM), f32),
        compiler_params=_cparams(("arbitrary",)), name="dec",
    )(page_table, qb, snew_b, lfnew_b, vnew, tri, ones, kc, vc, lfc)


def _post_kernel(*refs, gate_att):
    if gate_att:
        x_ref, att_ref, zatt_ref, y_ref, zssm_ref, gate_ref, wglu_ref, bglu_ref, wout_ref, gf_ref, o_ref = refs
        att = att_ref[...] * _silu(zatt_ref[...])
    else:
        x_ref, att_ref, y_ref, zssm_ref, gate_ref, wglu_ref, bglu_ref, wout_ref, gf_ref, o_ref = refs
        att = att_ref[...]
    y = jnp.concatenate([y_ref[j] for j in range(D_SSM // LANES)], axis=-1)
    g = jax.nn.gelu(y)
    glu = jax.nn.sigmoid(jnp.dot(g.astype(bf16), wglu_ref[...], preferred_element_type=f32) + bglu_ref[...])
    ssm = g * glu * _silu(zssm_ref[...])
    mix = jnp.concatenate([att.astype(bf16), ssm.astype(bf16)], axis=-1)
    x = x_ref[...]
    xo = x + gate_ref[...] * jnp.dot(mix, wout_ref[...], preferred_element_type=f32)
    o_ref[...] = xo * lax.rsqrt(jnp.mean(xo * xo, axis=-1, keepdims=True) + EPS) * gf_ref[...]


def _post_call(x, att, z_att, y4, z_ssm, gate, w_glu, b_glu, w_out, g_final, tm):
    n, l, d = x.shape
    row = lambda n_, i: (n_, i, 0)
    const2 = lambda n_, i: (0, 0)
    gate_spec = (pl.BlockSpec((None, 1, d), lambda n_, i: (n_, 0, 0)) if gate.shape[1] == 1
                 else pl.BlockSpec((None, tm, d), row))
    ins = [x, att] + ([z_att] if z_att is not None else []) + [y4, z_ssm, gate, w_glu, b_glu, w_out, g_final]
    in_specs = ([pl.BlockSpec((None, tm, d), row), pl.BlockSpec((None, tm, D_ATT), row)]
                + ([pl.BlockSpec((None, tm, D_ATT), row)] if z_att is not None else [])
                + [pl.BlockSpec((D_SSM // LANES, None, tm, LANES), lambda n_, i: (0, n_, i, 0)),
                   pl.BlockSpec((None, tm, D_SSM), row), gate_spec,
                   pl.BlockSpec(w_glu.shape, const2), pl.BlockSpec((1, D_SSM), const2),
                   pl.BlockSpec(w_out.shape, const2), pl.BlockSpec((1, d), const2)])
    return pl.pallas_call(
        functools.partial(_post_kernel, gate_att=z_att is not None),
        grid=(n, l // tm), in_specs=in_specs,
        out_specs=pl.BlockSpec((None, tm, d), row),
        out_shape=jax.ShapeDtypeStruct((n, l, d), f32),
        compiler_params=_cparams(("arbitrary", "arbitrary")), name="post",
    )(*ins)


def kernel(x_prompt, x_sample, c_prompt, c_sample, cache_k, cache_v, cache_logf, state_ssm_re, state_ssm_im,
           page_table, g_norm, w_ada, b_ada, w_in, b_fgate, a_re, a_im, log_dt, b_re, b_im, c_re, c_im, d_skip,
           w_glu, b_glu, w_out, g_final):
    n, l, d = x_prompt.shape
    ns = x_sample.shape[0]
    depth = w_in.shape[0]
    assert depth == 1 and x_sample.shape[1] == 1
    lyr = 0
    xs = x_sample.reshape(ns, d)

    splits = [0, D_ATT, 2 * D_ATT, 3 * D_ATT, 4 * D_ATT, 4 * D_ATT + ATT_HEADS,
              4 * D_ATT + ATT_HEADS + D_SSM, 4 * D_ATT + ATT_HEADS + 2 * D_SSM]
    w_t = jnp.swapaxes(w_in[lyr], 0, 1)
    wq, wk, wv, wz, wf, wu, wzs = [w_t[splits[i]:splits[i + 1]] for i in range(7)]
    w_nn = jnp.concatenate([wq, wz, wu, wzs], axis=0).T.astype(bf16)
    w_nt = jnp.concatenate([wk, wv, wf, jnp.zeros((NT_ROWS - 2 * D_ATT - ATT_HEADS, d), f32)], axis=0).astype(bf16)
    w_all = jnp.concatenate([wq, wk, wv, wz, wu, wzs, wf, jnp.zeros((LANES - ATT_HEADS, d), f32)], axis=0).T.astype(bf16)
    b_f = b_fgate[lyr]
    b_f_pad = jnp.concatenate([b_f, jnp.zeros((LANES - ATT_HEADS,), f32)])[None, :]
    seg = (jnp.arange(D_ATT)[:, None] // HEAD_DIM == jnp.arange(LANES)[None, :]).astype(f32)
    gn = g_norm[lyr][None, :]
    gf = g_final[None, :]
    w_glu_b = w_glu[lyr].astype(bf16)
    w_out_b = w_out[lyr].astype(bf16)
    b_glu2 = b_glu[lyr][None, :]

    mod = _ada_call(jnp.concatenate([c_prompt, c_sample], axis=0), w_ada[lyr], b_ada[lyr][None, :])
    shift_p, scale_p, gate_p = [mod[:n, k * d:(k + 1) * d].reshape(n, 1, d) for k in range(3)]
    shift_s, scale_s, gate_s = [mod[n:, k * d:(k + 1) * d] for k in range(3)]

    ab_re, ab_im, bb_re, bb_im = _ssm_param_call(a_re[lyr], a_im[lyr], log_dt[lyr], b_re[lyr], b_im[lyr])
    ab_re_row = ab_re.reshape(1, N_STATE)
    ab_im_row = ab_im.reshape(1, N_STATE)
    slabs = D_SSM // LANES
    slab_diag = lambda w: jnp.stack([_block_diag(b) for b in w.reshape((slabs, SSM_GROUPS // slabs) + w.shape[1:])])
    bd_re = slab_diag(bb_re).astype(bf16)
    bd_im = slab_diag(bb_im).astype(bf16)
    cd_re = slab_diag(jnp.swapaxes(c_re[lyr], 1, 2)).astype(bf16)
    cd_im = slab_diag(jnp.swapaxes(c_im[lyr], 1, 2)).astype(bf16)
    d_row = d_skip[lyr].reshape(1, D_SSM)

    tm = min(512, l)
    q, z_att, u4, z_ssm, kt, vt, lft = _pre_prompt_call(x_prompt, shift_p, scale_p, gn, w_nn, w_nt, b_f, tm)
    att = _att_call(q, kt, vt, lft, z_att, min(256, l))
    zeros_state = jnp.zeros((n, N_STATE), f32)
    y4, hp_re, hp_im = _ssm_call(u4, zeros_state, zeros_state, ab_re_row, ab_im_row, bd_re, bd_im, cd_re, cd_im,
                                 d_row, tl=min(32, l), time_major=False)
    y_prompt = _post_call(x_prompt, att, None, y4, z_ssm, gate_p, w_glu_b, b_glu2, w_out_b, gf, tm)

    heads_t = lambda t: jnp.transpose(t.reshape(n, ATT_HEADS, HEAD_DIM, l), (0, 3, 1, 2))[None]
    k_prompt = heads_t(kt)
    v_prompt = heads_t(vt)
    logf_prompt = jnp.transpose(lft, (0, 2, 1))[None]

    qs, k_s, v_s, zatt_s, u_s, zssm_s, lf_s, snew = _pre_sample_call(xs, shift_s, scale_s, gn, w_all, b_f_pad, seg)
    qb = jnp.broadcast_to(qs.reshape(ns, ATT_HEADS, HEAD_DIM, 1), (ns, ATT_HEADS, HEAD_DIM, LANES))
    snew_b = jnp.broadcast_to(snew[:, :ATT_HEADS, None], (ns, ATT_HEADS, LANES))
    lfnew_b = jnp.broadcast_to(lf_s[:, :ATT_HEADS, None], (ns, ATT_HEADS, LANES))
    kc = jnp.transpose(cache_k[lyr], (0, 2, 3, 1))
    vc = jnp.transpose(cache_v[lyr], (0, 2, 3, 1))
    lfc = jnp.transpose(cache_logf[lyr], (0, 2, 1))
    n_pages = page_table.shape[1]
    att_s = _dec_call(page_table, qb, snew_b, lfnew_b, v_s.reshape(ns, ATT_HEADS, HEAD_DIM), kc, vc, lfc,
                      cpp=min(16, n_pages // 2))

    u4s = jnp.transpose(u_s.reshape(1, ns, D_SSM // LANES, LANES), (2, 0, 1, 3))
    h0r = state_ssm_re[lyr].reshape(ns, N_STATE)
    h0i = state_ssm_im[lyr].reshape(ns, N_STATE)
    y4s, hs_re, hs_im = _ssm_call(u4s, h0r, h0i, ab_re_row, ab_im_row, bd_re, bd_im, cd_re, cd_im, d_row,
                                  tl=1, time_major=True)
    y_sample = _post_call(xs[None], att_s.reshape(1, ns, D_ATT), zatt_s[None], y4s, zssm_s[None], gate_s[None],
                          w_glu_b, b_glu2, w_out_b, gf, ns)

    st = lambda a, b: a.reshape(1, b, SSM_GROUPS, SSM_STATE)
    return (y_prompt, y_sample.reshape(ns, 1, d),
            k_prompt, v_prompt, logf_prompt, st(hp_re, n), st(hp_im, n),
            k_s.reshape(1, ns, 1, ATT_HEADS, HEAD_DIM), v_s.reshape(1, ns, 1, ATT_HEADS, HEAD_DIM),
            lf_s[:, :ATT_HEADS].reshape(1, ns, 1, ATT_HEADS), st(hs_re, ns), st(hs_im, ns))
```

```python
import functools
import math

import jax
import jax.numpy as jnp
from jax import lax
from jax.experimental import pallas as pl
from jax.experimental.pallas import tpu as pltpu

f32 = jnp.float32
bf16 = jnp.bfloat16

HEAD_DIM = 64
ATT_HEADS = 8
D_ATT = ATT_HEADS * HEAD_DIM
SSM_GROUP = 16
SSM_GROUPS = 32
SSM_STATE = 64
D_SSM = SSM_GROUP * SSM_GROUPS
N_STATE = SSM_GROUPS * SSM_STATE
EPS = 1e-6
QK_SCALE = HEAD_DIM ** -0.5
LOG2E = 1.0 / math.log(2.0)
LANES = 128
NT_ROWS = 2 * D_ATT + 16
VMEM_LIMIT = 56 * 1024 * 1024

_NT = (((1,), (1,)), ((), ()))
_HI = lax.Precision.HIGHEST


def _silu(x):
    return x * jax.nn.sigmoid(x)


def _log_sigmoid(x):
    return jnp.minimum(x, 0.0) - jnp.log1p(jnp.exp(-jnp.abs(x)))


def _modulated_norm(x, g, scale, shift):
    y = x * lax.rsqrt(jnp.mean(x * x, axis=-1, keepdims=True) + EPS)
    return (y * g) * (1.0 + scale) + shift


def _cparams(sem, vmem=VMEM_LIMIT):
    return pltpu.CompilerParams(dimension_semantics=sem, vmem_limit_bytes=vmem)


def _ada_kernel(c_ref, w_ref, b_ref, o_ref):
    s = _silu(c_ref[...]).astype(bf16)
    o_ref[...] = jnp.dot(s, w_ref[...].astype(bf16), preferred_element_type=f32) + b_ref[...]


def _ada_call(c_all, w_ada, b_ada):
    r, d = c_all.shape
    n_out = w_ada.shape[1]
    tn = 768
    return pl.pallas_call(
        _ada_kernel,
        grid=(n_out // tn,),
        in_specs=[pl.BlockSpec((r, d), lambda j: (0, 0)),
                  pl.BlockSpec((d, tn), lambda j: (0, j)),
                  pl.BlockSpec((1, tn), lambda j: (0, j))],
        out_specs=pl.BlockSpec((r, tn), lambda j: (0, j)),
        out_shape=jax.ShapeDtypeStruct((r, n_out), f32),
        compiler_params=_cparams(("arbitrary",)),
        name="ada",
    )(c_all, w_ada, b_ada)


def _ssm_param_kernel(are_ref, aim_ref, ldt_ref, bre_ref, bim_ref,
                      abr_ref, abi_ref, bbr_ref, bbi_ref):
    lr, li = are_ref[...], aim_ref[...]
    dt = jnp.exp(ldt_ref[...])
    mag = jnp.exp(lr * dt)
    abr = mag * jnp.cos(li * dt)
    abi = mag * jnp.sin(li * dt)
    xr, xi = abr - 1.0, abi
    den = lr * lr + li * li
    cr = (xr * lr + xi * li) / den
    ci = (xi * lr - xr * li) / den
    br, bi = bre_ref[...], bim_ref[...]
    abr_ref[...] = abr
    abi_ref[...] = abi
    bbr_ref[...] = cr * br - ci * bi
    bbi_ref[...] = cr * bi + ci * br


def _ssm_param_call(a_re, a_im, log_dt, b_re, b_im):
    g, p = a_re.shape
    hc = b_re.shape[-1]
    rep = lambda a: jnp.broadcast_to(a[:, None, :], (g, hc, p)).reshape(g * hc, p)
    ldt = jnp.broadcast_to(log_dt[:, None, None], (g, hc, p)).reshape(g * hc, p)
    bt = lambda b: jnp.swapaxes(b, 1, 2).reshape(g * hc, p)
    shp = jax.ShapeDtypeStruct((g * hc, p), f32)
    abr, abi, bbr, bbi = pl.pallas_call(
        _ssm_param_kernel, out_shape=[shp] * 4, name="ssm_par",
    )(rep(a_re), rep(a_im), ldt, bt(b_re), bt(b_im))
    first = lambda a: a.reshape(g, hc, p)[:, 0, :]
    return first(abr), first(abi), bbr.reshape(g, hc, p), bbi.reshape(g, hc, p)


def _block_diag(blocks):
    g, r, c = blocks.shape
    eye = jnp.eye(g, dtype=blocks.dtype)
    return (blocks[:, :, None, :] * eye[:, None, :, None]).reshape(g * r, g * c)


def _pre_prompt_kernel(x_ref, shift_ref, scale_ref, g_ref, wnn_ref, wnt_ref, bf_ref,
                       q_ref, zatt_ref, u_ref, zssm_ref, kt_ref, vt_ref, lft_ref, ktb_ref, vtb_ref):
    h = _modulated_norm(x_ref[...], g_ref[...], scale_ref[...], shift_ref[...])
    hb = h.astype(bf16)
    p = jnp.dot(hb, wnn_ref[...], preferred_element_type=f32)
    q_ref[...] = (p[:, :D_ATT] * (QK_SCALE * LOG2E)).astype(bf16)
    zatt_ref[...] = p[:, D_ATT:2 * D_ATT]
    for j in range(D_SSM // LANES):
        u_ref[j] = p[:, 2 * D_ATT + j * LANES:2 * D_ATT + (j + 1) * LANES]
    zssm_ref[...] = p[:, 2 * D_ATT + D_SSM:]
    pt = lax.dot_general(wnt_ref[...], hb, _NT, preferred_element_type=f32)
    kt_ref[...] = pt[:D_ATT]
    vt_ref[...] = pt[D_ATT:2 * D_ATT]
    ktb_ref[...] = pt[:D_ATT].astype(bf16)
    vtb_ref[...] = pt[D_ATT:2 * D_ATT].astype(bf16)
    lft_ref[...] = _log_sigmoid(pt[2 * D_ATT:2 * D_ATT + ATT_HEADS] + bf_ref[...])


def _pre_prompt_call(x, shift, scale, g_norm, w_nn, w_nt, b_f, tm):
    n, l, d = x.shape
    row = lambda n_, i: (n_, i, 0)
    col = lambda n_, i: (n_, 0, i)
    const2 = lambda n_, i: (0, 0)
    out_shape = [
        jax.ShapeDtypeStruct((n, l, D_ATT), bf16),
        jax.ShapeDtypeStruct((n, l, D_ATT), f32),
        jax.ShapeDtypeStruct((D_SSM // LANES, n, l, LANES), f32),
        jax.ShapeDtypeStruct((n, l, D_SSM), f32),
        jax.ShapeDtypeStruct((n, D_ATT, l), f32),
        jax.ShapeDtypeStruct((n, D_ATT, l), f32),
        jax.ShapeDtypeStruct((n, ATT_HEADS, l), f32),
        jax.ShapeDtypeStruct((n, D_ATT, l), bf16),
        jax.ShapeDtypeStruct((n, D_ATT, l), bf16),
    ]
    out_specs = [
        pl.BlockSpec((None, tm, D_ATT), row),
        pl.BlockSpec((None, tm, D_ATT), row),
        pl.BlockSpec((D_SSM // LANES, None, tm, LANES), lambda n_, i: (0, n_, i, 0)),
        pl.BlockSpec((None, tm, D_SSM), row),
        pl.BlockSpec((None, D_ATT, tm), col),
        pl.BlockSpec((None, D_ATT, tm), col),
        pl.BlockSpec((None, ATT_HEADS, tm), col),
        pl.BlockSpec((None, D_ATT, tm), col),
        pl.BlockSpec((None, D_ATT, tm), col),
    ]
    in_specs = [
        pl.BlockSpec((None, tm, d), row),
        pl.BlockSpec((None, 1, d), lambda n_, i: (n_, 0, 0)),
        pl.BlockSpec((None, 1, d), lambda n_, i: (n_, 0, 0)),
        pl.BlockSpec((1, d), const2),
        pl.BlockSpec(w_nn.shape, const2),
        pl.BlockSpec(w_nt.shape, const2),
        pl.BlockSpec((ATT_HEADS, tm), const2),
    ]
    return pl.pallas_call(
        _pre_prompt_kernel, grid=(n, l // tm), in_specs=in_specs, out_specs=out_specs,
        out_shape=out_shape, compiler_params=_cparams(("arbitrary", "arbitrary")), name="pre",
    )(x, shift, scale, g_norm, w_nn, w_nt, jnp.broadcast_to(b_f[:, None], (ATT_HEADS, tm)))


def _pre_sample_kernel(x_ref, shift_ref, scale_ref, g_ref, w_ref, bf_ref, seg_ref,
                       q_ref, k_ref, v_ref, zatt_ref, u_ref, zssm_ref, lf_ref, snew_ref):
    h = _modulated_norm(x_ref[...], g_ref[...], scale_ref[...], shift_ref[...])
    p = jnp.dot(h.astype(bf16), w_ref[...], preferred_element_type=f32)
    q = p[:, :D_ATT] * QK_SCALE
    k = p[:, D_ATT:2 * D_ATT]
    q_ref[...] = q
    k_ref[...] = k
    v_ref[...] = p[:, 2 * D_ATT:3 * D_ATT]
    zatt_ref[...] = p[:, 3 * D_ATT:4 * D_ATT]
    u_ref[...] = p[:, 4 * D_ATT:4 * D_ATT + D_SSM]
    zssm_ref[...] = p[:, 4 * D_ATT + D_SSM:4 * D_ATT + 2 * D_SSM]
    lf_ref[...] = _log_sigmoid(p[:, 4 * D_ATT + 2 * D_SSM:] + bf_ref[...])
    snew_ref[...] = jnp.dot(q * k, seg_ref[...], preferred_element_type=f32, precision=_HI)


def _pre_sample_call(x, shift, scale, g_norm, w_all, b_f_pad, seg):
    r, d = x.shape
    shp = lambda c: jax.ShapeDtypeStruct((r, c), f32)
    return pl.pallas_call(
        _pre_sample_kernel,
        out_shape=[shp(D_ATT), shp(D_ATT), shp(D_ATT), shp(D_ATT), shp(D_SSM), shp(D_SSM), shp(LANES), shp(LANES)],
        compiler_params=pltpu.CompilerParams(vmem_limit_bytes=VMEM_LIMIT), name="pre_s",
    )(x, shift, scale, g_norm, w_all, b_f_pad, seg)


def _split3(x):
    hi = x.astype(bf16).astype(f32)
    r = x - hi
    mid = r.astype(bf16).astype(f32)
    return hi, mid, r - mid


def _att_kernel(pt_ref, q_ref, kt_ref, vt_ref, lft_ref, z_ref,
                snew_ref, lfnew_ref, vnew_ref, tri_ref, ones_ref, qb_hbm, kc_hbm, vc_hbm, lfc_hbm,
                o_ref, od_ref,
                ka, va, cum, qa, m_s, acc_s,
                kbuf, vbuf, lfbuf, qbuf, sc, bias, tot, p_s, dacc, stat, ksem, vsem, lfsem, qsem,
                *, tq, seq, n_tok, n_pages, ppu, ring):
    n = pl.program_id(0)
    qi = pl.program_id(1)
    aug = HEAD_DIM
    n_tiles = seq // tq
    units = n_pages // ppu
    ahead = ring - 1
    k_end = n_tok * units
    v_lo, v_hi = units, (n_tok + 1) * units
    head_tile = lax.broadcasted_iota(jnp.int32, (ATT_HEADS, LANES), 0)

    def start_pages(src, buf, sem, tok, sub, slot):
        for j in range(ppu):
            pltpu.make_async_copy(src.at[pt_ref[tok, sub * ppu + j]], buf.at[slot, j], sem.at[slot]).start()

    def wait_pages(src, buf, sem, slot):
        for j in range(ppu):
            pltpu.make_async_copy(src.at[0], buf.at[slot, j], sem.at[slot]).wait()

    def request(w):
        tok, sub, slot = w // units, w % units, w % ring

        @pl.when(w < k_end)
        def _():
            start_pages(kc_hbm, kbuf, ksem, tok, sub, slot)

        @pl.when((w >= v_lo) & (w < v_hi))
        def _():
            start_pages(vc_hbm, vbuf, vsem, tok - 1, sub, slot)

    def start_token_inputs(t):
        slot = t % 2

        def body(pg, carry):
            pltpu.make_async_copy(lfc_hbm.at[pt_ref[t, pg]], lfbuf.at[slot, pg], lfsem.at[slot]).start()
            return carry
        lax.fori_loop(0, n_pages, body, 0)
        pltpu.make_async_copy(qb_hbm.at[t], qbuf.at[slot], qsem.at[slot]).start()

    def new_token(t):
        slot = t % 2

        @pl.when(t + 1 < n_tok)
        def _():
            start_token_inputs(t + 1)

        def wait_lf(pg, carry):
            pltpu.make_async_copy(lfc_hbm.at[0], lfbuf.at[slot, pg], lfsem.at[slot]).wait()
            return carry
        lax.fori_loop(0, n_pages, wait_lf, 0)
        pltpu.make_async_copy(qb_hbm.at[0], qbuf.at[slot], qsem.at[slot]).wait()
        lf2 = lfbuf[slot].reshape(n_pages * ATT_HEADS, LANES)
        inpage = jnp.dot(lf2, tri_ref[...], preferred_element_type=f32, precision=_HI)
        total = jnp.dot(lf2, ones_ref[...], preferred_element_type=f32, precision=_HI)
        bias[slot] = inpage.reshape(n_pages, ATT_HEADS, LANES)
        tot[...] = total.reshape(n_pages, ATT_HEADS, LANES)
        lfnew = lfnew_ref[t]

        def suffix(i, run):
            pg = n_pages - 1 - i
            bias[slot, pg] = bias[slot, pg] + run + lfnew
            return run + tot[pg]
        lax.fori_loop(0, n_pages, suffix, jnp.zeros((ATT_HEADS, LANES), f32))

    def token_softmax(t):
        s_all = sc[t % 2] + bias[t % 2]
        snew = snew_ref[t]
        m = jnp.max(jnp.max(s_all, axis=0), axis=-1, keepdims=True)
        m = jnp.maximum(m, snew)
        p_all = jnp.exp(s_all - m[None])
        p_s[...] = p_all
        p_new = jnp.exp(snew - m)
        stat[0] = p_new
        stat[1] = jnp.sum(jnp.sum(p_all, axis=0), axis=-1, keepdims=True) + p_new
        dacc[...] = jnp.zeros_like(dacc)

    def token_output(t):
        ones_row = jnp.ones((ATT_HEADS, LANES), f32)
        head_row = lax.broadcasted_iota(jnp.int32, (ATT_HEADS, HEAD_DIM), 0)
        past = jnp.zeros((ATT_HEADS, HEAD_DIM), f32)
        for h in range(ATT_HEADS):
            r = lax.dot_general(ones_row, dacc[h], _NT, preferred_element_type=f32, precision=_HI)
            past = jnp.where(head_row == h, r, past)
        od_ref[t] = (past + stat[0][:, :HEAD_DIM] * vnew_ref[t]) * (1.0 / stat[1][:, :HEAD_DIM])

    def unit_prologue(w):
        t = w // units

        @pl.when(w % units == 0)
        def _():
            @pl.when((t >= 2) & (t <= n_tok + 1))
            def _():
                token_output(t - 2)

            @pl.when((t >= 1) & (t <= n_tok))
            def _():
                token_softmax(t - 1)

            @pl.when(t < n_tok)
            def _():
                new_token(t)

        request(w + ahead)

        @pl.when(w < k_end)
        def _():
            wait_pages(kc_hbm, kbuf, ksem, w % ring)

        @pl.when((w >= v_lo) & (w < v_hi))
        def _():
            wait_pages(vc_hbm, vbuf, vsem, w % ring)

    def unit_compute(w):
        tok, sub, slot = w // units, w % units, w % ring
        qpar = jnp.minimum(tok, n_tok - 1) % 2
        for j in range(ppu):
            tile = jnp.zeros((ATT_HEADS, LANES), f32)
            for h in range(ATT_HEADS):
                s = jnp.sum(kbuf[slot, j, h] * qbuf[qpar, h], axis=0, keepdims=True)
                tile = jnp.where(head_tile == h, s, tile)
            sc[tok % 2, sub * ppu + j] = tile
        for h in range(ATT_HEADS):
            a = dacc[h]
            for j in range(ppu):
                pr = p_s[sub * ppu + j, pl.ds(h, 1), :]
                a = a + vbuf[slot, j, h] * jnp.broadcast_to(pr, (HEAD_DIM, LANES))
            dacc[h] = a

    @pl.when((n == 0) & (qi == 0))
    def _():
        vbuf[...] = jnp.zeros_like(vbuf)
        p_s[...] = jnp.zeros_like(p_s)
        dacc[...] = jnp.zeros_like(dacc)
        start_token_inputs(jnp.int32(0))
        for w0 in range(ahead):
            request(jnp.int32(w0))

    @pl.when(qi == 0)
    def _():
        c = lft_ref[...]
        lane = lax.broadcasted_iota(jnp.int32, c.shape, 1)
        d = 1
        while d < seq:
            c = c + jnp.where(lane >= d, pltpu.roll(c, d, axis=1), 0.0)
            d *= 2
        c = c * LOG2E
        cum[...] = jnp.zeros_like(cum)
        r16 = lax.broadcasted_iota(jnp.int32, (16, seq), 0)
        for h in range(ATT_HEADS):
            ka[h, 0:HEAD_DIM, :] = kt_ref[h * HEAD_DIM:(h + 1) * HEAD_DIM, :]
            va[h, 0:HEAD_DIM, :] = vt_ref[h * HEAD_DIM:(h + 1) * HEAD_DIM, :]
            hi, mid, lo = _split3(c[h:h + 1, :])
            for j, part in enumerate((hi, mid, lo)):
                cum[3 * h + j:3 * h + j + 1, :] = part
            ext = jnp.where(r16 == 0, -hi, jnp.where(r16 == 1, -mid, jnp.where(r16 == 2, -lo,
                            jnp.where(r16 < 6, 1.0, 0.0))))
            ka[h, aug:aug + 16, :] = ext.astype(bf16)
            ka[h, aug + 16:, :] = jnp.zeros((LANES - aug - 16, seq), bf16)
            va[h, HEAD_DIM:, :] = jnp.ones((LANES - HEAD_DIM, seq), bf16)

    q0 = pl.multiple_of(qi * tq, tq)
    cq_parts = cum[:, pl.ds(q0, tq)].T
    lane_q = lax.broadcasted_iota(jnp.int32, (tq, LANES), 1)
    ones_lanes = jnp.where(lane_q < aug + 3, 1.0, 0.0)
    for pr in range(ATT_HEADS // 2):
        qp = q_ref[:, pr * LANES:(pr + 1) * LANES].astype(f32)
        for hh in range(2):
            h = 2 * pr + hh
            qh = qp if hh == 0 else pltpu.roll(qp, HEAD_DIM, axis=1)
            cq = pltpu.roll(cq_parts, aug + 3 - 3 * h, axis=1)
            ext = jnp.where((lane_q >= aug + 3) & (lane_q < aug + 6), cq, ones_lanes)
            qa[h] = jnp.where(lane_q < HEAD_DIM, qh, ext).astype(bf16)

    row = lax.broadcasted_iota(jnp.int32, (tq, tq), 0)
    col = lax.broadcasted_iota(jnp.int32, (tq, tq), 1)
    causal = col <= row

    first_unit = n * (n_tiles * (n_tiles + 1) // 2) + (qi * (qi + 1)) // 2

    def block(kb, diagonal):
        w = first_unit if diagonal else first_unit + 1 + kb
        unit_prologue(w)
        unit_compute(w)
        k0 = pl.multiple_of(kb * tq, tq)
        for h in range(ATT_HEADS):
            s = jnp.dot(qa[h], ka[h, :, pl.ds(k0, tq)], preferred_element_type=f32)
            if diagonal:
                s = jnp.where(causal, s, -jnp.inf)
            rowmax = jnp.broadcast_to(jnp.max(s, axis=-1, keepdims=True), (tq, LANES))
            m_new = rowmax if diagonal else jnp.maximum(m_s[h], rowmax)
            p = [jnp.exp2(s[:, j * LANES:(j + 1) * LANES] - m_new) for j in range(tq // LANES)]
            p = jnp.concatenate(p, axis=1)
            pv = lax.dot_general(p.astype(bf16), va[h, :, pl.ds(k0, tq)], _NT, preferred_element_type=f32)
            acc_s[h] = pv if diagonal else jnp.exp2(m_s[h] - m_new) * acc_s[h] + pv
            m_s[h] = m_new

    def off_diagonal(kb, carry):
        block(kb, False)
        return carry

    block(qi, True)
    lax.fori_loop(0, qi, off_diagonal, 0)

    for pr in range(ATT_HEADS // 2):
        a0, a1 = acc_s[2 * pr], acc_s[2 * pr + 1]
        o0 = a0 * pltpu.roll(1.0 / a0, HEAD_DIM, axis=1)
        o1 = pltpu.roll(a1, HEAD_DIM, axis=1) * (1.0 / a1)
        o = jnp.where(lane_q < HEAD_DIM, o0, o1)
        cols = slice(pr * LANES, (pr + 1) * LANES)
        o_ref[:, cols] = (o * _silu(z_ref[:, cols])).astype(bf16)


def _att_call(q, ktb, vtb, lft, z_att, page_table, qb, snew_b, lfnew_b, vnew, kc, vc, lfc, tq, ppu, ring=4):
    n, l, _ = q.shape
    ns, n_pages = page_table.shape
    n_tiles = l // tq
    units = n_pages // ppu
    assert n * (n_tiles * (n_tiles + 1) // 2) >= (ns + 1) * units + 1, "too few key-block steps for the page units"
    lane = jnp.arange(LANES)
    tri = (lane[:, None] > lane[None, :]).astype(f32)
    ones = jnp.ones((LANES, LANES), f32)
    row = lambda n_, i, pt: (n_, i, 0)
    whole = lambda n_, i, pt: (n_, 0, 0)
    const = lambda nd: (lambda n_, i, pt: (0,) * nd)
    page = (ppu, ATT_HEADS, HEAD_DIM, LANES)
    grid_spec = pltpu.PrefetchScalarGridSpec(
        num_scalar_prefetch=1,
        grid=(n, n_tiles),
        in_specs=[pl.BlockSpec((None, tq, D_ATT), row),
                  pl.BlockSpec((None, D_ATT, l), whole),
                  pl.BlockSpec((None, D_ATT, l), whole),
                  pl.BlockSpec((None, ATT_HEADS, l), whole),
                  pl.BlockSpec((None, tq, D_ATT), row),
                  pl.BlockSpec(snew_b.shape, const(3)),
                  pl.BlockSpec(lfnew_b.shape, const(3)),
                  pl.BlockSpec(vnew.shape, const(3)),
                  pl.BlockSpec((LANES, LANES), const(2)),
                  pl.BlockSpec((LANES, LANES), const(2)),
                  pl.BlockSpec(memory_space=pl.ANY),
                  pl.BlockSpec(memory_space=pl.ANY),
                  pl.BlockSpec(memory_space=pl.ANY),
                  pl.BlockSpec(memory_space=pl.ANY)],
        out_specs=[pl.BlockSpec((None, tq, D_ATT), row),
                   pl.BlockSpec((ns, ATT_HEADS, HEAD_DIM), const(3))],
        scratch_shapes=[pltpu.VMEM((ATT_HEADS, LANES, l), bf16), pltpu.VMEM((ATT_HEADS, LANES, l), bf16),
                        pltpu.VMEM((LANES, l), f32),
                        pltpu.VMEM((ATT_HEADS, tq, LANES), bf16),
                        pltpu.VMEM((ATT_HEADS, tq, LANES), f32),
                        pltpu.VMEM((ATT_HEADS, tq, LANES), f32),
                        pltpu.VMEM((ring,) + page, f32),
                        pltpu.VMEM((ring,) + page, f32),
                        pltpu.VMEM((2, n_pages, ATT_HEADS, LANES), f32),
                        pltpu.VMEM((2, ATT_HEADS, HEAD_DIM, LANES), f32),
                        pltpu.VMEM((2, n_pages, ATT_HEADS, LANES), f32),
                        pltpu.VMEM((2, n_pages, ATT_HEADS, LANES), f32),
                        pltpu.VMEM((n_pages, ATT_HEADS, LANES), f32),
                        pltpu.VMEM((n_pages, ATT_HEADS, LANES), f32),
                        pltpu.VMEM((ATT_HEADS, HEAD_DIM, LANES), f32),
                        pltpu.VMEM((2, ATT_HEADS, LANES), f32),
                        pltpu.SemaphoreType.DMA((ring,)), pltpu.SemaphoreType.DMA((ring,)),
                        pltpu.SemaphoreType.DMA((2,)), pltpu.SemaphoreType.DMA((2,))],
    )
    return pl.pallas_call(
        functools.partial(_att_kernel, tq=tq, seq=l, n_tok=ns, n_pages=n_pages, ppu=ppu, ring=ring),
        grid_spec=grid_spec,
        out_shape=[jax.ShapeDtypeStruct((n, l, D_ATT), bf16),
                   jax.ShapeDtypeStruct((ns, ATT_HEADS, HEAD_DIM), f32)],
        compiler_params=_cparams(("arbitrary", "arbitrary")), name="att",
    )(page_table, q, ktb, vtb, lft, z_att, snew_b, lfnew_b, vnew, tri, ones, qb, kc, vc, lfc)


def _ssm_kernel(u_ref, h0r_ref, h0i_ref, ar_ref, ai_ref, bre_ref, bim_ref, cre_ref, cim_ref, d_ref,
                y_ref, hr_ref, hi_ref,
                utn, bu_re, bu_im, hs_re, hs_im, st_re, st_im, yscr, perm, *, nb, tl, time_major, chunk):
    i = pl.program_id(0)
    nj = D_SSM // LANES

    @pl.when(i == 0)
    def _():
        st_re[...] = h0r_ref[...]
        st_im[...] = h0i_ref[...]

    for j in range(nj):
        cols = slice(j * LANES, (j + 1) * LANES)
        if time_major:
            utn[:, cols] = u_ref[j].reshape(tl * nb, LANES)
        else:
            for b in range(nb):
                perm[j, b * tl:(b + 1) * tl, :] = u_ref[j, b]
            for t in range(tl):
                utn[t * nb:(t + 1) * nb, cols] = perm[j, pl.ds(t, nb, stride=tl), :]

    for c in range(nj):
        cols = slice(c * chunk, (c + 1) * chunk)
        ucols = slice(c * LANES, (c + 1) * LANES)
        ub = utn[:, ucols].astype(bf16)
        bu_re[:, cols] = jnp.dot(ub, bre_ref[c], preferred_element_type=f32)
        bu_im[:, cols] = jnp.dot(ub, bim_ref[c], preferred_element_type=f32)
        ar = jnp.broadcast_to(ar_ref[:, cols], (nb, chunk))
        ai = jnp.broadcast_to(ai_ref[:, cols], (nb, chunk))
        hr, hi = st_re[:, cols], st_im[:, cols]
        for t in range(tl):
            rows = slice(t * nb, (t + 1) * nb)
            hr, hi = (ar * hr - ai * hi + bu_re[rows, cols], ar * hi + ai * hr + bu_im[rows, cols])
            hs_re[rows, cols] = hr.astype(bf16)
            hs_im[rows, cols] = hi.astype(bf16)
        st_re[:, cols] = hr
        st_im[:, cols] = hi
        yscr[:, ucols] = (jnp.dot(hs_re[:, cols], cre_ref[c], preferred_element_type=f32)
                          - jnp.dot(hs_im[:, cols], cim_ref[c], preferred_element_type=f32)
                          + d_ref[:, ucols] * utn[:, ucols])
    if time_major:
        for j in range(nj):
            y_ref[j] = yscr[:, j * LANES:(j + 1) * LANES].reshape(tl, nb, LANES)
    else:
        for j in range(nj):
            for t in range(tl):
                perm[j, pl.ds(t, nb, stride=tl), :] = yscr[t * nb:(t + 1) * nb, j * LANES:(j + 1) * LANES]
            for b in range(nb):
                y_ref[j, b] = perm[j, b * tl:(b + 1) * tl, :]

    @pl.when(i == pl.num_programs(0) - 1)
    def _():
        hr_ref[...] = st_re[...]
        hi_ref[...] = st_im[...]


def _ssm_call(u4, h0_re, h0_im, ab_re, ab_im, b_re, b_im, c_re, c_im, d_skip, tl, time_major):
    nj = u4.shape[0]
    if time_major:
        l, nb = u4.shape[1], u4.shape[2]
        ublock = pl.BlockSpec((nj, tl, nb, LANES), lambda i: (0, i, 0, 0))
    else:
        nb, l = u4.shape[1], u4.shape[2]
        ublock = pl.BlockSpec((nj, nb, tl, LANES), lambda i: (0, 0, i, 0))
    rows = tl * nb
    full = lambda a: pl.BlockSpec(a.shape, lambda i, nd=a.ndim: (0,) * nd)
    return pl.pallas_call(
        functools.partial(_ssm_kernel, nb=nb, tl=tl, time_major=time_major, chunk=N_STATE // nj),
        grid=(l // tl,),
        in_specs=[ublock, full(h0_re), full(h0_im), full(ab_re), full(ab_im),
                  full(b_re), full(b_im), full(c_re), full(c_im), full(d_skip)],
        out_specs=[ublock, full(h0_re), full(h0_im)],
        out_shape=[jax.ShapeDtypeStruct(u4.shape, f32),
                   jax.ShapeDtypeStruct(h0_re.shape, f32), jax.ShapeDtypeStruct(h0_im.shape, f32)],
        scratch_shapes=[pltpu.VMEM((rows, D_SSM), f32),
                        pltpu.VMEM((rows, N_STATE), f32), pltpu.VMEM((rows, N_STATE), f32),
                        pltpu.VMEM((rows, N_STATE), bf16), pltpu.VMEM((rows, N_STATE), bf16),
                        pltpu.VMEM((nb, N_STATE), f32), pltpu.VMEM((nb, N_STATE), f32),
                        pltpu.VMEM((rows, D_SSM), f32),
                        pltpu.VMEM((nj, rows, LANES), f32)],
        compiler_params=_cparams(("arbitrary",)), name="ssm",
    )(u4, h0_re, h0_im, ab_re, ab_im, b_re, b_im, c_re, c_im, d_skip)


def _post_kernel(*refs, gate_att):
    if gate_att:
        x_ref, att_ref, zatt_ref, y_ref, zssm_ref, gate_ref, wglu_ref, bglu_ref, wout_ref, gf_ref, o_ref = refs
        att = att_ref[...] * _silu(zatt_ref[...])
    else:
        x_ref, att_ref, y_ref, zssm_ref, gate_ref, wglu_ref, bglu_ref, wout_ref, gf_ref, o_ref = refs
        att = att_ref[...]
    y = jnp.concatenate([y_ref[j] for j in range(D_SSM // LANES)], axis=-1)
    g = jax.nn.gelu(y)
    glu = jax.nn.sigmoid(jnp.dot(g.astype(bf16), wglu_ref[...], preferred_element_type=f32) + bglu_ref[...])
    ssm = g * glu * _silu(zssm_ref[...])
    mix = jnp.concatenate([att.astype(bf16), ssm.astype(bf16)], axis=-1)
    x = x_ref[...]
    xo = x + gate_ref[...] * jnp.dot(mix, wout_ref[...], preferred_element_type=f32)
    o_ref[...] = xo * lax.rsqrt(jnp.mean(xo * xo, axis=-1, keepdims=True) + EPS) * gf_ref[...]


def _post_call(x, att, z_att, y4, z_ssm, gate, w_glu, b_glu, w_out, g_final, tm):
    n, l, d = x.shape
    row = lambda n_, i: (n_, i, 0)
    const2 = lambda n_, i: (0, 0)
    gate_spec = (pl.BlockSpec((None, 1, d), lambda n_, i: (n_, 0, 0)) if gate.shape[1] == 1
                 else pl.BlockSpec((None, tm, d), row))
    ins = [x, att] + ([z_att] if z_att is not None else []) + [y4, z_ssm, gate, w_glu, b_glu, w_out, g_final]
    in_specs = ([pl.BlockSpec((None, tm, d), row), pl.BlockSpec((None, tm, D_ATT), row)]
                + ([pl.BlockSpec((None, tm, D_ATT), row)] if z_att is not None else [])
                + [pl.BlockSpec((D_SSM // LANES, None, tm, LANES), lambda n_, i: (0, n_, i, 0)),
                   pl.BlockSpec((None, tm, D_SSM), row), gate_spec,
                   pl.BlockSpec(w_glu.shape, const2), pl.BlockSpec((1, D_SSM), const2),
                   pl.BlockSpec(w_out.shape, const2), pl.BlockSpec((1, d), const2)])
    return pl.pallas_call(
        functools.partial(_post_kernel, gate_att=z_att is not None),
        grid=(n, l // tm), in_specs=in_specs,
        out_specs=pl.BlockSpec((None, tm, d), row),
        out_shape=jax.ShapeDtypeStruct((n, l, d), f32),
        compiler_params=_cparams(("arbitrary", "arbitrary")), name="post",
    )(*ins)


def kernel(x_prompt, x_sample, c_prompt, c_sample, cache_k, cache_v, cache_logf, state_ssm_re, state_ssm_im,
           page_table, g_norm, w_ada, b_ada, w_in, b_fgate, a_re, a_im, log_dt, b_re, b_im, c_re, c_im, d_skip,
           w_glu, b_glu, w_out, g_final):
    n, l, d = x_prompt.shape
    ns = x_sample.shape[0]
    depth = w_in.shape[0]
    assert depth == 1 and x_sample.shape[1] == 1
    lyr = 0
    xs = x_sample.reshape(ns, d)

    splits = [0, D_ATT, 2 * D_ATT, 3 * D_ATT, 4 * D_ATT, 4 * D_ATT + ATT_HEADS,
              4 * D_ATT + ATT_HEADS + D_SSM, 4 * D_ATT + ATT_HEADS + 2 * D_SSM]
    w_t = jnp.swapaxes(w_in[lyr], 0, 1)
    wq, wk, wv, wz, wf, wu, wzs = [w_t[splits[i]:splits[i + 1]] for i in range(7)]
    w_nn = jnp.concatenate([wq, wz, wu, wzs], axis=0).T.astype(bf16)
    w_nt = jnp.concatenate([wk, wv, wf, jnp.zeros((NT_ROWS - 2 * D_ATT - ATT_HEADS, d), f32)], axis=0).astype(bf16)
    w_all = jnp.concatenate([wq, wk, wv, wz, wu, wzs, wf, jnp.zeros((LANES - ATT_HEADS, d), f32)], axis=0).T.astype(bf16)
    b_f = b_fgate[lyr]
    b_f_pad = jnp.concatenate([b_f, jnp.zeros((LANES - ATT_HEADS,), f32)])[None, :]
    seg = (jnp.arange(D_ATT)[:, None] // HEAD_DIM == jnp.arange(LANES)[None, :]).astype(f32)
    gn = g_norm[lyr][None, :]
    gf = g_final[None, :]
    w_glu_b = w_glu[lyr].astype(bf16)
    w_out_b = w_out[lyr].astype(bf16)
    b_glu2 = b_glu[lyr][None, :]

    mod = _ada_call(jnp.concatenate([c_prompt, c_sample], axis=0), w_ada[lyr], b_ada[lyr][None, :])
    shift_p, scale_p, gate_p = [mod[:n, k * d:(k + 1) * d].reshape(n, 1, d) for k in range(3)]
    shift_s, scale_s, gate_s = [mod[n:, k * d:(k + 1) * d] for k in range(3)]

    ab_re, ab_im, bb_re, bb_im = _ssm_param_call(a_re[lyr], a_im[lyr], log_dt[lyr], b_re[lyr], b_im[lyr])
    ab_re_row = ab_re.reshape(1, N_STATE)
    ab_im_row = ab_im.reshape(1, N_STATE)
    slabs = D_SSM // LANES
    slab_diag = lambda w: jnp.stack([_block_diag(b) for b in w.reshape((slabs, SSM_GROUPS // slabs) + w.shape[1:])])
    bd_re = slab_diag(bb_re).astype(bf16)
    bd_im = slab_diag(bb_im).astype(bf16)
    cd_re = slab_diag(jnp.swapaxes(c_re[lyr], 1, 2)).astype(bf16)
    cd_im = slab_diag(jnp.swapaxes(c_im[lyr], 1, 2)).astype(bf16)
    d_row = d_skip[lyr].reshape(1, D_SSM)

    tm = min(512, l)
    q, z_att, u4, z_ssm, kt, vt, lft, ktb, vtb = _pre_prompt_call(x_prompt, shift_p, scale_p, gn, w_nn, w_nt,
                                                                   b_f, tm)
    qs, k_s, v_s, zatt_s, u_s, zssm_s, lf_s, snew = _pre_sample_call(xs, shift_s, scale_s, gn, w_all, b_f_pad, seg)

    qb = jnp.broadcast_to(qs.reshape(ns, ATT_HEADS, HEAD_DIM, 1), (ns, ATT_HEADS, HEAD_DIM, LANES))
    snew_b = jnp.broadcast_to(snew[:, :ATT_HEADS, None], (ns, ATT_HEADS, LANES))
    lfnew_b = jnp.broadcast_to(lf_s[:, :ATT_HEADS, None], (ns, ATT_HEADS, LANES))
    kc = jnp.transpose(cache_k[lyr], (0, 2, 3, 1))
    vc = jnp.transpose(cache_v[lyr], (0, 2, 3, 1))
    lfc = jnp.transpose(cache_logf[lyr], (0, 2, 1))
    n_pages = page_table.shape[1]
    att, att_s = _att_call(q, ktb, vtb, lft, z_att, page_table, qb, snew_b, lfnew_b,
                           v_s.reshape(ns, ATT_HEADS, HEAD_DIM), kc, vc, lfc,
                           tq=min(256, l), ppu=min(8, n_pages // 2))

    zeros_state = jnp.zeros((n, N_STATE), f32)
    y4, hp_re, hp_im = _ssm_call(u4, zeros_state, zeros_state, ab_re_row, ab_im_row, bd_re, bd_im, cd_re, cd_im,
                                 d_row, tl=min(32, l), time_major=False)
    y_prompt = _post_call(x_prompt, att, None, y4, z_ssm, gate_p, w_glu_b, b_glu2, w_out_b, gf, tm)

    heads_t = lambda t: jnp.transpose(t.reshape(n, ATT_HEADS, HEAD_DIM, l), (0, 3, 1, 2))[None]
    k_prompt = heads_t(kt)
    v_prompt = heads_t(vt)
    logf_prompt = jnp.transpose(lft, (0, 2, 1))[None]

    u4s =jnp.transpose(u_s.reshape(1, ns, D_SSM // LANES, LANES), (2, 0, 1, 3))
    h0r = state_ssm_re[lyr].reshape(ns, N_STATE)
    h0i = state_ssm_im[lyr].reshape(ns, N_STATE)
    y4s, hs_re, hs_im = _ssm_call(u4s, h0r, h0i, ab_re_row, ab_im_row, bd_re, bd_im, cd_re, cd_im, d_row,
                                  tl=1, time_major=True)
    y_sample = _post_call(xs[None], att_s.reshape(1, ns, D_ATT), zatt_s[None], y4s, zssm_s[None], gate_s[None],
                          w_glu_b, b_glu2, w_out_b, gf, ns)

    st = lambda a, b: a.reshape(1, b, SSM_GROUPS, SSM_STATE)
    return (y_prompt, y_sample.reshape(ns, 1, d),
            k_prompt, v_prompt, logf_prompt, st(hp_re, n), st(hp_im, n),
            k_s.reshape(1, ns, 1, ATT_HEADS, HEAD_DIM), v_s.reshape(1, ns, 1, ATT_HEADS, HEAD_DIM),
            lf_s[:, :ATT_HEADS].reshape(1, ns, 1, ATT_HEADS), st(hs_re, ns), st(hs_im, ns))
```

```python
import functools
import math

import jax
import jax.numpy as jnp
from jax import lax
from jax.experimental import pallas as pl
from jax.experimental.pallas import tpu as pltpu

f32 = jnp.float32
bf16 = jnp.bfloat16

HEAD_DIM = 64
ATT_HEADS = 8
D_ATT = ATT_HEADS * HEAD_DIM
SSM_GROUP = 16
SSM_GROUPS = 32
SSM_STATE = 64
D_SSM = SSM_GROUP * SSM_GROUPS
N_STATE = SSM_GROUPS * SSM_STATE
EPS = 1e-6
QK_SCALE = HEAD_DIM ** -0.5
LOG2E = 1.0 / math.log(2.0)
LANES = 128
NT_ROWS = 2 * D_ATT + 16
VMEM_LIMIT = 56 * 1024 * 1024

_NT = (((1,), (1,)), ((), ()))
_HI = lax.Precision.HIGHEST


def _silu(x):
    return x * jax.nn.sigmoid(x)


def _log_sigmoid(x):
    return jnp.minimum(x, 0.0) - jnp.log1p(jnp.exp(-jnp.abs(x)))


def _modulated_norm(x, g, scale, shift):
    y = x * lax.rsqrt(jnp.mean(x * x, axis=-1, keepdims=True) + EPS)
    return (y * g) * (1.0 + scale) + shift


def _cparams(sem, vmem=VMEM_LIMIT):
    return pltpu.CompilerParams(dimension_semantics=sem, vmem_limit_bytes=vmem)


def _ada_kernel(c_ref, w_ref, b_ref, o_ref):
    s = _silu(c_ref[...]).astype(bf16)
    o_ref[...] = jnp.dot(s, w_ref[...].astype(bf16), preferred_element_type=f32) + b_ref[...]


def _ada_call(c_all, w_ada, b_ada):
    r, d = c_all.shape
    n_out = w_ada.shape[1]
    tn = 768
    return pl.pallas_call(
        _ada_kernel,
        grid=(n_out // tn,),
        in_specs=[pl.BlockSpec((r, d), lambda j: (0, 0)),
                  pl.BlockSpec((d, tn), lambda j: (0, j)),
                  pl.BlockSpec((1, tn), lambda j: (0, j))],
        out_specs=pl.BlockSpec((r, tn), lambda j: (0, j)),
        out_shape=jax.ShapeDtypeStruct((r, n_out), f32),
        compiler_params=_cparams(("arbitrary",)),
        name="ada",
    )(c_all, w_ada, b_ada)


def _ssm_param_kernel(are_ref, aim_ref, ldt_ref, bre_ref, bim_ref,
                      abr_ref, abi_ref, bbr_ref, bbi_ref):
    lr, li = are_ref[...], aim_ref[...]
    dt = jnp.exp(ldt_ref[...])
    mag = jnp.exp(lr * dt)
    abr = mag * jnp.cos(li * dt)
    abi = mag * jnp.sin(li * dt)
    xr, xi = abr - 1.0, abi
    den = lr * lr + li * li
    cr = (xr * lr + xi * li) / den
    ci = (xi * lr - xr * li) / den
    br, bi = bre_ref[...], bim_ref[...]
    abr_ref[...] = abr
    abi_ref[...] = abi
    bbr_ref[...] = cr * br - ci * bi
    bbi_ref[...] = cr * bi + ci * br


def _ssm_param_call(a_re, a_im, log_dt, b_re, b_im):
    g, p = a_re.shape
    hc = b_re.shape[-1]
    rep = lambda a: jnp.broadcast_to(a[:, None, :], (g, hc, p)).reshape(g * hc, p)
    ldt = jnp.broadcast_to(log_dt[:, None, None], (g, hc, p)).reshape(g * hc, p)
    bt = lambda b: jnp.swapaxes(b, 1, 2).reshape(g * hc, p)
    shp = jax.ShapeDtypeStruct((g * hc, p), f32)
    abr, abi, bbr, bbi = pl.pallas_call(
        _ssm_param_kernel, out_shape=[shp] * 4, name="ssm_par",
    )(rep(a_re), rep(a_im), ldt, bt(b_re), bt(b_im))
    first = lambda a: a.reshape(g, hc, p)[:, 0, :]
    return first(abr), first(abi), bbr.reshape(g, hc, p), bbi.reshape(g, hc, p)


def _block_diag(blocks):
    g, r, c = blocks.shape
    eye = jnp.eye(g, dtype=blocks.dtype)
    return (blocks[:, :, None, :] * eye[:, None, :, None]).reshape(g * r, g * c)


def _pre_prompt_kernel(x_ref, shift_ref, scale_ref, g_ref, wnn_ref, wnt_ref, bf_ref,
                       q_ref, zatt_ref, u_ref, zssm_ref, kt_ref, vt_ref, lft_ref, ktb_ref, vtb_ref):
    h = _modulated_norm(x_ref[...], g_ref[...], scale_ref[...], shift_ref[...])
    hb = h.astype(bf16)
    p = jnp.dot(hb, wnn_ref[...], preferred_element_type=f32)
    q_ref[...] = (p[:, :D_ATT] * (QK_SCALE * LOG2E)).astype(bf16)
    zatt_ref[...] = p[:, D_ATT:2 * D_ATT]
    u_ref[...] = p[:, 2 * D_ATT:2 * D_ATT + D_SSM]
    zssm_ref[...] = p[:, 2 * D_ATT + D_SSM:]
    pt = lax.dot_general(wnt_ref[...], hb, _NT, preferred_element_type=f32)
    kt_ref[...] = pt[:D_ATT]
    vt_ref[...] = pt[D_ATT:2 * D_ATT]
    ktb_ref[...] = pt[:D_ATT].astype(bf16)
    vtb_ref[...] = pt[D_ATT:2 * D_ATT].astype(bf16)
    lft_ref[...] = _log_sigmoid(pt[2 * D_ATT:2 * D_ATT + ATT_HEADS] + bf_ref[...])


def _pre_prompt_call(x, shift, scale, g_norm, w_nn, w_nt, b_f, tm):
    n, l, d = x.shape
    row = lambda n_, i: (n_, i, 0)
    col = lambda n_, i: (n_, 0, i)
    const2 = lambda n_, i: (0, 0)
    out_shape = [
        jax.ShapeDtypeStruct((n, l, D_ATT), bf16),
        jax.ShapeDtypeStruct((n, l, D_ATT), f32),
        jax.ShapeDtypeStruct((n, l, D_SSM), f32),
        jax.ShapeDtypeStruct((n, l, D_SSM), f32),
        jax.ShapeDtypeStruct((n, D_ATT, l), f32),
        jax.ShapeDtypeStruct((n, D_ATT, l), f32),
        jax.ShapeDtypeStruct((n, ATT_HEADS, l), f32),
        jax.ShapeDtypeStruct((n, D_ATT, l), bf16),
        jax.ShapeDtypeStruct((n, D_ATT, l), bf16),
    ]
    out_specs = [
        pl.BlockSpec((None, tm, D_ATT), row),
        pl.BlockSpec((None, tm, D_ATT), row),
        pl.BlockSpec((None, tm, D_SSM), row),
        pl.BlockSpec((None, tm, D_SSM), row),
        pl.BlockSpec((None, D_ATT, tm), col),
        pl.BlockSpec((None, D_ATT, tm), col),
        pl.BlockSpec((None, ATT_HEADS, tm), col),
        pl.BlockSpec((None, D_ATT, tm), col),
        pl.BlockSpec((None, D_ATT, tm), col),
    ]
    in_specs = [
        pl.BlockSpec((None, tm, d), row),
        pl.BlockSpec((None, 1, d), lambda n_, i: (n_, 0, 0)),
        pl.BlockSpec((None, 1, d), lambda n_, i: (n_, 0, 0)),
        pl.BlockSpec((1, d), const2),
        pl.BlockSpec(w_nn.shape, const2),
        pl.BlockSpec(w_nt.shape, const2),
        pl.BlockSpec((ATT_HEADS, tm), const2),
    ]
    return pl.pallas_call(
        _pre_prompt_kernel, grid=(n, l // tm), in_specs=in_specs, out_specs=out_specs,
        out_shape=out_shape, compiler_params=_cparams(("arbitrary", "arbitrary")), name="pre",
    )(x, shift, scale, g_norm, w_nn, w_nt, jnp.broadcast_to(b_f[:, None], (ATT_HEADS, tm)))


def _pre_sample_kernel(x_ref, shift_ref, scale_ref, g_ref, w_ref, bf_ref, seg_ref,
                       q_ref, k_ref, v_ref, zatt_ref, u_ref, zssm_ref, lf_ref, snew_ref):
    h = _modulated_norm(x_ref[...], g_ref[...], scale_ref[...], shift_ref[...])
    p = jnp.dot(h.astype(bf16), w_ref[...], preferred_element_type=f32)
    q = p[:, :D_ATT] * QK_SCALE
    k = p[:, D_ATT:2 * D_ATT]
    q_ref[...] = q
    k_ref[...] = k
    v_ref[...] = p[:, 2 * D_ATT:3 * D_ATT]
    zatt_ref[...] = p[:, 3 * D_ATT:4 * D_ATT]
    u_ref[...] = p[:, 4 * D_ATT:4 * D_ATT + D_SSM]
    zssm_ref[...] = p[:, 4 * D_ATT + D_SSM:4 * D_ATT + 2 * D_SSM]
    lf_ref[...] = _log_sigmoid(p[:, 4 * D_ATT + 2 * D_SSM:] + bf_ref[...])
    snew_ref[...] = jnp.dot(q * k, seg_ref[...], preferred_element_type=f32, precision=_HI)


def _pre_sample_call(x, shift, scale, g_norm, w_all, b_f_pad, seg):
    r, d = x.shape
    shp = lambda c: jax.ShapeDtypeStruct((r, c), f32)
    return pl.pallas_call(
        _pre_sample_kernel,
        out_shape=[shp(D_ATT), shp(D_ATT), shp(D_ATT), shp(D_ATT), shp(D_SSM), shp(D_SSM), shp(LANES), shp(LANES)],
        compiler_params=pltpu.CompilerParams(vmem_limit_bytes=VMEM_LIMIT), name="pre_s",
    )(x, shift, scale, g_norm, w_all, b_f_pad, seg)


def _split3(x):
    hi = x.astype(bf16).astype(f32)
    r = x - hi
    mid = r.astype(bf16).astype(f32)
    return hi, mid, r - mid


def _att_kernel(pt_ref, q_ref, kt_ref, vt_ref, lft_ref, z_ref,
                snew_ref, lfnew_ref, vnew_ref, tri_ref, qb_hbm, kc_hbm, vc_hbm, lfc_hbm,
                o_ref, od_ref,
                ka, va, cum, qa, m_s, acc_s,
                kbuf, vbuf, lfbuf, qbuf, sc, bias, tot, p_s, dacc, stat, ksem, vsem, lfsem, qsem,
                *, tq, seq, n_tok, n_pages, ppu, ring):
    n = pl.program_id(0)
    qi = pl.program_id(1)
    aug = HEAD_DIM
    n_tiles = seq // tq
    units = n_pages // ppu
    ahead = ring - 1
    k_end = n_tok * units
    v_lo, v_hi = units, (n_tok + 1) * units

    def start_pages(src, buf, sem, tok, sub, slot):
        for j in range(ppu):
            pltpu.make_async_copy(src.at[pt_ref[tok, sub * ppu + j]], buf.at[slot, j], sem.at[slot]).start()

    def wait_pages(src, buf, sem, slot):
        for j in range(ppu):
            pltpu.make_async_copy(src.at[0], buf.at[slot, j], sem.at[slot]).wait()

    def request(w):
        tok, sub, slot = w // units, w % units, w % ring

        @pl.when(w < k_end)
        def _():
            start_pages(kc_hbm, kbuf, ksem, tok, sub, slot)

        @pl.when((w >= v_lo) & (w < v_hi))
        def _():
            start_pages(vc_hbm, vbuf, vsem, tok - 1, sub, slot)

    def start_token_inputs(t):
        slot = t % 2

        def body(pg, carry):
            pltpu.make_async_copy(lfc_hbm.at[pt_ref[t, pg]], lfbuf.at[slot, pg], lfsem.at[slot]).start()
            return carry
        lax.fori_loop(0, n_pages, body, 0)
        pltpu.make_async_copy(qb_hbm.at[t], qbuf.at[slot], qsem.at[slot]).start()

    def new_token(t):
        slot = t % 2

        @pl.when(t + 1 < n_tok)
        def _():
            start_token_inputs(t + 1)

        def wait_lf(pg, carry):
            pltpu.make_async_copy(lfc_hbm.at[0], lfbuf.at[slot, pg], lfsem.at[slot]).wait()
            return carry
        lax.fori_loop(0, n_pages, wait_lf, 0)
        pltpu.make_async_copy(qb_hbm.at[0], qbuf.at[slot], qsem.at[slot]).wait()
        lf2 = lfbuf[slot].reshape(n_pages * ATT_HEADS, LANES)
        sums = sum(jnp.dot(part.astype(bf16), tri_ref[...], preferred_element_type=f32) for part in _split3(lf2))
        bias[slot] = sums[:, :LANES].reshape(n_pages, ATT_HEADS, LANES)
        tot[...] = sums[:, LANES:].reshape(n_pages, ATT_HEADS, LANES)
        lfnew = lfnew_ref[t]

        def suffix(i, run):
            pg = n_pages - 1 - i
            bias[slot, pg] = bias[slot, pg] + run + lfnew
            return run + tot[pg]
        lax.fori_loop(0, n_pages, suffix, jnp.zeros((ATT_HEADS, LANES), f32))

    def token_softmax(t):
        s_all = sc[t % 2] + bias[t % 2]
        snew = snew_ref[t]
        m = jnp.max(jnp.max(s_all, axis=0), axis=-1, keepdims=True)
        m = jnp.maximum(m, snew)
        p_all = jnp.exp(s_all - m[None])
        p_s[...] = p_all
        p_new = jnp.exp(snew - m)
        stat[0] = p_new
        stat[1] = jnp.sum(jnp.sum(p_all, axis=0), axis=-1, keepdims=True) + p_new
        dacc[...] = jnp.zeros_like(dacc)

    def token_output(t):
        ones_row = jnp.ones((ATT_HEADS, LANES), f32)
        head_row = lax.broadcasted_iota(jnp.int32, (ATT_HEADS, HEAD_DIM), 0)
        past = jnp.zeros((ATT_HEADS, HEAD_DIM), f32)
        for h in range(ATT_HEADS):
            r = lax.dot_general(ones_row, dacc[h], _NT, preferred_element_type=f32, precision=_HI)
            past = jnp.where(head_row == h, r, past)
        od_ref[t] = (past + stat[0][:, :HEAD_DIM] * vnew_ref[t]) * (1.0 / stat[1][:, :HEAD_DIM])

    def unit_prologue(w):
        t = w // units

        @pl.when(w % units == 0)
        def _():
            @pl.when((t >= 2) & (t <= n_tok + 1))
            def _():
                token_output(t - 2)

            @pl.when((t >= 1) & (t <= n_tok))
            def _():
                token_softmax(t - 1)

            @pl.when(t < n_tok)
            def _():
                new_token(t)

        request(w + ahead)

        @pl.when(w < k_end)
        def _():
            wait_pages(kc_hbm, kbuf, ksem, w % ring)

        @pl.when((w >= v_lo) & (w < v_hi))
        def _():
            wait_pages(vc_hbm, vbuf, vsem, w % ring)

    def unit_compute(w, h):
        tok, sub, slot = w // units, w % units, w % ring
        qh = qbuf[jnp.minimum(tok, n_tok - 1) % 2, h]
        a = dacc[h]
        for j in range(ppu):
            sc[tok % 2, sub * ppu + j, pl.ds(h, 1), :] = jnp.sum(kbuf[slot, j, h] * qh, axis=0, keepdims=True)
            pr = p_s[sub * ppu + j, pl.ds(h, 1), :]
            a = a + vbuf[slot, j, h] * jnp.broadcast_to(pr, (HEAD_DIM, LANES))
        dacc[h] = a

    @pl.when((n == 0) & (qi == 0))
    def _():
        vbuf[...] = jnp.zeros_like(vbuf)
        p_s[...] = jnp.zeros_like(p_s)
        dacc[...] = jnp.zeros_like(dacc)
        start_token_inputs(jnp.int32(0))
        for w0 in range(ahead):
            request(jnp.int32(w0))

    @pl.when(qi == 0)
    def _():
        c = lft_ref[...]
        lane = lax.broadcasted_iota(jnp.int32, c.shape, 1)
        d = 1
        while d < seq:
            c = c + jnp.where(lane >= d, pltpu.roll(c, d, axis=1), 0.0)
            d *= 2
        c = c * LOG2E
        cum[...] = jnp.zeros_like(cum)
        r16 = lax.broadcasted_iota(jnp.int32, (16, seq), 0)
        for h in range(ATT_HEADS):
            ka[h, 0:HEAD_DIM, :] = kt_ref[h * HEAD_DIM:(h + 1) * HEAD_DIM, :]
            va[h, 0:HEAD_DIM, :] = vt_ref[h * HEAD_DIM:(h + 1) * HEAD_DIM, :]
            hi, mid, lo = _split3(c[h:h + 1, :])
            for j, part in enumerate((hi, mid, lo)):
                cum[3 * h + j:3 * h + j + 1, :] = part
            ext = jnp.where(r16 == 0, -hi, jnp.where(r16 == 1, -mid, jnp.where(r16 == 2, -lo,
                            jnp.where(r16 < 6, 1.0, 0.0))))
            ka[h, aug:aug + 16, :] = ext.astype(bf16)
            ka[h, aug + 16:, :] = jnp.zeros((LANES - aug - 16, seq), bf16)
            va[h, HEAD_DIM:, :] = jnp.ones((LANES - HEAD_DIM, seq), bf16)

    q0 = pl.multiple_of(qi * tq, tq)
    cq_parts = cum[:, pl.ds(q0, tq)].T
    lane_q = lax.broadcasted_iota(jnp.int32, (tq, LANES), 1)
    ones_lanes = jnp.where(lane_q < aug + 3, 1.0, 0.0)
    for pr in range(ATT_HEADS // 2):
        qp = q_ref[:, pr * LANES:(pr + 1) * LANES].astype(f32)
        for hh in range(2):
            h = 2 * pr + hh
            qh = qp if hh == 0 else pltpu.roll(qp, HEAD_DIM, axis=1)
            cq = pltpu.roll(cq_parts, aug + 3 - 3 * h, axis=1)
            ext = jnp.where((lane_q >= aug + 3) & (lane_q < aug + 6), cq, ones_lanes)
            qa[h] = jnp.where(lane_q < HEAD_DIM, qh, ext).astype(bf16)

    row = lax.broadcasted_iota(jnp.int32, (tq, tq), 0)
    col = lax.broadcasted_iota(jnp.int32, (tq, tq), 1)
    causal = col <= row

    first_unit = n * (n_tiles * (n_tiles + 1) // 2) + (qi * (qi + 1)) // 2

    def block(kb, diagonal):
        w = first_unit if diagonal else first_unit + 1 + kb
        unit_prologue(w)
        k0 = pl.multiple_of(kb * tq, tq)
        for h in range(ATT_HEADS):
            unit_compute(w, h)
            s = jnp.dot(qa[h], ka[h, :, pl.ds(k0, tq)], preferred_element_type=f32)
            if diagonal:
                s = jnp.where(causal, s, -jnp.inf)
            rowmax = jnp.broadcast_to(jnp.max(s, axis=-1, keepdims=True), (tq, LANES))
            m_new = rowmax if diagonal else jnp.maximum(m_s[h], rowmax)
            p = [jnp.exp2(s[:, j * LANES:(j + 1) * LANES] - m_new) for j in range(tq // LANES)]
            p = jnp.concatenate(p, axis=1)
            pv = lax.dot_general(p.astype(bf16), va[h, :, pl.ds(k0, tq)], _NT, preferred_element_type=f32)
            acc_s[h] = pv if diagonal else jnp.exp2(m_s[h] - m_new) * acc_s[h] + pv
            m_s[h] = m_new

    def off_diagonal(kb, carry):
        block(kb, False)
        return carry

    block(qi, True)
    lax.fori_loop(0, qi, off_diagonal, 0)

    for pr in range(ATT_HEADS // 2):
        a0, a1 = acc_s[2 * pr], acc_s[2 * pr + 1]
        o0 = a0 * pltpu.roll(1.0 / a0, HEAD_DIM, axis=1)
        o1 = pltpu.roll(a1, HEAD_DIM, axis=1) * (1.0 / a1)
        o = jnp.where(lane_q < HEAD_DIM, o0, o1)
        cols = slice(pr * LANES, (pr + 1) * LANES)
        o_ref[:, cols] = (o * _silu(z_ref[:, cols])).astype(bf16)


def _att_call(q, ktb, vtb, lft, z_att, page_table, qb, snew_b, lfnew_b, vnew, kc, vc, lfc, tq, ppu, ring=4):
    n, l, _ = q.shape
    ns, n_pages = page_table.shape
    n_tiles = l // tq
    units = n_pages // ppu
    assert n * (n_tiles * (n_tiles + 1) // 2) >= (ns + 1) * units + 1, "too few key-block steps for the page units"
    lane = jnp.arange(LANES)
    tri = jnp.concatenate([lane[:, None] > lane[None, :], jnp.ones((LANES, LANES), bool)], axis=1).astype(bf16)
    row = lambda n_, i, pt: (n_, i, 0)
    whole = lambda n_, i, pt: (n_, 0, 0)
    const = lambda nd: (lambda n_, i, pt: (0,) * nd)
    page = (ppu, ATT_HEADS, HEAD_DIM, LANES)
    grid_spec = pltpu.PrefetchScalarGridSpec(
        num_scalar_prefetch=1,
        grid=(n, n_tiles),
        in_specs=[pl.BlockSpec((None, tq, D_ATT), row),
                  pl.BlockSpec((None, D_ATT, l), whole),
                  pl.BlockSpec((None, D_ATT, l), whole),
                  pl.BlockSpec((None, ATT_HEADS, l), whole),
                  pl.BlockSpec((None, tq, D_ATT), row),
                  pl.BlockSpec(snew_b.shape, const(3)),
                  pl.BlockSpec(lfnew_b.shape, const(3)),
                  pl.BlockSpec(vnew.shape, const(3)),
                  pl.BlockSpec((LANES, 2 * LANES), const(2)),
                  pl.BlockSpec(memory_space=pl.ANY),
                  pl.BlockSpec(memory_space=pl.ANY),
                  pl.BlockSpec(memory_space=pl.ANY),
                  pl.BlockSpec(memory_space=pl.ANY)],
        out_specs=[pl.BlockSpec((None, tq, D_ATT), row),
                   pl.BlockSpec((ns, ATT_HEADS, HEAD_DIM), const(3))],
        scratch_shapes=[pltpu.VMEM((ATT_HEADS, LANES, l), bf16), pltpu.VMEM((ATT_HEADS, LANES, l), bf16),
                        pltpu.VMEM((LANES, l), f32),
                        pltpu.VMEM((ATT_HEADS, tq, LANES), bf16),
                        pltpu.VMEM((ATT_HEADS, tq, LANES), f32),
                        pltpu.VMEM((ATT_HEADS, tq, LANES), f32),
                        pltpu.VMEM((ring,) + page, f32),
                        pltpu.VMEM((ring,) + page, f32),
                        pltpu.VMEM((2, n_pages, ATT_HEADS, LANES), f32),
                        pltpu.VMEM((2, ATT_HEADS, HEAD_DIM, LANES), f32),
                        pltpu.VMEM((2, n_pages, ATT_HEADS, LANES), f32),
                        pltpu.VMEM((2, n_pages, ATT_HEADS, LANES), f32),
                        pltpu.VMEM((n_pages, ATT_HEADS, LANES), f32),
                        pltpu.VMEM((n_pages, ATT_HEADS, LANES), f32),
                        pltpu.VMEM((ATT_HEADS, HEAD_DIM, LANES), f32),
                        pltpu.VMEM((2, ATT_HEADS, LANES), f32),
                        pltpu.SemaphoreType.DMA((ring,)), pltpu.SemaphoreType.DMA((ring,)),
                        pltpu.SemaphoreType.DMA((2,)), pltpu.SemaphoreType.DMA((2,))],
    )
    return pl.pallas_call(
        functools.partial(_att_kernel, tq=tq, seq=l, n_tok=ns, n_pages=n_pages, ppu=ppu, ring=ring),
        grid_spec=grid_spec,
        out_shape=[jax.ShapeDtypeStruct((n, l, D_ATT), bf16),
                   jax.ShapeDtypeStruct((ns, ATT_HEADS, HEAD_DIM), f32)],
        compiler_params=_cparams(("arbitrary", "arbitrary")), name="att",
    )(page_table, q, ktb, vtb, lft, z_att, snew_b, lfnew_b, vnew, tri, qb, kc, vc, lfc)


def _ssm_kernel(u_ref, h0r_ref, h0i_ref, ar_ref, ai_ref, bre_ref, bim_ref, cre_ref, cim_ref, d_ref,
                y_ref, hr_ref, hi_ref,
                utn, bu_re, bu_im, hs_re, hs_im, st_re, st_im, yscr, perm, *, nb, tl, time_major, chunk):
    i = pl.program_id(0)
    nj = D_SSM // LANES

    @pl.when(i == 0)
    def _():
        st_re[...] = h0r_ref[...]
        st_im[...] = h0i_ref[...]

    if time_major:
        utn[...] = u_ref[...].reshape(tl * nb, D_SSM)
    else:
        for j in range(nj):
            cols = slice(j * LANES, (j + 1) * LANES)
            for b in range(nb):
                perm[j, b * tl:(b + 1) * tl, :] = u_ref[b, :, cols]
            for t in range(tl):
                utn[t * nb:(t + 1) * nb, cols] = perm[j, pl.ds(t, nb, stride=tl), :]

    for c in range(nj):
        cols = slice(c * chunk, (c + 1) * chunk)
        ucols = slice(c * LANES, (c + 1) * LANES)
        ub = utn[:, ucols].astype(bf16)
        bu_re[:, cols] = jnp.dot(ub, bre_ref[c], preferred_element_type=f32)
        bu_im[:, cols] = jnp.dot(ub, bim_ref[c], preferred_element_type=f32)
        ar = jnp.broadcast_to(ar_ref[:, cols], (nb, chunk))
        ai = jnp.broadcast_to(ai_ref[:, cols], (nb, chunk))
        hr, hi = st_re[:, cols], st_im[:, cols]
        for t in range(tl):
            rows = slice(t * nb, (t + 1) * nb)
            hr, hi = (ar * hr - ai * hi + bu_re[rows, cols], ar * hi + ai * hr + bu_im[rows, cols])
            hs_re[rows, cols] = hr.astype(bf16)
            hs_im[rows, cols] = hi.astype(bf16)
        st_re[:, cols] = hr
        st_im[:, cols] = hi
        yscr[:, ucols] = (jnp.dot(hs_re[:, cols], cre_ref[c], preferred_element_type=f32)
                          - jnp.dot(hs_im[:, cols], cim_ref[c], preferred_element_type=f32)
                          + d_ref[:, ucols] * utn[:, ucols])
    if time_major:
        y_ref[...] = yscr[...].reshape(tl, nb, D_SSM)
    else:
        for j in range(nj):
            cols = slice(j * LANES, (j + 1) * LANES)
            for t in range(tl):
                perm[j, pl.ds(t, nb, stride=tl), :] = yscr[t * nb:(t + 1) * nb, cols]
            for b in range(nb):
                y_ref[b, :, cols] = perm[j, b * tl:(b + 1) * tl, :]

    @pl.when(i == pl.num_programs(0) - 1)
    def _():
        hr_ref[...] = st_re[...]
        hi_ref[...] = st_im[...]


def _ssm_call(u, h0_re, h0_im, ab_re, ab_im, b_re, b_im, c_re, c_im, d_skip, tl, time_major):
    nj = D_SSM // LANES
    if time_major:
        l, nb = u.shape[0], u.shape[1]
        ublock = pl.BlockSpec((tl, nb, D_SSM), lambda i: (i, 0, 0))
    else:
        nb, l = u.shape[0], u.shape[1]
        ublock = pl.BlockSpec((nb, tl, D_SSM), lambda i: (0, i, 0))
    rows = tl * nb
    full = lambda a: pl.BlockSpec(a.shape, lambda i, nd=a.ndim: (0,) * nd)
    return pl.pallas_call(
        functools.partial(_ssm_kernel, nb=nb, tl=tl, time_major=time_major, chunk=N_STATE // nj),
        grid=(l // tl,),
        in_specs=[ublock, full(h0_re), full(h0_im), full(ab_re), full(ab_im),
                  full(b_re), full(b_im), full(c_re), full(c_im), full(d_skip)],
        out_specs=[ublock, full(h0_re), full(h0_im)],
        out_shape=[jax.ShapeDtypeStruct(u.shape, f32),
                   jax.ShapeDtypeStruct(h0_re.shape, f32), jax.ShapeDtypeStruct(h0_im.shape, f32)],
        scratch_shapes=[pltpu.VMEM((rows, D_SSM), f32),
                        pltpu.VMEM((rows, N_STATE), f32), pltpu.VMEM((rows, N_STATE), f32),
                        pltpu.VMEM((rows, N_STATE), bf16), pltpu.VMEM((rows, N_STATE), bf16),
                        pltpu.VMEM((nb, N_STATE), f32), pltpu.VMEM((nb, N_STATE), f32),
                        pltpu.VMEM((rows, D_SSM), f32),
                        pltpu.VMEM((nj, rows, LANES), f32)],
        compiler_params=_cparams(("arbitrary",)), name="ssm",
    )(u, h0_re, h0_im, ab_re, ab_im, b_re, b_im, c_re, c_im, d_skip)


def _post_kernel(*refs, gate_att):
    if gate_att:
        x_ref, att_ref, zatt_ref, y_ref, zssm_ref, gate_ref, wglu_ref, bglu_ref, wout_ref, gf_ref, o_ref = refs
        att = att_ref[...] * _silu(zatt_ref[...])
    else:
        x_ref, att_ref, y_ref, zssm_ref, gate_ref, wglu_ref, bglu_ref, wout_ref, gf_ref, o_ref = refs
        att = att_ref[...]
    g = jax.nn.gelu(y_ref[...])
    glu = jax.nn.sigmoid(jnp.dot(g.astype(bf16), wglu_ref[...], preferred_element_type=f32) + bglu_ref[...])
    ssm = g * glu * _silu(zssm_ref[...])
    mix = jnp.concatenate([att.astype(bf16), ssm.astype(bf16)], axis=-1)
    x = x_ref[...]
    xo = x + gate_ref[...] * jnp.dot(mix, wout_ref[...], preferred_element_type=f32)
    o_ref[...] = xo * lax.rsqrt(jnp.mean(xo * xo, axis=-1, keepdims=True) + EPS) * gf_ref[...]


def _post_call(x, att, z_att, y, z_ssm, gate, w_glu, b_glu, w_out, g_final, tm):
    n, l, d = x.shape
    row = lambda n_, i: (n_, i, 0)
    const2 = lambda n_, i: (0, 0)
    gate_spec = (pl.BlockSpec((None, 1, d), lambda n_, i: (n_, 0, 0)) if gate.shape[1] == 1
                 else pl.BlockSpec((None, tm, d), row))
    ins = [x, att] + ([z_att] if z_att is not None else []) + [y, z_ssm, gate, w_glu, b_glu, w_out, g_final]
    in_specs = ([pl.BlockSpec((None, tm, d), row), pl.BlockSpec((None, tm, D_ATT), row)]
                + ([pl.BlockSpec((None, tm, D_ATT), row)] if z_att is not None else [])
                + [pl.BlockSpec((None, tm, D_SSM), row),
                   pl.BlockSpec((None, tm, D_SSM), row), gate_spec,
                   pl.BlockSpec(w_glu.shape, const2), pl.BlockSpec((1, D_SSM), const2),
                   pl.BlockSpec(w_out.shape, const2), pl.BlockSpec((1, d), const2)])
    return pl.pallas_call(
        functools.partial(_post_kernel, gate_att=z_att is not None),
        grid=(n, l // tm), in_specs=in_specs,
        out_specs=pl.BlockSpec((None, tm, d), row),
        out_shape=jax.ShapeDtypeStruct((n, l, d), f32),
        compiler_params=_cparams(("arbitrary", "arbitrary")), name="post",
    )(*ins)


def kernel(x_prompt, x_sample, c_prompt, c_sample, cache_k, cache_v, cache_logf, state_ssm_re, state_ssm_im,
           page_table, g_norm, w_ada, b_ada, w_in, b_fgate, a_re, a_im, log_dt, b_re, b_im, c_re, c_im, d_skip,
           w_glu, b_glu, w_out, g_final):
    n, l, d = x_prompt.shape
    ns = x_sample.shape[0]
    depth = w_in.shape[0]
    assert depth == 1 and x_sample.shape[1] == 1
    lyr = 0
    xs = x_sample.reshape(ns, d)

    splits = [0, D_ATT, 2 * D_ATT, 3 * D_ATT, 4 * D_ATT, 4 * D_ATT + ATT_HEADS,
              4 * D_ATT + ATT_HEADS + D_SSM, 4 * D_ATT + ATT_HEADS + 2 * D_SSM]
    w_t = jnp.swapaxes(w_in[lyr], 0, 1)
    wq, wk, wv, wz, wf, wu, wzs = [w_t[splits[i]:splits[i + 1]] for i in range(7)]
    w_nn = jnp.concatenate([wq, wz, wu, wzs], axis=0).T.astype(bf16)
    w_nt = jnp.concatenate([wk, wv, wf, jnp.zeros((NT_ROWS - 2 * D_ATT - ATT_HEADS, d), f32)], axis=0).astype(bf16)
    w_all = jnp.concatenate([wq, wk, wv, wz, wu, wzs, wf, jnp.zeros((LANES - ATT_HEADS, d), f32)], axis=0).T.astype(bf16)
    b_f = b_fgate[lyr]
    b_f_pad = jnp.concatenate([b_f, jnp.zeros((LANES - ATT_HEADS,), f32)])[None, :]
    seg = (jnp.arange(D_ATT)[:, None] // HEAD_DIM == jnp.arange(LANES)[None, :]).astype(f32)
    gn = g_norm[lyr][None, :]
    gf = g_final[None, :]
    w_glu_b = w_glu[lyr].astype(bf16)
    w_out_b = w_out[lyr].astype(bf16)
    b_glu2 = b_glu[lyr][None, :]

    mod = _ada_call(jnp.concatenate([c_prompt, c_sample], axis=0), w_ada[lyr], b_ada[lyr][None, :])
    shift_p, scale_p, gate_p = [mod[:n, k * d:(k + 1) * d].reshape(n, 1, d) for k in range(3)]
    shift_s, scale_s, gate_s = [mod[n:, k * d:(k + 1) * d] for k in range(3)]

    ab_re, ab_im, bb_re, bb_im = _ssm_param_call(a_re[lyr], a_im[lyr], log_dt[lyr], b_re[lyr], b_im[lyr])
    ab_re_row = ab_re.reshape(1, N_STATE)
    ab_im_row = ab_im.reshape(1, N_STATE)
    slabs = D_SSM // LANES
    slab_diag = lambda w: jnp.stack([_block_diag(b) for b in w.reshape((slabs, SSM_GROUPS // slabs) + w.shape[1:])])
    bd_re = slab_diag(bb_re).astype(bf16)
    bd_im = slab_diag(bb_im).astype(bf16)
    cd_re = slab_diag(jnp.swapaxes(c_re[lyr], 1, 2)).astype(bf16)
    cd_im = slab_diag(jnp.swapaxes(c_im[lyr], 1, 2)).astype(bf16)
    d_row = d_skip[lyr].reshape(1, D_SSM)

    tm = min(512, l)
    q, z_att, u_p, z_ssm, kt, vt, lft, ktb, vtb = _pre_prompt_call(x_prompt, shift_p, scale_p, gn, w_nn, w_nt,
                                                                   b_f, tm)
    qs, k_s, v_s, zatt_s, u_s, zssm_s, lf_s, snew = _pre_sample_call(xs, shift_s, scale_s, gn, w_all, b_f_pad, seg)

    qb = jnp.broadcast_to(qs.reshape(ns, ATT_HEADS, HEAD_DIM, 1), (ns, ATT_HEADS, HEAD_DIM, LANES))
    snew_b = jnp.broadcast_to(snew[:, :ATT_HEADS, None], (ns, ATT_HEADS, LANES))
    lfnew_b = jnp.broadcast_to(lf_s[:, :ATT_HEADS, None], (ns, ATT_HEADS, LANES))
    kc = jnp.transpose(cache_k[lyr], (0, 2, 3, 1))
    vc = jnp.transpose(cache_v[lyr], (0, 2, 3, 1))
    lfc = jnp.transpose(cache_logf[lyr], (0, 2, 1))
    n_pages = page_table.shape[1]
    att, att_s = _att_call(q, ktb, vtb, lft, z_att, page_table, qb, snew_b, lfnew_b,
                           v_s.reshape(ns, ATT_HEADS, HEAD_DIM), kc, vc, lfc,
                           tq=min(256, l), ppu=min(8, n_pages // 2))

    zeros_state = jnp.zeros((n, N_STATE), f32)
    y_p, hp_re, hp_im = _ssm_call(u_p, zeros_state, zeros_state, ab_re_row, ab_im_row, bd_re, bd_im, cd_re, cd_im,
                                  d_row, tl=min(64, l), time_major=False)
    y_prompt = _post_call(x_prompt, att, None, y_p, z_ssm, gate_p, w_glu_b, b_glu2, w_out_b, gf, min(1024, l))

    heads_t = lambda t: jnp.transpose(t.reshape(n, ATT_HEADS, HEAD_DIM, l), (0, 3, 1, 2))[None]
    k_prompt = heads_t(kt)
    v_prompt = heads_t(vt)
    logf_prompt = jnp.transpose(lft, (0, 2, 1))[None]

    h0r = state_ssm_re[lyr].reshape(ns, N_STATE)
    h0i = state_ssm_im[lyr].reshape(ns, N_STATE)
    y_s, hs_re, hs_im = _ssm_call(u_s[None], h0r, h0i, ab_re_row, ab_im_row, bd_re, bd_im, cd_re, cd_im, d_row,
                                  tl=1, time_major=True)
    y_sample = _post_call(xs[None], att_s.reshape(1, ns, D_ATT), zatt_s[None], y_s, zssm_s[None], gate_s[None],
                          w_glu_b, b_glu2, w_out_b, gf, ns)

    st = lambda a, b: a.reshape(1, b, SSM_GROUPS, SSM_STATE)
    return (y_prompt, y_sample.reshape(ns, 1, d),
            k_prompt, v_prompt, logf_prompt, st(hp_re, n), st(hp_im, n),
            k_s.reshape(1, ns, 1, ATT_HEADS, HEAD_DIM), v_s.reshape(1, ns, 1, ATT_HEADS, HEAD_DIM),
            lf_s[:, :ATT_HEADS].reshape(1, ns, 1, ATT_HEADS), st(hs_re, ns), st(hs_im, ns))
```

```python
import functools
import math

import jax
import jax.numpy as jnp
from jax import lax
from jax.experimental import pallas as pl
from jax.experimental.pallas import tpu as pltpu

f32 = jnp.float32
bf16 = jnp.bfloat16

HEAD_DIM = 64
ATT_HEADS = 8
D_ATT = ATT_HEADS * HEAD_DIM
SSM_GROUP = 16
SSM_GROUPS = 32
SSM_STATE = 64
D_SSM = SSM_GROUP * SSM_GROUPS
N_STATE = SSM_GROUPS * SSM_STATE
EPS = 1e-6
QK_SCALE = HEAD_DIM ** -0.5
LOG2E = 1.0 / math.log(2.0)
LANES = 128
NT_ROWS = 2 * D_ATT + 16
VMEM_LIMIT = 56 * 1024 * 1024

_NT = (((1,), (1,)), ((), ()))
_HI = lax.Precision.HIGHEST


def _silu(x):
    return x * jax.nn.sigmoid(x)


def _log_sigmoid(x):
    return jnp.minimum(x, 0.0) - jnp.log1p(jnp.exp(-jnp.abs(x)))


def _modulated_norm(x, g, scale, shift):
    y = x * lax.rsqrt(jnp.mean(x * x, axis=-1, keepdims=True) + EPS)
    return (y * g) * (1.0 + scale) + shift


def _cparams(sem, vmem=VMEM_LIMIT):
    return pltpu.CompilerParams(dimension_semantics=sem, vmem_limit_bytes=vmem)


def _ada_kernel(c_ref, w_ref, b_ref, o_ref):
    s = _silu(c_ref[...]).astype(bf16)
    o_ref[...] = jnp.dot(s, w_ref[...].astype(bf16), preferred_element_type=f32) + b_ref[...]


def _ada_call(c_all, w_ada, b_ada):
    r, d = c_all.shape
    n_out = w_ada.shape[1]
    tn = 768
    return pl.pallas_call(
        _ada_kernel,
        grid=(n_out // tn,),
        in_specs=[pl.BlockSpec((r, d), lambda j: (0, 0)),
                  pl.BlockSpec((d, tn), lambda j: (0, j)),
                  pl.BlockSpec((1, tn), lambda j: (0, j))],
        out_specs=pl.BlockSpec((r, tn), lambda j: (0, j)),
        out_shape=jax.ShapeDtypeStruct((r, n_out), f32),
        compiler_params=_cparams(("arbitrary",)),
        name="ada",
    )(c_all, w_ada, b_ada)


def _ssm_param_kernel(are_ref, aim_ref, ldt_ref, bre_ref, bim_ref,
                      abr_ref, abi_ref, bbr_ref, bbi_ref):
    lr, li = are_ref[...], aim_ref[...]
    dt = jnp.exp(ldt_ref[...])
    mag = jnp.exp(lr * dt)
    abr = mag * jnp.cos(li * dt)
    abi = mag * jnp.sin(li * dt)
    xr, xi = abr - 1.0, abi
    den = lr * lr + li * li
    cr = (xr * lr + xi * li) / den
    ci = (xi * lr - xr * li) / den
    br, bi = bre_ref[...], bim_ref[...]
    abr_ref[...] = abr
    abi_ref[...] = abi
    bbr_ref[...] = cr * br - ci * bi
    bbi_ref[...] = cr * bi + ci * br


def _ssm_param_call(a_re, a_im, log_dt, b_re, b_im):
    g, p = a_re.shape
    hc = b_re.shape[-1]
    rep = lambda a: jnp.broadcast_to(a[:, None, :], (g, hc, p)).reshape(g * hc, p)
    ldt = jnp.broadcast_to(log_dt[:, None, None], (g, hc, p)).reshape(g * hc, p)
    bt = lambda b: jnp.swapaxes(b, 1, 2).reshape(g * hc, p)
    shp = jax.ShapeDtypeStruct((g * hc, p), f32)
    abr, abi, bbr, bbi = pl.pallas_call(
        _ssm_param_kernel, out_shape=[shp] * 4, name="ssm_par",
    )(rep(a_re), rep(a_im), ldt, bt(b_re), bt(b_im))
    first = lambda a: a.reshape(g, hc, p)[:, 0, :]
    return first(abr), first(abi), bbr.reshape(g, hc, p), bbi.reshape(g, hc, p)


def _block_diag(blocks):
    *lead, g, r, c = blocks.shape
    eye = jnp.eye(g, dtype=blocks.dtype)
    return (blocks[..., :, :, None, :] * eye[:, None, :, None]).reshape(*lead, g * r, g * c)


def _split3(x):
    hi = x.astype(bf16).astype(f32)
    r = x - hi
    mid = r.astype(bf16).astype(f32)
    return hi, mid, r - mid


AUG = HEAD_DIM


def _pre_prompt_kernel(x_ref, shift_ref, scale_ref, g_ref, wnn_ref, wnt_ref, bf_ref,
                       qa_ref, zatt_ref, u_ref, zssm_ref, kt_ref, vt_ref, lft_ref, ktb_ref, vtb_ref, cum_ref,
                       carry, parts):
    i = pl.program_id(1)
    tm = x_ref.shape[0]

    @pl.when(i == 0)
    def _():
        carry[...] = jnp.zeros_like(carry)

    @pl.when((pl.program_id(0) == 0) & (i == 0))
    def _():
        parts[...] = jnp.zeros_like(parts)

    h = _modulated_norm(x_ref[...], g_ref[...], scale_ref[...], shift_ref[...])
    hb = h.astype(bf16)
    pt = lax.dot_general(wnt_ref[...], hb, _NT, preferred_element_type=f32)
    kt_ref[...] = pt[:D_ATT]
    vt_ref[...] = pt[D_ATT:2 * D_ATT]
    ktb_ref[...] = pt[:D_ATT].astype(bf16)
    vtb_ref[...] = pt[D_ATT:2 * D_ATT].astype(bf16)
    lf = _log_sigmoid(pt[2 * D_ATT:2 * D_ATT + ATT_HEADS] + bf_ref[...])
    lft_ref[...] = lf
    p = jnp.dot(hb, wnn_ref[...], preferred_element_type=f32)
    zatt_ref[...] = p[:, D_ATT:2 * D_ATT]
    u_ref[...] = p[:, 2 * D_ATT:2 * D_ATT + D_SSM]
    zssm_ref[...] = p[:, 2 * D_ATT + D_SSM:]

    c = lf
    lane = lax.broadcasted_iota(jnp.int32, c.shape, 1)
    d = 1
    while d < tm:
        c = c + jnp.where(lane >= d, pltpu.roll(c, d, axis=1), 0.0)
        d *= 2
    c = c + jnp.concatenate([carry[...]] * (tm // LANES), axis=1)
    carry[...] = jnp.broadcast_to(c[:, tm - 1:tm], carry.shape)
    c = c * LOG2E
    cum_ref[...] = c
    for j, part in enumerate(_split3(c)):
        for hd in range(ATT_HEADS):
            parts[3 * hd + j:3 * hd + j + 1, :] = part[hd:hd + 1, :]
    cq_parts = parts[...].T
    lane_q = lax.broadcasted_iota(jnp.int32, (tm, LANES), 1)
    ones_lanes = jnp.where(lane_q < AUG + 3, 1.0, 0.0)
    for pr in range(ATT_HEADS // 2):
        qp = p[:, pr * LANES:(pr + 1) * LANES] * (QK_SCALE * LOG2E)
        for hh in range(2):
            hd = 2 * pr + hh
            qh = qp if hh == 0 else pltpu.roll(qp, HEAD_DIM, axis=1)
            cq = pltpu.roll(cq_parts, AUG + 3 - 3 * hd, axis=1)
            ext = jnp.where((lane_q >= AUG + 3) & (lane_q < AUG + 6), cq, ones_lanes)
            qa_ref[hd] = jnp.where(lane_q < HEAD_DIM, qh, ext).astype(bf16)


def _pre_prompt_call(x, shift, scale, g_norm, w_nn, w_nt, b_f, tm):
    n, l, d = x.shape
    row = lambda n_, i: (n_, i, 0)
    col = lambda n_, i: (n_, 0, i)
    const2 = lambda n_, i: (0, 0)
    out_shape = [
        jax.ShapeDtypeStruct((n, ATT_HEADS, l, LANES), bf16),
        jax.ShapeDtypeStruct((n, l, D_ATT), f32),
        jax.ShapeDtypeStruct((n, l, D_SSM), f32),
        jax.ShapeDtypeStruct((n, l, D_SSM), f32),
        jax.ShapeDtypeStruct((n, D_ATT, l), f32),
        jax.ShapeDtypeStruct((n, D_ATT, l), f32),
        jax.ShapeDtypeStruct((n, ATT_HEADS, l), f32),
        jax.ShapeDtypeStruct((n, D_ATT, l), bf16),
        jax.ShapeDtypeStruct((n, D_ATT, l), bf16),
        jax.ShapeDtypeStruct((n, ATT_HEADS, l), f32),
    ]
    out_specs = [
        pl.BlockSpec((None, ATT_HEADS, tm, LANES), lambda n_, i: (n_, 0, i, 0)),
        pl.BlockSpec((None, tm, D_ATT), row),
        pl.BlockSpec((None, tm, D_SSM), row),
        pl.BlockSpec((None, tm, D_SSM), row),
        pl.BlockSpec((None, D_ATT, tm), col),
        pl.BlockSpec((None, D_ATT, tm), col),
        pl.BlockSpec((None, ATT_HEADS, tm), col),
        pl.BlockSpec((None, D_ATT, tm), col),
        pl.BlockSpec((None, D_ATT, tm), col),
        pl.BlockSpec((None, ATT_HEADS, tm), col),
    ]
    in_specs = [
        pl.BlockSpec((None, tm, d), row),
        pl.BlockSpec((None, 1, d), lambda n_, i: (n_, 0, 0)),
        pl.BlockSpec((None, 1, d), lambda n_, i: (n_, 0, 0)),
        pl.BlockSpec((1, d), const2),
        pl.BlockSpec(w_nn.shape, const2),
        pl.BlockSpec(w_nt.shape, const2),
        pl.BlockSpec((ATT_HEADS, tm), const2),
    ]
    return pl.pallas_call(
        _pre_prompt_kernel, grid=(n, l // tm), in_specs=in_specs, out_specs=out_specs,
        out_shape=out_shape,
        scratch_shapes=[pltpu.VMEM((ATT_HEADS, LANES), f32), pltpu.VMEM((LANES, tm), f32)],
        compiler_params=_cparams(("arbitrary", "arbitrary")), name="pre",
    )(x, shift, scale, g_norm, w_nn, w_nt, jnp.broadcast_to(b_f[:, None], (ATT_HEADS, tm)))


def _pre_sample_kernel(x_ref, shift_ref, scale_ref, g_ref, wnn_ref, wnt_ref, bf_ref, seg_ref,
                       q_ref, k_ref, v_ref, zatt_ref, u_ref, zssm_ref, lf_ref, snew_ref):
    hb = _modulated_norm(x_ref[...], g_ref[...], scale_ref[...], shift_ref[...]).astype(bf16)
    p = jnp.dot(hb, wnn_ref[...], preferred_element_type=f32)
    pk = lax.dot_general(hb, wnt_ref[...], _NT, preferred_element_type=f32)
    q = p[:, :D_ATT] * QK_SCALE
    k = pk[:, :D_ATT]
    q_ref[...] = q
    k_ref[...] = k
    v_ref[...] = pk[:, D_ATT:2 * D_ATT]
    zatt_ref[...] = p[:, D_ATT:2 * D_ATT]
    u_ref[...] = p[:, 2 * D_ATT:2 * D_ATT + D_SSM]
    zssm_ref[...] = p[:, 2 * D_ATT + D_SSM:]
    lf_ref[...] = _log_sigmoid(pk[:, 2 * D_ATT:] + bf_ref[...])
    snew_ref[...] = jnp.dot(q * k, seg_ref[...], preferred_element_type=f32, precision=_HI)


def _pre_sample_call(x, shift, scale, g_norm, w_nn, w_nt, b_f_pad, seg):
    r, d = x.shape
    shp = lambda c: jax.ShapeDtypeStruct((r, c), f32)
    return pl.pallas_call(
        _pre_sample_kernel,
        out_shape=[shp(D_ATT), shp(D_ATT), shp(D_ATT), shp(D_ATT), shp(D_SSM), shp(D_SSM),
                   shp(NT_ROWS - 2 * D_ATT), shp(LANES)],
        compiler_params=pltpu.CompilerParams(vmem_limit_bytes=VMEM_LIMIT), name="pre_s",
    )(x, shift, scale, g_norm, w_nn, w_nt, b_f_pad, seg)


def _att_kernel(pt_ref, qa_ref, kt_ref, vt_ref, cum_ref, z_ref,
                snew_ref, lfnew_ref, vnew_ref, tri_ref, qb_hbm, kc_hbm, vc_hbm, lfc_hbm,
                o_ref, od_ref,
                ka, va, m_s, acc_s,
                kbuf, vbuf, lfbuf, qbuf, sc, bias, tot, p_s, dacc, stat, ksem, vsem, lfsem, qsem,
                *, tq, seq, n_tok, n_pages, ppu, ring):
    n = pl.program_id(0)
    qi = pl.program_id(1)
    n_tiles = seq // tq
    units = n_pages // ppu
    ahead = ring - 1
    k_end = n_tok * units
    v_lo, v_hi = units, (n_tok + 1) * units

    def start_pages(src, buf, sem, tok, sub, slot):
        for j in range(ppu):
            pltpu.make_async_copy(src.at[pt_ref[tok, sub * ppu + j]], buf.at[slot, j], sem.at[slot]).start()

    def wait_pages(src, buf, sem, slot):
        for j in range(ppu):
            pltpu.make_async_copy(src.at[0], buf.at[slot, j], sem.at[slot]).wait()

    def request(w):
        tok, sub, slot = w // units, w % units, w % ring

        @pl.when(w < k_end)
        def _():
            start_pages(kc_hbm, kbuf, ksem, tok, sub, slot)

        @pl.when((w >= v_lo) & (w < v_hi))
        def _():
            start_pages(vc_hbm, vbuf, vsem, tok - 1, sub, slot)

    def start_token_inputs(t):
        slot = t % 2

        def body(pg, carry):
            pltpu.make_async_copy(lfc_hbm.at[pt_ref[t, pg]], lfbuf.at[slot, pg], lfsem.at[slot]).start()
            return carry
        lax.fori_loop(0, n_pages, body, 0)
        pltpu.make_async_copy(qb_hbm.at[t], qbuf.at[slot], qsem.at[slot]).start()

    def new_token(t):
        slot = t % 2

        @pl.when(t + 1 < n_tok)
        def _():
            start_token_inputs(t + 1)

        def wait_lf(pg, carry):
            pltpu.make_async_copy(lfc_hbm.at[0], lfbuf.at[slot, pg], lfsem.at[slot]).wait()
            return carry
        lax.fori_loop(0, n_pages, wait_lf, 0)
        pltpu.make_async_copy(qb_hbm.at[0], qbuf.at[slot], qsem.at[slot]).wait()
        lf2 = lfbuf[slot].reshape(n_pages * ATT_HEADS, LANES)
        sums = sum(jnp.dot(part.astype(bf16), tri_ref[...], preferred_element_type=f32) for part in _split3(lf2))
        bias[slot] = sums[:, :LANES].reshape(n_pages, ATT_HEADS, LANES)
        tot[...] = sums[:, LANES:].reshape(n_pages, ATT_HEADS, LANES)
        lfnew = lfnew_ref[t]

        def suffix(i, run):
            pg = n_pages - 1 - i
            bias[slot, pg] = bias[slot, pg] + run + lfnew
            return run + tot[pg]
        lax.fori_loop(0, n_pages, suffix, jnp.zeros((ATT_HEADS, LANES), f32))

    def token_softmax(t):
        s_all = sc[t % 2] + bias[t % 2]
        snew = snew_ref[t]
        m = jnp.max(jnp.max(s_all, axis=0), axis=-1, keepdims=True)
        m = jnp.maximum(m, snew)
        p_all = jnp.exp(s_all - m[None])
        p_s[...] = p_all
        p_new = jnp.exp(snew - m)
        stat[0] = p_new
        stat[1] = jnp.sum(jnp.sum(p_all, axis=0), axis=-1, keepdims=True) + p_new
        dacc[...] = jnp.zeros_like(dacc)

    def token_output(t):
        ones_row = jnp.ones((ATT_HEADS, LANES), f32)
        head_row = lax.broadcasted_iota(jnp.int32, (ATT_HEADS, HEAD_DIM), 0)
        past = jnp.zeros((ATT_HEADS, HEAD_DIM), f32)
        for h in range(ATT_HEADS):
            r = lax.dot_general(ones_row, dacc[h], _NT, preferred_element_type=f32, precision=_HI)
            past = jnp.where(head_row == h, r, past)
        od_ref[t] = (past + stat[0][:, :HEAD_DIM] * vnew_ref[t]) * (1.0 / stat[1][:, :HEAD_DIM])

    def unit_prologue(w):
        t = w // units

        @pl.when(w % units == 0)
        def _():
            @pl.when((t >= 2) & (t <= n_tok + 1))
            def _():
                token_output(t - 2)

            @pl.when((t >= 1) & (t <= n_tok))
            def _():
                token_softmax(t - 1)

            @pl.when(t < n_tok)
            def _():
                new_token(t)

        request(w + ahead)

        @pl.when(w < k_end)
        def _():
            wait_pages(kc_hbm, kbuf, ksem, w % ring)

        @pl.when((w >= v_lo) & (w < v_hi))
        def _():
            wait_pages(vc_hbm, vbuf, vsem, w % ring)

    def unit_compute(w, h):
        tok, sub, slot = w // units, w % units, w % ring
        qh = qbuf[jnp.minimum(tok, n_tok - 1) % 2, h]
        a = dacc[h]
        for j in range(ppu):
            sc[tok % 2, sub * ppu + j, pl.ds(h, 1), :] = jnp.sum(kbuf[slot, j, h] * qh, axis=0, keepdims=True)
            pr = p_s[sub * ppu + j, pl.ds(h, 1), :]
            a = a + vbuf[slot, j, h] * jnp.broadcast_to(pr, (HEAD_DIM, LANES))
        dacc[h] = a

    @pl.when((n == 0) & (qi == 0))
    def _():
        vbuf[...] = jnp.zeros_like(vbuf)
        p_s[...] = jnp.zeros_like(p_s)
        dacc[...] = jnp.zeros_like(dacc)
        start_token_inputs(jnp.int32(0))
        for w0 in range(ahead):
            request(jnp.int32(w0))

    @pl.when(qi == 0)
    def _():
        c = cum_ref[...]
        r16 = lax.broadcasted_iota(jnp.int32, (16, seq), 0)
        for h in range(ATT_HEADS):
            ka[h, 0:HEAD_DIM, :] = kt_ref[h * HEAD_DIM:(h + 1) * HEAD_DIM, :]
            va[h, 0:HEAD_DIM, :] = vt_ref[h * HEAD_DIM:(h + 1) * HEAD_DIM, :]
            hi, mid, lo = _split3(c[h:h + 1, :])
            ext = jnp.where(r16 == 0, -hi, jnp.where(r16 == 1, -mid, jnp.where(r16 == 2, -lo,
                            jnp.where(r16 < 6, 1.0, 0.0))))
            ka[h, AUG:AUG + 16, :] = ext.astype(bf16)
            ka[h, AUG + 16:, :] = jnp.zeros((LANES - AUG - 16, seq), bf16)
            va[h, HEAD_DIM:, :] = jnp.ones((LANES - HEAD_DIM, seq), bf16)

    lane_q = lax.broadcasted_iota(jnp.int32, (tq, LANES), 1)
    row = lax.broadcasted_iota(jnp.int32, (tq, tq), 0)
    col = lax.broadcasted_iota(jnp.int32, (tq, tq), 1)
    causal = col <= row

    first_unit = n * (n_tiles * (n_tiles + 1) // 2) + (qi * (qi + 1)) // 2

    def block(kb, diagonal):
        w = first_unit if diagonal else first_unit + 1 + kb
        unit_prologue(w)
        k0 = pl.multiple_of(kb * tq, tq)
        for h in range(ATT_HEADS):
            unit_compute(w, h)
            s = jnp.dot(qa_ref[h], ka[h, :, pl.ds(k0, tq)], preferred_element_type=f32)
            if diagonal:
                s = jnp.where(causal, s, -jnp.inf)
            rowmax = jnp.broadcast_to(jnp.max(s, axis=-1, keepdims=True), (tq, LANES))
            m_new = rowmax if diagonal else jnp.maximum(m_s[h], rowmax)
            p = [jnp.exp2(s[:, j * LANES:(j + 1) * LANES] - m_new) for j in range(tq // LANES)]
            p = jnp.concatenate(p, axis=1)
            pv = lax.dot_general(p.astype(bf16), va[h, :, pl.ds(k0, tq)], _NT, preferred_element_type=f32)
            acc_s[h] = pv if diagonal else jnp.exp2(m_s[h] - m_new) * acc_s[h] + pv
            m_s[h] = m_new

    def off_diagonal(kb, carry):
        block(kb, False)
        return carry

    block(qi, True)
    lax.fori_loop(0, qi, off_diagonal, 0)

    for pr in range(ATT_HEADS // 2):
        a0, a1 = acc_s[2 * pr], acc_s[2 * pr + 1]
        o0 = a0 * pltpu.roll(1.0 / a0, HEAD_DIM, axis=1)
        o1 = pltpu.roll(a1, HEAD_DIM, axis=1) * (1.0 / a1)
        o = jnp.where(lane_q < HEAD_DIM, o0, o1)
        cols = slice(pr * LANES, (pr + 1) * LANES)
        o_ref[:, cols] = (o * _silu(z_ref[:, cols])).astype(bf16)


def _att_call(qa, ktb, vtb, cum, z_att, page_table, qb, snew_b, lfnew_b, vnew, kc, vc, lfc, tq, ppu, ring=4):
    n, _, l, _ = qa.shape
    ns, n_pages = page_table.shape
    n_tiles = l // tq
    units = n_pages // ppu
    assert n * (n_tiles * (n_tiles + 1) // 2) >= (ns + 1) * units + 1, "too few key-block steps for the page units"
    lane = jnp.arange(LANES)
    tri = jnp.concatenate([lane[:, None] > lane[None, :], jnp.ones((LANES, LANES), bool)], axis=1).astype(bf16)
    row = lambda n_, i, pt: (n_, i, 0)
    whole = lambda n_, i, pt: (n_, 0, 0)
    const = lambda nd: (lambda n_, i, pt: (0,) * nd)
    page = (ppu, ATT_HEADS, HEAD_DIM, LANES)
    grid_spec = pltpu.PrefetchScalarGridSpec(
        num_scalar_prefetch=1,
        grid=(n, n_tiles),
        in_specs=[pl.BlockSpec((None, ATT_HEADS, tq, LANES), lambda n_, i, pt: (n_, 0, i, 0)),
                  pl.BlockSpec((None, D_ATT, l), whole),
                  pl.BlockSpec((None, D_ATT, l), whole),
                  pl.BlockSpec((None, ATT_HEADS, l), whole),
                  pl.BlockSpec((None, tq, D_ATT), row),
                  pl.BlockSpec(snew_b.shape, const(3)),
                  pl.BlockSpec(lfnew_b.shape, const(3)),
                  pl.BlockSpec(vnew.shape, const(3)),
                  pl.BlockSpec((LANES, 2 * LANES), const(2)),
                  pl.BlockSpec(memory_space=pl.ANY),
                  pl.BlockSpec(memory_space=pl.ANY),
                  pl.BlockSpec(memory_space=pl.ANY),
                  pl.BlockSpec(memory_space=pl.ANY)],
        out_specs=[pl.BlockSpec((None, tq, D_ATT), row),
                   pl.BlockSpec((ns, ATT_HEADS, HEAD_DIM), const(3))],
        scratch_shapes=[pltpu.VMEM((ATT_HEADS, LANES, l), bf16), pltpu.VMEM((ATT_HEADS, LANES, l), bf16),
                        pltpu.VMEM((ATT_HEADS, tq, LANES), f32),
                        pltpu.VMEM((ATT_HEADS, tq, LANES), f32),
                        pltpu.VMEM((ring,) + page, f32),
                        pltpu.VMEM((ring,) + page, f32),
                        pltpu.VMEM((2, n_pages, ATT_HEADS, LANES), f32),
                        pltpu.VMEM((2, ATT_HEADS, HEAD_DIM, LANES), f32),
                        pltpu.VMEM((2, n_pages, ATT_HEADS, LANES), f32),
                        pltpu.VMEM((2, n_pages, ATT_HEADS, LANES), f32),
                        pltpu.VMEM((n_pages, ATT_HEADS, LANES), f32),
                        pltpu.VMEM((n_pages, ATT_HEADS, LANES), f32),
                        pltpu.VMEM((ATT_HEADS, HEAD_DIM, LANES), f32),
                        pltpu.VMEM((2, ATT_HEADS, LANES), f32),
                        pltpu.SemaphoreType.DMA((ring,)), pltpu.SemaphoreType.DMA((ring,)),
                        pltpu.SemaphoreType.DMA((2,)), pltpu.SemaphoreType.DMA((2,))],
    )
    return pl.pallas_call(
        functools.partial(_att_kernel, tq=tq, seq=l, n_tok=ns, n_pages=n_pages, ppu=ppu, ring=ring),
        grid_spec=grid_spec,
        out_shape=[jax.ShapeDtypeStruct((n, l, D_ATT), bf16),
                   jax.ShapeDtypeStruct((ns, ATT_HEADS, HEAD_DIM), f32)],
        compiler_params=_cparams(("arbitrary", "arbitrary")), name="att",
    )(page_table, qa, ktb, vtb, cum, z_att, snew_b, lfnew_b, vnew, tri, qb, kc, vc, lfc)


def _ssm_kernel(u_ref, h0r_ref, h0i_ref, ar_ref, ai_ref, bre_ref, bim_ref, cre_ref, cim_ref, d_ref,
                y_ref, hr_ref, hi_ref,
                utn, bu_re, bu_im, hs_re, hs_im, st_re, st_im, yscr, perm, *, nb, tl, time_major, chunk):
    i = pl.program_id(0)
    nj = D_SSM // LANES

    @pl.when(i == 0)
    def _():
        st_re[...] = h0r_ref[...]
        st_im[...] = h0i_ref[...]

    if time_major:
        utn[...] = u_ref[...].reshape(tl * nb, D_SSM)
    else:
        for j in range(nj):
            cols = slice(j * LANES, (j + 1) * LANES)
            for b in range(nb):
                perm[j, b * tl:(b + 1) * tl, :] = u_ref[b, :, cols]
            for t in range(tl):
                utn[t * nb:(t + 1) * nb, cols] = perm[j, pl.ds(t, nb, stride=tl), :]

    for c in range(nj):
        cols = slice(c * chunk, (c + 1) * chunk)
        ucols = slice(c * LANES, (c + 1) * LANES)
        ub = utn[:, ucols].astype(bf16)
        bu_re[:, cols] = jnp.dot(ub, bre_ref[c], preferred_element_type=f32)
        bu_im[:, cols] = jnp.dot(ub, bim_ref[c], preferred_element_type=f32)
        ar = jnp.broadcast_to(ar_ref[:, cols], (nb, chunk))
        ai = jnp.broadcast_to(ai_ref[:, cols], (nb, chunk))
        hr, hi = st_re[:, cols], st_im[:, cols]
        for t in range(tl):
            rows = slice(t * nb, (t + 1) * nb)
            hr, hi = (ar * hr - ai * hi + bu_re[rows, cols], ar * hi + ai * hr + bu_im[rows, cols])
            hs_re[rows, cols] = hr.astype(bf16)
            hs_im[rows, cols] = hi.astype(bf16)
        st_re[:, cols] = hr
        st_im[:, cols] = hi
        yscr[:, ucols] = (jnp.dot(hs_re[:, cols], cre_ref[c], preferred_element_type=f32)
                          - jnp.dot(hs_im[:, cols], cim_ref[c], preferred_element_type=f32)
                          + d_ref[:, ucols] * utn[:, ucols])
    if time_major:
        y_ref[...] = yscr[...].reshape(tl, nb, D_SSM)
    else:
        for j in range(nj):
            cols = slice(j * LANES, (j + 1) * LANES)
            for t in range(tl):
                perm[j, pl.ds(t, nb, stride=tl), :] = yscr[t * nb:(t + 1) * nb, cols]
            for b in range(nb):
                y_ref[b, :, cols] = perm[j, b * tl:(b + 1) * tl, :]

    @pl.when(i == pl.num_programs(0) - 1)
    def _():
        hr_ref[...] = st_re[...]
        hi_ref[...] = st_im[...]


def _ssm_call(u, h0_re, h0_im, ab_re, ab_im, b_re, b_im, c_re, c_im, d_skip, tl, time_major):
    nj = D_SSM // LANES
    if time_major:
        l, nb = u.shape[0], u.shape[1]
        ublock = pl.BlockSpec((tl, nb, D_SSM), lambda i: (i, 0, 0))
    else:
        nb, l = u.shape[0], u.shape[1]
        ublock = pl.BlockSpec((nb, tl, D_SSM), lambda i: (0, i, 0))
    rows = tl * nb
    full = lambda a: pl.BlockSpec(a.shape, lambda i, nd=a.ndim: (0,) * nd)
    return pl.pallas_call(
        functools.partial(_ssm_kernel, nb=nb, tl=tl, time_major=time_major, chunk=N_STATE // nj),
        grid=(l // tl,),
        in_specs=[ublock, full(h0_re), full(h0_im), full(ab_re), full(ab_im),
                  full(b_re), full(b_im), full(c_re), full(c_im), full(d_skip)],
        out_specs=[ublock, full(h0_re), full(h0_im)],
        out_shape=[jax.ShapeDtypeStruct(u.shape, f32),
                   jax.ShapeDtypeStruct(h0_re.shape, f32), jax.ShapeDtypeStruct(h0_im.shape, f32)],
        scratch_shapes=[pltpu.VMEM((rows, D_SSM), f32),
                        pltpu.VMEM((rows, N_STATE), f32), pltpu.VMEM((rows, N_STATE), f32),
                        pltpu.VMEM((rows, N_STATE), bf16), pltpu.VMEM((rows, N_STATE), bf16),
                        pltpu.VMEM((nb, N_STATE), f32), pltpu.VMEM((nb, N_STATE), f32),
                        pltpu.VMEM((rows, D_SSM), f32),
                        pltpu.VMEM((nj, rows, LANES), f32)],
        compiler_params=_cparams(("arbitrary",)), name="ssm",
    )(u, h0_re, h0_im, ab_re, ab_im, b_re, b_im, c_re, c_im, d_skip)


def _post_kernel(*refs, gate_att):
    if gate_att:
        x_ref, att_ref, zatt_ref, y_ref, zssm_ref, gate_ref, wglu_ref, bglu_ref, wout_ref, gf_ref, o_ref = refs
        att = att_ref[...] * _silu(zatt_ref[...])
    else:
        x_ref, att_ref, y_ref, zssm_ref, gate_ref, wglu_ref, bglu_ref, wout_ref, gf_ref, o_ref = refs
        att = att_ref[...]
    g = jax.nn.gelu(y_ref[...])
    glu = jax.nn.sigmoid(jnp.dot(g.astype(bf16), wglu_ref[...], preferred_element_type=f32) + bglu_ref[...])
    ssm = g * glu * _silu(zssm_ref[...])
    mix = jnp.concatenate([att.astype(bf16), ssm.astype(bf16)], axis=-1)
    x = x_ref[...]
    xo = x + gate_ref[...] * jnp.dot(mix, wout_ref[...], preferred_element_type=f32)
    o_ref[...] = xo * lax.rsqrt(jnp.mean(xo * xo, axis=-1, keepdims=True) + EPS) * gf_ref[...]


def _post_call(x, att, z_att, y, z_ssm, gate, w_glu, b_glu, w_out, g_final, tm):
    n, l, d = x.shape
    row = lambda n_, i: (n_, i, 0)
    const2 = lambda n_, i: (0, 0)
    gate_spec = (pl.BlockSpec((None, 1, d), lambda n_, i: (n_, 0, 0)) if gate.shape[1] == 1
                 else pl.BlockSpec((None, tm, d), row))
    ins = [x, att] + ([z_att] if z_att is not None else []) + [y, z_ssm, gate, w_glu, b_glu, w_out, g_final]
    in_specs = ([pl.BlockSpec((None, tm, d), row), pl.BlockSpec((None, tm, D_ATT), row)]
                + ([pl.BlockSpec((None, tm, D_ATT), row)] if z_att is not None else [])
                + [pl.BlockSpec((None, tm, D_SSM), row),
                   pl.BlockSpec((None, tm, D_SSM), row), gate_spec,
                   pl.BlockSpec(w_glu.shape, const2), pl.BlockSpec((1, D_SSM), const2),
                   pl.BlockSpec(w_out.shape, const2), pl.BlockSpec((1, d), const2)])
    return pl.pallas_call(
        functools.partial(_post_kernel, gate_att=z_att is not None),
        grid=(n, l // tm), in_specs=in_specs,
        out_specs=pl.BlockSpec((None, tm, d), row),
        out_shape=jax.ShapeDtypeStruct((n, l, d), f32),
        compiler_params=_cparams(("arbitrary", "arbitrary")), name="post",
    )(*ins)


def kernel(x_prompt, x_sample, c_prompt, c_sample, cache_k, cache_v, cache_logf, state_ssm_re, state_ssm_im,
           page_table, g_norm, w_ada, b_ada, w_in, b_fgate, a_re, a_im, log_dt, b_re, b_im, c_re, c_im, d_skip,
           w_glu, b_glu, w_out, g_final):
    n, l, d = x_prompt.shape
    ns = x_sample.shape[0]
    depth = w_in.shape[0]
    assert depth == 1 and x_sample.shape[1] == 1
    lyr = 0
    xs = x_sample.reshape(ns, d)

    splits = [0, D_ATT, 2 * D_ATT, 3 * D_ATT, 4 * D_ATT, 4 * D_ATT + ATT_HEADS,
              4 * D_ATT + ATT_HEADS + D_SSM, 4 * D_ATT + ATT_HEADS + 2 * D_SSM]
    w_t = jnp.swapaxes(w_in[lyr], 0, 1)
    wq, wk, wv, wz, wf, wu, wzs = [w_t[splits[i]:splits[i + 1]] for i in range(7)]
    w_nn = jnp.concatenate([wq, wz, wu, wzs], axis=0).T.astype(bf16)
    w_nt = jnp.concatenate([wk, wv, wf, jnp.zeros((NT_ROWS - 2 * D_ATT - ATT_HEADS, d), f32)], axis=0).astype(bf16)
    b_f = b_fgate[lyr]
    b_f_pad = jnp.concatenate([b_f, jnp.zeros((NT_ROWS - 2 * D_ATT - ATT_HEADS,), f32)])[None, :]
    seg = (jnp.arange(D_ATT)[:, None] // HEAD_DIM == jnp.arange(LANES)[None, :]).astype(f32)
    gn = g_norm[lyr][None, :]
    gf = g_final[None, :]
    w_glu_b = w_glu[lyr].astype(bf16)
    w_out_b = w_out[lyr].astype(bf16)
    b_glu2 = b_glu[lyr][None, :]

    mod = _ada_call(jnp.concatenate([c_prompt, c_sample], axis=0), w_ada[lyr], b_ada[lyr][None, :])
    shift_p, scale_p, gate_p = [mod[:n, k * d:(k + 1) * d].reshape(n, 1, d) for k in range(3)]
    shift_s, scale_s, gate_s = [mod[n:, k * d:(k + 1) * d] for k in range(3)]

    ab_re, ab_im, bb_re, bb_im = _ssm_param_call(a_re[lyr], a_im[lyr], log_dt[lyr], b_re[lyr], b_im[lyr])
    ab_re_row = ab_re.reshape(1, N_STATE)
    ab_im_row = ab_im.reshape(1, N_STATE)
    slabs = D_SSM // LANES
    slab_diag = lambda w: _block_diag(w.reshape((slabs, SSM_GROUPS // slabs) + w.shape[1:]))
    bd_re = slab_diag(bb_re).astype(bf16)
    bd_im = slab_diag(bb_im).astype(bf16)
    cd_re = slab_diag(jnp.swapaxes(c_re[lyr], 1, 2)).astype(bf16)
    cd_im = slab_diag(jnp.swapaxes(c_im[lyr], 1, 2)).astype(bf16)
    d_row = d_skip[lyr].reshape(1, D_SSM)

    tm = min(512, l)
    qa, z_att, u_p, z_ssm, kt, vt, lft, ktb, vtb, cum = _pre_prompt_call(x_prompt, shift_p, scale_p, gn, w_nn, w_nt,
                                                                         b_f, tm)
    qs, k_s, v_s, zatt_s, u_s, zssm_s, lf_s, snew = _pre_sample_call(xs, shift_s, scale_s, gn, w_nn, w_nt,
                                                                     b_f_pad, seg)

    qb = jnp.broadcast_to(qs.reshape(ns, ATT_HEADS, HEAD_DIM, 1), (ns, ATT_HEADS, HEAD_DIM, LANES))
    snew_b = jnp.broadcast_to(snew[:, :ATT_HEADS, None], (ns, ATT_HEADS, LANES))
    lfnew_b = jnp.broadcast_to(lf_s[:, :ATT_HEADS, None], (ns, ATT_HEADS, LANES))
    kc = jnp.transpose(cache_k[lyr], (0, 2, 3, 1))
    vc = jnp.transpose(cache_v[lyr], (0, 2, 3, 1))
    lfc = jnp.transpose(cache_logf[lyr], (0, 2, 1))
    n_pages = page_table.shape[1]
    att, att_s = _att_call(qa, ktb, vtb, cum, z_att, page_table, qb, snew_b, lfnew_b,
                           v_s.reshape(ns, ATT_HEADS, HEAD_DIM), kc, vc, lfc,
                           tq=min(256, l), ppu=min(8, n_pages // 2))

    zeros_state = jnp.zeros((n, N_STATE), f32)
    y_p, hp_re, hp_im = _ssm_call(u_p, zeros_state, zeros_state, ab_re_row, ab_im_row, bd_re, bd_im, cd_re, cd_im,
                                  d_row, tl=min(64, l), time_major=False)
    y_prompt = _post_call(x_prompt, att, None, y_p, z_ssm, gate_p, w_glu_b, b_glu2, w_out_b, gf, min(1024, l))

    heads_t = lambda t: jnp.transpose(t.reshape(n, ATT_HEADS, HEAD_DIM, l), (0, 3, 1, 2))[None]
    k_prompt = heads_t(kt)
    v_prompt = heads_t(vt)
    logf_prompt = jnp.transpose(lft, (0, 2, 1))[None]

    h0r = state_ssm_re[lyr].reshape(ns, N_STATE)
    h0i = state_ssm_im[lyr].reshape(ns, N_STATE)
    y_s, hs_re, hs_im = _ssm_call(u_s[None], h0r, h0i, ab_re_row, ab_im_row, bd_re, bd_im, cd_re, cd_im, d_row,
                                  tl=1, time_major=True)
    y_sample = _post_call(xs[None], att_s.reshape(1, ns, D_ATT), zatt_s[None], y_s, zssm_s[None], gate_s[None],
                          w_glu_b, b_glu2, w_out_b, gf, ns)

    st = lambda a, b: a.reshape(1, b, SSM_GROUPS, SSM_STATE)
    return (y_prompt, y_sample.reshape(ns, 1, d),
            k_prompt, v_prompt, logf_prompt, st(hp_re, n), st(hp_im, n),
            k_s.reshape(1, ns, 1, ATT_HEADS, HEAD_DIM), v_s.reshape(1, ns, 1, ATT_HEADS, HEAD_DIM),
            lf_s[:, :ATT_HEADS].reshape(1, ns, 1, ATT_HEADS), st(hs_re, ns), st(hs_im, ns))
```

```python
import functools
import math

import jax
import jax.numpy as jnp
from jax import lax
from jax.experimental import pallas as pl
from jax.experimental.pallas import tpu as pltpu

f32 = jnp.float32
bf16 = jnp.bfloat16

HEAD_DIM = 64
ATT_HEADS = 8
D_ATT = ATT_HEADS * HEAD_DIM
SSM_GROUP = 16
SSM_GROUPS = 32
SSM_STATE = 64
D_SSM = SSM_GROUP * SSM_GROUPS
N_STATE = SSM_GROUPS * SSM_STATE
EPS = 1e-6
QK_SCALE = HEAD_DIM ** -0.5
LOG2E = 1.0 / math.log(2.0)
LANES = 128
NT_ROWS = 2 * D_ATT + 16
VMEM_LIMIT = 56 * 1024 * 1024

_NT = (((1,), (1,)), ((), ()))
_HI = lax.Precision.HIGHEST


def _silu(x):
    return x * jax.nn.sigmoid(x)


def _log_sigmoid(x):
    return jnp.minimum(x, 0.0) - jnp.log1p(jnp.exp(-jnp.abs(x)))


def _modulated_norm(x, g, scale, shift):
    y = x * lax.rsqrt(jnp.mean(x * x, axis=-1, keepdims=True) + EPS)
    return (y * g) * (1.0 + scale) + shift


def _cparams(sem, vmem=VMEM_LIMIT):
    return pltpu.CompilerParams(dimension_semantics=sem, vmem_limit_bytes=vmem)


def _ada_kernel(c_ref, w_ref, b_ref, o_ref):
    s = _silu(c_ref[...]).astype(bf16)
    o_ref[...] = jnp.dot(s, w_ref[...].astype(bf16), preferred_element_type=f32) + b_ref[...]


def _ada_call(c_all, w_ada, b_ada):
    r, d = c_all.shape
    n_out = w_ada.shape[1]
    tn = 768
    return pl.pallas_call(
        _ada_kernel,
        grid=(n_out // tn,),
        in_specs=[pl.BlockSpec((r, d), lambda j: (0, 0)),
                  pl.BlockSpec((d, tn), lambda j: (0, j)),
                  pl.BlockSpec((1, tn), lambda j: (0, j))],
        out_specs=pl.BlockSpec((r, tn), lambda j: (0, j)),
        out_shape=jax.ShapeDtypeStruct((r, n_out), f32),
        compiler_params=_cparams(("arbitrary",)),
        name="ada",
    )(c_all, w_ada, b_ada)


def _ssm_param_kernel(are_ref, aim_ref, ldt_ref, bre_ref, bim_ref,
                      abr_ref, abi_ref, bbr_ref, bbi_ref):
    lr, li = are_ref[...], aim_ref[...]
    dt = jnp.exp(ldt_ref[...])
    mag = jnp.exp(lr * dt)
    abr = mag * jnp.cos(li * dt)
    abi = mag * jnp.sin(li * dt)
    xr, xi = abr - 1.0, abi
    den = lr * lr + li * li
    cr = (xr * lr + xi * li) / den
    ci = (xi * lr - xr * li) / den
    br, bi = bre_ref[...], bim_ref[...]
    abr_ref[...] = abr
    abi_ref[...] = abi
    bbr_ref[...] = cr * br - ci * bi
    bbi_ref[...] = cr * bi + ci * br


def _ssm_param_call(a_re, a_im, log_dt, b_re, b_im):
    g, p = a_re.shape
    hc = b_re.shape[-1]
    rep = lambda a: jnp.broadcast_to(a[:, None, :], (g, hc, p)).reshape(g * hc, p)
    ldt = jnp.broadcast_to(log_dt[:, None, None], (g, hc, p)).reshape(g * hc, p)
    bt = lambda b: jnp.swapaxes(b, 1, 2).reshape(g * hc, p)
    shp = jax.ShapeDtypeStruct((g * hc, p), f32)
    abr, abi, bbr, bbi = pl.pallas_call(
        _ssm_param_kernel, out_shape=[shp] * 4, name="ssm_par",
    )(rep(a_re), rep(a_im), ldt, bt(b_re), bt(b_im))
    first = lambda a: a.reshape(g, hc, p)[:, 0, :]
    return first(abr), first(abi), bbr.reshape(g, hc, p), bbi.reshape(g, hc, p)


def _block_diag(blocks):
    *lead, g, r, c = blocks.shape
    eye = jnp.eye(g, dtype=blocks.dtype)
    return (blocks[..., :, :, None, :] * eye[:, None, :, None]).reshape(*lead, g * r, g * c)


def _split3(x):
    hi = x.astype(bf16).astype(f32)
    r = x - hi
    mid = r.astype(bf16).astype(f32)
    return hi, mid, r - mid


AUG = HEAD_DIM


def _pre_prompt_kernel(x_ref, shift_ref, scale_ref, g_ref, wnn_ref, wnt_ref, bf_ref,
                       qa_ref, zatt_ref, u_ref, zssm_ref, kt_ref, vt_ref, lft_ref, ktb_ref, vtb_ref, cum_ref,
                       carry, parts):
    i = pl.program_id(1)
    tm = x_ref.shape[0]

    @pl.when(i == 0)
    def _():
        carry[...] = jnp.zeros_like(carry)

    @pl.when((pl.program_id(0) == 0) & (i == 0))
    def _():
        parts[...] = jnp.zeros_like(parts)

    h = _modulated_norm(x_ref[...], g_ref[...], scale_ref[...], shift_ref[...])
    hb = h.astype(bf16)
    pt = lax.dot_general(wnt_ref[...], hb, _NT, preferred_element_type=f32)
    kt_ref[...] = pt[:D_ATT]
    vt_ref[...] = pt[D_ATT:2 * D_ATT]
    ktb_ref[...] = pt[:D_ATT].astype(bf16)
    vtb_ref[...] = pt[D_ATT:2 * D_ATT].astype(bf16)
    lf = _log_sigmoid(pt[2 * D_ATT:2 * D_ATT + ATT_HEADS] + bf_ref[...])
    lft_ref[...] = lf
    p = jnp.dot(hb, wnn_ref[...], preferred_element_type=f32)
    zatt_ref[...] = p[:, D_ATT:2 * D_ATT]
    u_ref[...] = p[:, 2 * D_ATT:2 * D_ATT + D_SSM]
    zssm_ref[...] = p[:, 2 * D_ATT + D_SSM:]

    c = lf
    lane = lax.broadcasted_iota(jnp.int32, c.shape, 1)
    d = 1
    while d < tm:
        c = c + jnp.where(lane >= d, pltpu.roll(c, d, axis=1), 0.0)
        d *= 2
    c = c + jnp.concatenate([carry[...]] * (tm // LANES), axis=1)
    carry[...] = jnp.broadcast_to(c[:, tm - 1:tm], carry.shape)
    c = c * LOG2E
    cum_ref[...] = c
    for j, part in enumerate(_split3(c)):
        for hd in range(ATT_HEADS):
            parts[3 * hd + j:3 * hd + j + 1, :] = part[hd:hd + 1, :]
    cq_parts = parts[...].T
    lane_q = lax.broadcasted_iota(jnp.int32, (tm, LANES), 1)
    ones_lanes = jnp.where(lane_q < AUG + 3, 1.0, 0.0)
    for pr in range(ATT_HEADS // 2):
        qp = p[:, pr * LANES:(pr + 1) * LANES] * (QK_SCALE * LOG2E)
        for hh in range(2):
            hd = 2 * pr + hh
            qh = qp if hh == 0 else pltpu.roll(qp, HEAD_DIM, axis=1)
            cq = pltpu.roll(cq_parts, AUG + 3 - 3 * hd, axis=1)
            ext = jnp.where((lane_q >= AUG + 3) & (lane_q < AUG + 6), cq, ones_lanes)
            qa_ref[hd] = jnp.where(lane_q < HEAD_DIM, qh, ext).astype(bf16)


def _pre_prompt_call(x, shift, scale, g_norm, w_nn, w_nt, b_f, tm):
    n, l, d = x.shape
    row = lambda n_, i: (n_, i, 0)
    col = lambda n_, i: (n_, 0, i)
    const2 = lambda n_, i: (0, 0)
    out_shape = [
        jax.ShapeDtypeStruct((n, ATT_HEADS, l, LANES), bf16),
        jax.ShapeDtypeStruct((n, l, D_ATT), f32),
        jax.ShapeDtypeStruct((n, l, D_SSM), f32),
        jax.ShapeDtypeStruct((n, l, D_SSM), f32),
        jax.ShapeDtypeStruct((n, D_ATT, l), f32),
        jax.ShapeDtypeStruct((n, D_ATT, l), f32),
        jax.ShapeDtypeStruct((n, ATT_HEADS, l), f32),
        jax.ShapeDtypeStruct((n, D_ATT, l), bf16),
        jax.ShapeDtypeStruct((n, D_ATT, l), bf16),
        jax.ShapeDtypeStruct((n, ATT_HEADS, l), f32),
    ]
    out_specs = [
        pl.BlockSpec((None, ATT_HEADS, tm, LANES), lambda n_, i: (n_, 0, i, 0)),
        pl.BlockSpec((None, tm, D_ATT), row),
        pl.BlockSpec((None, tm, D_SSM), row),
        pl.BlockSpec((None, tm, D_SSM), row),
        pl.BlockSpec((None, D_ATT, tm), col),
        pl.BlockSpec((None, D_ATT, tm), col),
        pl.BlockSpec((None, ATT_HEADS, tm), col),
        pl.BlockSpec((None, D_ATT, tm), col),
        pl.BlockSpec((None, D_ATT, tm), col),
        pl.BlockSpec((None, ATT_HEADS, tm), col),
    ]
    in_specs = [
        pl.BlockSpec((None, tm, d), row),
        pl.BlockSpec((None, 1, d), lambda n_, i: (n_, 0, 0)),
        pl.BlockSpec((None, 1, d), lambda n_, i: (n_, 0, 0)),
        pl.BlockSpec((1, d), const2),
        pl.BlockSpec(w_nn.shape, const2),
        pl.BlockSpec(w_nt.shape, const2),
        pl.BlockSpec((ATT_HEADS, tm), const2),
    ]
    return pl.pallas_call(
        _pre_prompt_kernel, grid=(n, l // tm), in_specs=in_specs, out_specs=out_specs,
        out_shape=out_shape,
        scratch_shapes=[pltpu.VMEM((ATT_HEADS, LANES), f32), pltpu.VMEM((LANES, tm), f32)],
        compiler_params=_cparams(("arbitrary", "arbitrary")), name="pre",
    )(x, shift, scale, g_norm, w_nn, w_nt, jnp.broadcast_to(b_f[:, None], (ATT_HEADS, tm)))


def _pre_sample_kernel(x_ref, shift_ref, scale_ref, g_ref, wnn_ref, wnt_ref, bf_ref, seg_ref,
                       q_ref, k_ref, v_ref, zatt_ref, u_ref, zssm_ref, lf_ref, snew_ref):
    hb = _modulated_norm(x_ref[...], g_ref[...], scale_ref[...], shift_ref[...]).astype(bf16)
    p = jnp.dot(hb, wnn_ref[...], preferred_element_type=f32)
    pk = lax.dot_general(hb, wnt_ref[...], _NT, preferred_element_type=f32)
    q = p[:, :D_ATT] * QK_SCALE
    k = pk[:, :D_ATT]
    q_ref[...] = q
    k_ref[...] = k
    v_ref[...] = pk[:, D_ATT:2 * D_ATT]
    zatt_ref[...] = p[:, D_ATT:2 * D_ATT]
    u_ref[...] = p[:, 2 * D_ATT:2 * D_ATT + D_SSM]
    zssm_ref[...] = p[:, 2 * D_ATT + D_SSM:]
    lf_ref[...] = _log_sigmoid(pk[:, 2 * D_ATT:] + bf_ref[...])
    snew_ref[...] = jnp.dot(q * k, seg_ref[...], preferred_element_type=f32, precision=_HI)


def _pre_sample_call(x, shift, scale, g_norm, w_nn, w_nt, b_f_pad, seg):
    r, d = x.shape
    shp = lambda c: jax.ShapeDtypeStruct((r, c), f32)
    return pl.pallas_call(
        _pre_sample_kernel,
        out_shape=[shp(D_ATT), shp(D_ATT), shp(D_ATT), shp(D_ATT), shp(D_SSM), shp(D_SSM),
                   shp(NT_ROWS - 2 * D_ATT), shp(LANES)],
        compiler_params=pltpu.CompilerParams(vmem_limit_bytes=VMEM_LIMIT), name="pre_s",
    )(x, shift, scale, g_norm, w_nn, w_nt, b_f_pad, seg)


def _att_kernel(pt_ref, qa_ref, kt_ref, vt_ref, cum_ref, z_ref,
                snew_ref, lfnew_ref, vnew_ref, tri_ref, qb_hbm, kc_hbm, vc_hbm, lfc_hbm,
                o_ref, od_ref,
                ka, va, m_s, acc_s,
                kbuf, vbuf, lfbuf, qbuf, sc, bias, tot, p_s, dacc, stat, ksem, vsem, lfsem, qsem,
                *, tq, seq, n_tok, n_pages, ppu, ring):
    n = pl.program_id(0)
    qi = pl.program_id(1)
    n_tiles = seq // tq
    units = n_pages // ppu
    ahead = ring - 1
    k_end = n_tok * units
    v_lo, v_hi = units, (n_tok + 1) * units

    def start_pages(src, buf, sem, tok, sub, slot):
        for j in range(ppu):
            pltpu.make_async_copy(src.at[pt_ref[tok, sub * ppu + j]], buf.at[slot, j], sem.at[slot]).start()

    def wait_pages(src, buf, sem, slot):
        for j in range(ppu):
            pltpu.make_async_copy(src.at[0], buf.at[slot, j], sem.at[slot]).wait()

    def request(w):
        tok, sub, slot = w // units, w % units, w % ring

        @pl.when(w < k_end)
        def _():
            start_pages(kc_hbm, kbuf, ksem, tok, sub, slot)

        @pl.when((w >= v_lo) & (w < v_hi))
        def _():
            start_pages(vc_hbm, vbuf, vsem, tok - 1, sub, slot)

    def start_token_inputs(t):
        slot = t % 2

        def body(pg, carry):
            pltpu.make_async_copy(lfc_hbm.at[pt_ref[t, pg]], lfbuf.at[slot, pg], lfsem.at[slot]).start()
            return carry
        lax.fori_loop(0, n_pages, body, 0)
        pltpu.make_async_copy(qb_hbm.at[t], qbuf.at[slot], qsem.at[slot]).start()

    def new_token(t):
        slot = t % 2

        @pl.when(t + 1 < n_tok)
        def _():
            start_token_inputs(t + 1)

        def wait_lf(pg, carry):
            pltpu.make_async_copy(lfc_hbm.at[0], lfbuf.at[slot, pg], lfsem.at[slot]).wait()
            return carry
        lax.fori_loop(0, n_pages, wait_lf, 0)
        pltpu.make_async_copy(qb_hbm.at[0], qbuf.at[slot], qsem.at[slot]).wait()
        lf2 = lfbuf[slot].reshape(n_pages * ATT_HEADS, LANES)
        sums = sum(jnp.dot(part.astype(bf16), tri_ref[...], preferred_element_type=f32) for part in _split3(lf2))
        bias[slot] = sums[:, :LANES].reshape(n_pages, ATT_HEADS, LANES)
        tot[...] = sums[:, LANES:].reshape(n_pages, ATT_HEADS, LANES)
        lfnew = lfnew_ref[t]

        def suffix(i, run):
            pg = n_pages - 1 - i
            bias[slot, pg] = bias[slot, pg] + run + lfnew
            return run + tot[pg]
        lax.fori_loop(0, n_pages, suffix, jnp.zeros((ATT_HEADS, LANES), f32))

    def token_softmax(t):
        s_all = sc[t % 2] + bias[t % 2]
        snew = snew_ref[t]
        m = jnp.max(jnp.max(s_all, axis=0), axis=-1, keepdims=True)
        m = jnp.maximum(m, snew)
        p_all = jnp.exp(s_all - m[None])
        p_s[...] = p_all
        p_new = jnp.exp(snew - m)
        stat[0] = p_new
        stat[1] = jnp.sum(jnp.sum(p_all, axis=0), axis=-1, keepdims=True) + p_new
        dacc[...] = jnp.zeros_like(dacc)

    def token_output(t):
        ones_row = jnp.ones((ATT_HEADS, LANES), f32)
        head_row = lax.broadcasted_iota(jnp.int32, (ATT_HEADS, HEAD_DIM), 0)
        past = jnp.zeros((ATT_HEADS, HEAD_DIM), f32)
        for h in range(ATT_HEADS):
            r = lax.dot_general(ones_row, dacc[h], _NT, preferred_element_type=f32, precision=_HI)
            past = jnp.where(head_row == h, r, past)
        od_ref[t] = (past + stat[0][:, :HEAD_DIM] * vnew_ref[t]) * (1.0 / stat[1][:, :HEAD_DIM])

    def unit_prologue(w):
        t = w // units

        @pl.when(w % units == 0)
        def _():
            @pl.when((t >= 2) & (t <= n_tok + 1))
            def _():
                token_output(t - 2)

            @pl.when((t >= 1) & (t <= n_tok))
            def _():
                token_softmax(t - 1)

            @pl.when(t < n_tok)
            def _():
                new_token(t)

        request(w + ahead)

        @pl.when(w < k_end)
        def _():
            wait_pages(kc_hbm, kbuf, ksem, w % ring)

        @pl.when((w >= v_lo) & (w < v_hi))
        def _():
            wait_pages(vc_hbm, vbuf, vsem, w % ring)

    def unit_compute(w, h):
        tok, sub, slot = w // units, w % units, w % ring
        qh = qbuf[jnp.minimum(tok, n_tok - 1) % 2, h]
        a = dacc[h]
        for j in range(ppu):
            sc[tok % 2, sub * ppu + j, pl.ds(h, 1), :] = jnp.sum(kbuf[slot, j, h] * qh, axis=0, keepdims=True)
            pr = p_s[sub * ppu + j, pl.ds(h, 1), :]
            a = a + vbuf[slot, j, h] * jnp.broadcast_to(pr, (HEAD_DIM, LANES))
        dacc[h] = a

    @pl.when((n == 0) & (qi == 0))
    def _():
        vbuf[...] = jnp.zeros_like(vbuf)
        p_s[...] = jnp.zeros_like(p_s)
        dacc[...] = jnp.zeros_like(dacc)
        start_token_inputs(jnp.int32(0))
        for w0 in range(ahead):
            request(jnp.int32(w0))

    @pl.when(qi == 0)
    def _():
        c = cum_ref[...]
        r16 = lax.broadcasted_iota(jnp.int32, (16, seq), 0)
        for h in range(ATT_HEADS):
            ka[h, 0:HEAD_DIM, :] = kt_ref[h * HEAD_DIM:(h + 1) * HEAD_DIM, :]
            va[h, 0:HEAD_DIM, :] = vt_ref[h * HEAD_DIM:(h + 1) * HEAD_DIM, :]
            hi, mid, lo = _split3(c[h:h + 1, :])
            ext = jnp.where(r16 == 0, -hi, jnp.where(r16 == 1, -mid, jnp.where(r16 == 2, -lo,
                            jnp.where(r16 < 6, 1.0, 0.0))))
            ka[h, AUG:AUG + 16, :] = ext.astype(bf16)
            ka[h, AUG + 16:, :] = jnp.zeros((LANES - AUG - 16, seq), bf16)
            va[h, HEAD_DIM:, :] = jnp.ones((LANES - HEAD_DIM, seq), bf16)

    lane_q = lax.broadcasted_iota(jnp.int32, (tq, LANES), 1)
    row = lax.broadcasted_iota(jnp.int32, (tq, tq), 0)
    col = lax.broadcasted_iota(jnp.int32, (tq, tq), 1)
    causal = col <= row

    first_unit = n * (n_tiles * (n_tiles + 1) // 2) + (qi * (qi + 1)) // 2

    def block(kb, diagonal):
        w = first_unit if diagonal else first_unit + 1 + kb
        unit_prologue(w)
        k0 = pl.multiple_of(kb * tq, tq)
        for h in range(ATT_HEADS):
            unit_compute(w, h)
            s = jnp.dot(qa_ref[h], ka[h, :, pl.ds(k0, tq)], preferred_element_type=f32)
            if diagonal:
                s = jnp.where(causal, s, -jnp.inf)
            rowmax = jnp.broadcast_to(jnp.max(s, axis=-1, keepdims=True), (tq, LANES))
            m_new = rowmax if diagonal else jnp.maximum(m_s[h], rowmax)
            p = [jnp.exp2(s[:, j * LANES:(j + 1) * LANES] - m_new) for j in range(tq // LANES)]
            p = jnp.concatenate(p, axis=1)
            pv = lax.dot_general(p.astype(bf16), va[h, :, pl.ds(k0, tq)], _NT, preferred_element_type=f32)
            acc_s[h] = pv if diagonal else jnp.exp2(m_s[h] - m_new) * acc_s[h] + pv
            m_s[h] = m_new

    def off_diagonal(kb, carry):
        block(kb, False)
        return carry

    block(qi, True)
    lax.fori_loop(0, qi, off_diagonal, 0)

    for pr in range(ATT_HEADS // 2):
        a0, a1 = acc_s[2 * pr], acc_s[2 * pr + 1]
        o0 = a0 * pltpu.roll(1.0 / a0, HEAD_DIM, axis=1)
        o1 = pltpu.roll(a1, HEAD_DIM, axis=1) * (1.0 / a1)
        o = jnp.where(lane_q < HEAD_DIM, o0, o1)
        cols = slice(pr * LANES, (pr + 1) * LANES)
        o_ref[:, cols] = (o * _silu(z_ref[:, cols])).astype(bf16)


def _att_call(qa, ktb, vtb, cum, z_att, page_table, qb, snew_b, lfnew_b, vnew, kc, vc, lfc, tq, ppu, ring=4):
    n, _, l, _ = qa.shape
    ns, n_pages = page_table.shape
    n_tiles = l // tq
    units = n_pages // ppu
    assert n * (n_tiles * (n_tiles + 1) // 2) >= (ns + 1) * units + 1, "too few key-block steps for the page units"
    lane = jnp.arange(LANES)
    tri = jnp.concatenate([lane[:, None] > lane[None, :], jnp.ones((LANES, LANES), bool)], axis=1).astype(bf16)
    row = lambda n_, i, pt: (n_, i, 0)
    whole = lambda n_, i, pt: (n_, 0, 0)
    const = lambda nd: (lambda n_, i, pt: (0,) * nd)
    page = (ppu, ATT_HEADS, HEAD_DIM, LANES)
    grid_spec = pltpu.PrefetchScalarGridSpec(
        num_scalar_prefetch=1,
        grid=(n, n_tiles),
        in_specs=[pl.BlockSpec((None, ATT_HEADS, tq, LANES), lambda n_, i, pt: (n_, 0, i, 0)),
                  pl.BlockSpec((None, D_ATT, l), whole),
                  pl.BlockSpec((None, D_ATT, l), whole),
                  pl.BlockSpec((None, ATT_HEADS, l), whole),
                  pl.BlockSpec((None, tq, D_ATT), row),
                  pl.BlockSpec(snew_b.shape, const(3)),
                  pl.BlockSpec(lfnew_b.shape, const(3)),
                  pl.BlockSpec(vnew.shape, const(3)),
                  pl.BlockSpec((LANES, 2 * LANES), const(2)),
                  pl.BlockSpec(memory_space=pl.ANY),
                  pl.BlockSpec(memory_space=pl.ANY),
                  pl.BlockSpec(memory_space=pl.ANY),
                  pl.BlockSpec(memory_space=pl.ANY)],
        out_specs=[pl.BlockSpec((None, tq, D_ATT), row),
                   pl.BlockSpec((ns, ATT_HEADS, HEAD_DIM), const(3))],
        scratch_shapes=[pltpu.VMEM((ATT_HEADS, LANES, l), bf16), pltpu.VMEM((ATT_HEADS, LANES, l), bf16),
                        pltpu.VMEM((ATT_HEADS, tq, LANES), f32),
                        pltpu.VMEM((ATT_HEADS, tq, LANES), f32),
                        pltpu.VMEM((ring,) + page, f32),
                        pltpu.VMEM((ring,) + page, f32),
                        pltpu.VMEM((2, n_pages, ATT_HEADS, LANES), f32),
                        pltpu.VMEM((2, ATT_HEADS, HEAD_DIM, LANES), f32),
                        pltpu.VMEM((2, n_pages, ATT_HEADS, LANES), f32),
                        pltpu.VMEM((2, n_pages, ATT_HEADS, LANES), f32),
                        pltpu.VMEM((n_pages, ATT_HEADS, LANES), f32),
                        pltpu.VMEM((n_pages, ATT_HEADS, LANES), f32),
                        pltpu.VMEM((ATT_HEADS, HEAD_DIM, LANES), f32),
                        pltpu.VMEM((2, ATT_HEADS, LANES), f32),
                        pltpu.SemaphoreType.DMA((ring,)), pltpu.SemaphoreType.DMA((ring,)),
                        pltpu.SemaphoreType.DMA((2,)), pltpu.SemaphoreType.DMA((2,))],
    )
    return pl.pallas_call(
        functools.partial(_att_kernel, tq=tq, seq=l, n_tok=ns, n_pages=n_pages, ppu=ppu, ring=ring),
        grid_spec=grid_spec,
        out_shape=[jax.ShapeDtypeStruct((n, l, D_ATT), bf16),
                   jax.ShapeDtypeStruct((ns, ATT_HEADS, HEAD_DIM), f32)],
        compiler_params=_cparams(("arbitrary", "arbitrary")), name="att",
    )(page_table, qa, ktb, vtb, cum, z_att, snew_b, lfnew_b, vnew, tri, qb, kc, vc, lfc)


def _post_math(x, att, y, z_ssm, gate, w_glu, b_glu, w_out, g_final):
    g = jax.nn.gelu(y)
    glu = jax.nn.sigmoid(jnp.dot(g.astype(bf16), w_glu, preferred_element_type=f32) + b_glu)
    ssm = g * glu * _silu(z_ssm)
    mix = jnp.concatenate([att.astype(bf16), ssm.astype(bf16)], axis=-1)
    xo = x + gate * jnp.dot(mix, w_out, preferred_element_type=f32)
    return xo * lax.rsqrt(jnp.mean(xo * xo, axis=-1, keepdims=True) + EPS) * g_final


def _ssm_kernel(*refs, nb, tl, time_major, chunk, with_post):
    u_ref, h0r_ref, h0i_ref, ar_ref, ai_ref, bre_ref, bim_ref, cre_ref, cim_ref, d_ref = refs[:10]
    if with_post:
        x_ref, att_ref, zssm_ref, gate_ref, wglu_ref, bglu_ref, wout_ref, gf_ref = refs[10:18]
        refs = refs[18:]
    else:
        refs = refs[10:]
    y_ref, hr_ref, hi_ref, utn, bu_re, bu_im, hs_re, hs_im, st_re, st_im, yscr, perm = refs[:12]
    i = pl.program_id(0)
    nj = D_SSM // LANES
    last = pl.num_programs(0) - (2 if with_post else 1)

    @pl.when(i == 0)
    def _():
        st_re[...] = h0r_ref[...]
        st_im[...] = h0i_ref[...]

    if with_post:
        y_prev = refs[12]

        @pl.when(i == 0)
        def _():
            y_prev[...] = jnp.zeros_like(y_prev)

    if time_major:
        utn[...] = u_ref[...].reshape(tl * nb, D_SSM)
    else:
        for j in range(nj):
            cols = slice(j * LANES, (j + 1) * LANES)
            for b in range(nb):
                perm[j, b * tl:(b + 1) * tl, :] = u_ref[b, :, cols]
            for t in range(tl):
                utn[t * nb:(t + 1) * nb, cols] = perm[j, pl.ds(t, nb, stride=tl), :]

    for c in range(nj):
        cols = slice(c * chunk, (c + 1) * chunk)
        ub = utn[:, c * LANES:(c + 1) * LANES].astype(bf16)
        bu_re[:, cols] = jnp.dot(ub, bre_ref[c], preferred_element_type=f32)
        bu_im[:, cols] = jnp.dot(ub, bim_ref[c], preferred_element_type=f32)

    if with_post:
        rows, d = nb * tl, x_ref.shape[-1]
        gate = jnp.broadcast_to(gate_ref[...], (nb, tl, d)).reshape(rows, d)
        out = _post_math(x_ref[...].reshape(rows, d), att_ref[...].reshape(rows, D_ATT), y_prev[...],
                         zssm_ref[...].reshape(rows, D_SSM), gate, wglu_ref[...], bglu_ref[...],
                         wout_ref[...], gf_ref[...])
        y_ref[...] = out.reshape(nb, tl, d)

    for c in range(nj):
        cols = slice(c * chunk, (c + 1) * chunk)
        ucols = slice(c * LANES, (c + 1) * LANES)
        ar = jnp.broadcast_to(ar_ref[:, cols], (nb, chunk))
        ai = jnp.broadcast_to(ai_ref[:, cols], (nb, chunk))
        hr, hi = st_re[:, cols], st_im[:, cols]
        for t in range(tl):
            rows = slice(t * nb, (t + 1) * nb)
            hr, hi = (ar * hr - ai * hi + bu_re[rows, cols], ar * hi + ai * hr + bu_im[rows, cols])
            hs_re[rows, cols] = hr.astype(bf16)
            hs_im[rows, cols] = hi.astype(bf16)
        st_re[:, cols] = hr
        st_im[:, cols] = hi
        yscr[:, ucols] = (jnp.dot(hs_re[:, cols], cre_ref[c], preferred_element_type=f32)
                          - jnp.dot(hs_im[:, cols], cim_ref[c], preferred_element_type=f32)
                          + d_ref[:, ucols] * utn[:, ucols])
    if time_major:
        y_ref[...] = yscr[...].reshape(tl, nb, D_SSM)
    else:
        for j in range(nj):
            cols = slice(j * LANES, (j + 1) * LANES)
            for t in range(tl):
                perm[j, pl.ds(t, nb, stride=tl), :] = yscr[t * nb:(t + 1) * nb, cols]
            if with_post:
                y_prev[:, cols] = perm[j]
            else:
                for b in range(nb):
                    y_ref[b, :, cols] = perm[j, b * tl:(b + 1) * tl, :]

    @pl.when(i == last)
    def _():
        hr_ref[...] = st_re[...]
        hi_ref[...] = st_im[...]


def _ssm_call(u, h0_re, h0_im, ab_re, ab_im, b_re, b_im, c_re, c_im, d_skip, tl, time_major, post=None):
    nj = D_SSM // LANES
    if time_major:
        l, nb = u.shape[0], u.shape[1]
        block = lambda w: pl.BlockSpec((tl, nb, w), lambda i: (i, 0, 0))
    else:
        nb, l = u.shape[0], u.shape[1]
        block = lambda w: pl.BlockSpec((nb, tl, w), lambda i: (0, i, 0))
    rows = tl * nb
    steps = l // tl
    full = lambda a: pl.BlockSpec(a.shape, lambda i, nd=a.ndim: (0,) * nd)
    ins = [u, h0_re, h0_im, ab_re, ab_im, b_re, b_im, c_re, c_im, d_skip]
    in_specs = [block(D_SSM)] + [full(a) for a in ins[1:]]
    out_spec, out_width, extra_scratch = block(D_SSM), D_SSM, []
    if post is not None:
        assert not time_major
        x, att, z_ssm = post[:3]
        out_width = x.shape[-1]
        behind = lambda w: pl.BlockSpec((nb, tl, w), lambda i: (0, jnp.maximum(i - 1, 0), 0))
        in_specs[0] = pl.BlockSpec((nb, tl, D_SSM), lambda i, last_block=steps - 1: (0, jnp.minimum(i, last_block), 0))
        ins += list(post)
        in_specs += [behind(out_width), behind(D_ATT), behind(D_SSM)] + [full(a) for a in post[3:]]
        out_spec = behind(out_width)
        extra_scratch = [pltpu.VMEM((rows, D_SSM), f32)]
        steps += 1
    return pl.pallas_call(
        functools.partial(_ssm_kernel, nb=nb, tl=tl, time_major=time_major, chunk=N_STATE // nj,
                          with_post=post is not None),
        grid=(steps,),
        in_specs=in_specs,
        out_specs=[out_spec, full(h0_re), full(h0_im)],
        out_shape=[jax.ShapeDtypeStruct(u.shape[:2] + (out_width,), f32),
                   jax.ShapeDtypeStruct(h0_re.shape, f32), jax.ShapeDtypeStruct(h0_im.shape, f32)],
        scratch_shapes=[pltpu.VMEM((rows, D_SSM), f32),
                        pltpu.VMEM((rows, N_STATE), f32), pltpu.VMEM((rows, N_STATE), f32),
                        pltpu.VMEM((rows, N_STATE), bf16), pltpu.VMEM((rows, N_STATE), bf16),
                        pltpu.VMEM((nb, N_STATE), f32), pltpu.VMEM((nb, N_STATE), f32),
                        pltpu.VMEM((rows, D_SSM), f32),
                        pltpu.VMEM((nj, rows, LANES), f32)] + extra_scratch,
        compiler_params=_cparams(("arbitrary",)), name="ssm",
    )(*ins)


def _post_kernel(*refs, gate_att):
    if gate_att:
        x_ref, att_ref, zatt_ref, y_ref, zssm_ref, gate_ref, wglu_ref, bglu_ref, wout_ref, gf_ref, o_ref = refs
        att = att_ref[...] * _silu(zatt_ref[...])
    else:
        x_ref, att_ref, y_ref, zssm_ref, gate_ref, wglu_ref, bglu_ref, wout_ref, gf_ref, o_ref = refs
        att = att_ref[...]
    o_ref[...] = _post_math(x_ref[...], att, y_ref[...], zssm_ref[...], gate_ref[...], wglu_ref[...], bglu_ref[...],
                            wout_ref[...], gf_ref[...])


def _post_call(x, att, z_att, y, z_ssm, gate, w_glu, b_glu, w_out, g_final, tm):
    n, l, d = x.shape
    row = lambda n_, i: (n_, i, 0)
    const2 = lambda n_, i: (0, 0)
    gate_spec = (pl.BlockSpec((None, 1, d), lambda n_, i: (n_, 0, 0)) if gate.shape[1] == 1
                 else pl.BlockSpec((None, tm, d), row))
    ins = [x, att] + ([z_att] if z_att is not None else []) + [y, z_ssm, gate, w_glu, b_glu, w_out, g_final]
    in_specs = ([pl.BlockSpec((None, tm, d), row), pl.BlockSpec((None, tm, D_ATT), row)]
                + ([pl.BlockSpec((None, tm, D_ATT), row)] if z_att is not None else [])
                + [pl.BlockSpec((None, tm, D_SSM), row),
                   pl.BlockSpec((None, tm, D_SSM), row), gate_spec,
                   pl.BlockSpec(w_glu.shape, const2), pl.BlockSpec((1, D_SSM), const2),
                   pl.BlockSpec(w_out.shape, const2), pl.BlockSpec((1, d), const2)])
    return pl.pallas_call(
        functools.partial(_post_kernel, gate_att=z_att is not None),
        grid=(n, l // tm), in_specs=in_specs,
        out_specs=pl.BlockSpec((None, tm, d), row),
        out_shape=jax.ShapeDtypeStruct((n, l, d), f32),
        compiler_params=_cparams(("arbitrary", "arbitrary")), name="post",
    )(*ins)


def kernel(x_prompt, x_sample, c_prompt, c_sample, cache_k, cache_v, cache_logf, state_ssm_re, state_ssm_im,
           page_table, g_norm, w_ada, b_ada, w_in, b_fgate, a_re, a_im, log_dt, b_re, b_im, c_re, c_im, d_skip,
           w_glu, b_glu, w_out, g_final):
    n, l, d = x_prompt.shape
    ns = x_sample.shape[0]
    depth = w_in.shape[0]
    assert depth == 1 and x_sample.shape[1] == 1
    lyr = 0
    xs = x_sample.reshape(ns, d)

    splits = [0, D_ATT, 2 * D_ATT, 3 * D_ATT, 4 * D_ATT, 4 * D_ATT + ATT_HEADS,
              4 * D_ATT + ATT_HEADS + D_SSM, 4 * D_ATT + ATT_HEADS + 2 * D_SSM]
    w_t = jnp.swapaxes(w_in[lyr], 0, 1)
    wq, wk, wv, wz, wf, wu, wzs = [w_t[splits[i]:splits[i + 1]] for i in range(7)]
    w_nn = jnp.concatenate([wq, wz, wu, wzs], axis=0).T.astype(bf16)
    w_nt = jnp.concatenate([wk, wv, wf, jnp.zeros((NT_ROWS - 2 * D_ATT - ATT_HEADS, d), f32)], axis=0).astype(bf16)
    b_f = b_fgate[lyr]
    b_f_pad = jnp.concatenate([b_f, jnp.zeros((NT_ROWS - 2 * D_ATT - ATT_HEADS,), f32)])[None, :]
    seg = (jnp.arange(D_ATT)[:, None] // HEAD_DIM == jnp.arange(LANES)[None, :]).astype(f32)
    gn = g_norm[lyr][None, :]
    gf = g_final[None, :]
    w_glu_b = w_glu[lyr].astype(bf16)
    w_out_b = w_out[lyr].astype(bf16)
    b_glu2 = b_glu[lyr][None, :]

    mod = _ada_call(jnp.concatenate([c_prompt, c_sample], axis=0), w_ada[lyr], b_ada[lyr][None, :])
    shift_p, scale_p, gate_p = [mod[:n, k * d:(k + 1) * d].reshape(n, 1, d) for k in range(3)]
    shift_s, scale_s, gate_s = [mod[n:, k * d:(k + 1) * d] for k in range(3)]

    ab_re, ab_im, bb_re, bb_im = _ssm_param_call(a_re[lyr], a_im[lyr], log_dt[lyr], b_re[lyr], b_im[lyr])
    ab_re_row = ab_re.reshape(1, N_STATE)
    ab_im_row = ab_im.reshape(1, N_STATE)
    slabs = D_SSM // LANES
    slab_diag = lambda w: _block_diag(w.reshape((slabs, SSM_GROUPS // slabs) + w.shape[1:]))
    bd_re = slab_diag(bb_re).astype(bf16)
    bd_im = slab_diag(bb_im).astype(bf16)
    cd_re = slab_diag(jnp.swapaxes(c_re[lyr], 1, 2)).astype(bf16)
    cd_im = slab_diag(jnp.swapaxes(c_im[lyr], 1, 2)).astype(bf16)
    d_row = d_skip[lyr].reshape(1, D_SSM)

    tm = min(512, l)
    qa, z_att, u_p, z_ssm, kt, vt, lft, ktb, vtb, cum = _pre_prompt_call(x_prompt, shift_p, scale_p, gn, w_nn, w_nt,
                                                                         b_f, tm)
    qs, k_s, v_s, zatt_s, u_s, zssm_s, lf_s, snew = _pre_sample_call(xs, shift_s, scale_s, gn, w_nn, w_nt,
                                                                     b_f_pad, seg)

    qb = jnp.broadcast_to(qs.reshape(ns, ATT_HEADS, HEAD_DIM, 1), (ns, ATT_HEADS, HEAD_DIM, LANES))
    snew_b = jnp.broadcast_to(snew[:, :ATT_HEADS, None], (ns, ATT_HEADS, LANES))
    lfnew_b = jnp.broadcast_to(lf_s[:, :ATT_HEADS, None], (ns, ATT_HEADS, LANES))
    kc = jnp.transpose(cache_k[lyr], (0, 2, 3, 1))
    vc = jnp.transpose(cache_v[lyr], (0, 2, 3, 1))
    lfc = jnp.transpose(cache_logf[lyr], (0, 2, 1))
    n_pages = page_table.shape[1]
    att, att_s = _att_call(qa, ktb, vtb, cum, z_att, page_table, qb, snew_b, lfnew_b,
                           v_s.reshape(ns, ATT_HEADS, HEAD_DIM), kc, vc, lfc,
                           tq=min(256, l), ppu=min(8, n_pages // 2))

    zeros_state = jnp.zeros((n, N_STATE), f32)
    y_prompt, hp_re, hp_im = _ssm_call(u_p, zeros_state, zeros_state, ab_re_row, ab_im_row, bd_re, bd_im, cd_re, cd_im,
                                       d_row, tl=min(32, l), time_major=False,
                                       post=(x_prompt, att, z_ssm, gate_p, w_glu_b, b_glu2, w_out_b, gf))

    heads_t = lambda t: jnp.transpose(t.reshape(n, ATT_HEADS, HEAD_DIM, l), (0, 3, 1, 2))[None]
    k_prompt = heads_t(kt)
    v_prompt = heads_t(vt)
    logf_prompt = jnp.transpose(lft, (0, 2, 1))[None]

    h0r = state_ssm_re[lyr].reshape(ns, N_STATE)
    h0i = state_ssm_im[lyr].reshape(ns, N_STATE)
    y_s, hs_re, hs_im = _ssm_call(u_s[None], h0r, h0i, ab_re_row, ab_im_row, bd_re, bd_im, cd_re, cd_im, d_row,
                                  tl=1, time_major=True)
    y_sample = _post_call(xs[None], att_s.reshape(1, ns, D_ATT), zatt_s[None], y_s, zssm_s[None], gate_s[None],
                          w_glu_b, b_glu2, w_out_b, gf, ns)

    st = lambda a, b: a.reshape(1, b, SSM_GROUPS, SSM_STATE)
    return (y_prompt, y_sample.reshape(ns, 1, d),
            k_prompt, v_prompt, logf_prompt, st(hp_re, n), st(hp_im, n),
            k_s.reshape(1, ns, 1, ATT_HEADS, HEAD_DIM), v_s.reshape(1, ns, 1, ATT_HEADS, HEAD_DIM),
            lf_s[:, :ATT_HEADS].reshape(1, ns, 1, ATT_HEADS), st(hs_re, ns), st(hs_im, ns))
```

```python
import functools
import math

import jax
import jax.numpy as jnp
from jax import lax
from jax.experimental import pallas as pl
from jax.experimental.pallas import tpu as pltpu

f32 = jnp.float32
bf16 = jnp.bfloat16

HEAD_DIM = 64
ATT_HEADS = 8
D_ATT = ATT_HEADS * HEAD_DIM
SSM_GROUP = 16
SSM_GROUPS = 32
SSM_STATE = 64
D_SSM = SSM_GROUP * SSM_GROUPS
N_STATE = SSM_GROUPS * SSM_STATE
EPS = 1e-6
QK_SCALE = HEAD_DIM ** -0.5
LOG2E = 1.0 / math.log(2.0)
LANES = 128
NT_ROWS = 2 * D_ATT + 16
VMEM_LIMIT = 56 * 1024 * 1024

_NT = (((1,), (1,)), ((), ()))
_HI = lax.Precision.HIGHEST


def _silu(x):
    return x * jax.nn.sigmoid(x)


def _log_sigmoid(x):
    return jnp.minimum(x, 0.0) - jnp.log1p(jnp.exp(-jnp.abs(x)))


def _modulated_norm(x, g, scale, shift):
    y = x * lax.rsqrt(jnp.mean(x * x, axis=-1, keepdims=True) + EPS)
    return (y * g) * (1.0 + scale) + shift


def _cparams(sem, vmem=VMEM_LIMIT):
    return pltpu.CompilerParams(dimension_semantics=sem, vmem_limit_bytes=vmem)


def _ada_kernel(c_ref, w_ref, b_ref, o_ref):
    s = _silu(c_ref[...]).astype(bf16)
    o_ref[...] = jnp.dot(s, w_ref[...].astype(bf16), preferred_element_type=f32) + b_ref[...]


def _ada_call(c_all, w_ada, b_ada):
    r, d = c_all.shape
    n_out = w_ada.shape[1]
    tn = 768
    return pl.pallas_call(
        _ada_kernel,
        grid=(n_out // tn,),
        in_specs=[pl.BlockSpec((r, d), lambda j: (0, 0)),
                  pl.BlockSpec((d, tn), lambda j: (0, j)),
                  pl.BlockSpec((1, tn), lambda j: (0, j))],
        out_specs=pl.BlockSpec((r, tn), lambda j: (0, j)),
        out_shape=jax.ShapeDtypeStruct((r, n_out), f32),
        compiler_params=_cparams(("arbitrary",)),
        name="ada",
    )(c_all, w_ada, b_ada)


def _ssm_param_kernel(are_ref, aim_ref, ldt_ref, bre_ref, bim_ref,
                      abr_ref, abi_ref, bbr_ref, bbi_ref):
    lr, li = are_ref[...], aim_ref[...]
    dt = jnp.exp(ldt_ref[...])
    mag = jnp.exp(lr * dt)
    abr = mag * jnp.cos(li * dt)
    abi = mag * jnp.sin(li * dt)
    xr, xi = abr - 1.0, abi
    den = lr * lr + li * li
    cr = (xr * lr + xi * li) / den
    ci = (xi * lr - xr * li) / den
    br, bi = bre_ref[...], bim_ref[...]
    abr_ref[...] = abr
    abi_ref[...] = abi
    bbr_ref[...] = cr * br - ci * bi
    bbi_ref[...] = cr * bi + ci * br


def _ssm_param_call(a_re, a_im, log_dt, b_re, b_im):
    g, p = a_re.shape
    hc = b_re.shape[-1]
    rep = lambda a: jnp.broadcast_to(a[:, None, :], (g, hc, p)).reshape(g * hc, p)
    ldt = jnp.broadcast_to(log_dt[:, None, None], (g, hc, p)).reshape(g * hc, p)
    bt = lambda b: jnp.swapaxes(b, 1, 2).reshape(g * hc, p)
    shp = jax.ShapeDtypeStruct((g * hc, p), f32)
    abr, abi, bbr, bbi = pl.pallas_call(
        _ssm_param_kernel, out_shape=[shp] * 4, name="ssm_par",
    )(rep(a_re), rep(a_im), ldt, bt(b_re), bt(b_im))
    first = lambda a: a.reshape(g, hc, p)[:, 0, :]
    return first(abr), first(abi), bbr.reshape(g, hc, p), bbi.reshape(g, hc, p)


def _block_diag(blocks):
    *lead, g, r, c = blocks.shape
    eye = jnp.eye(g, dtype=blocks.dtype)
    return (blocks[..., :, :, None, :] * eye[:, None, :, None]).reshape(*lead, g * r, g * c)


def _split3(x):
    hi = x.astype(bf16).astype(f32)
    r = x - hi
    mid = r.astype(bf16).astype(f32)
    return hi, mid, r - mid


AUG = HEAD_DIM


def _pre_prompt_kernel(x_ref, shift_ref, scale_ref, g_ref, wnn_ref, wnt_ref, bf_ref,
                       qa_ref, zatt_ref, u_ref, zssm_ref, kt_ref, vt_ref, lft_ref, ktb_ref, vtb_ref, cum_ref,
                       carry, parts):
    i = pl.program_id(1)
    tm = x_ref.shape[0]

    @pl.when(i == 0)
    def _():
        carry[...] = jnp.zeros_like(carry)

    @pl.when((pl.program_id(0) == 0) & (i == 0))
    def _():
        parts[...] = jnp.zeros_like(parts)

    h = _modulated_norm(x_ref[...], g_ref[...], scale_ref[...], shift_ref[...])
    hb = h.astype(bf16)
    pt = lax.dot_general(wnt_ref[...], hb, _NT, preferred_element_type=f32)
    kt_ref[...] = pt[:D_ATT]
    vt_ref[...] = pt[D_ATT:2 * D_ATT]
    ktb_ref[...] = pt[:D_ATT].astype(bf16)
    vtb_ref[...] = pt[D_ATT:2 * D_ATT].astype(bf16)
    lf = _log_sigmoid(pt[2 * D_ATT:2 * D_ATT + ATT_HEADS] + bf_ref[...])
    lft_ref[...] = lf
    p = jnp.dot(hb, wnn_ref[...], preferred_element_type=f32)
    zatt_ref[...] = p[:, D_ATT:2 * D_ATT]
    u_ref[...] = p[:, 2 * D_ATT:2 * D_ATT + D_SSM]
    zssm_ref[...] = p[:, 2 * D_ATT + D_SSM:]

    c = lf
    lane = lax.broadcasted_iota(jnp.int32, c.shape, 1)
    d = 1
    while d < tm:
        c = c + jnp.where(lane >= d, pltpu.roll(c, d, axis=1), 0.0)
        d *= 2
    c = c + jnp.concatenate([carry[...]] * (tm // LANES), axis=1)
    carry[...] = jnp.broadcast_to(c[:, tm - 1:tm], carry.shape)
    c = c * LOG2E
    cum_ref[...] = c
    for j, part in enumerate(_split3(c)):
        for hd in range(ATT_HEADS):
            parts[3 * hd + j:3 * hd + j + 1, :] = part[hd:hd + 1, :]
    cq_parts = parts[...].T
    lane_q = lax.broadcasted_iota(jnp.int32, (tm, LANES), 1)
    ones_lanes = jnp.where(lane_q < AUG + 3, 1.0, 0.0)
    for pr in range(ATT_HEADS // 2):
        qp = p[:, pr * LANES:(pr + 1) * LANES] * (QK_SCALE * LOG2E)
        for hh in range(2):
            hd = 2 * pr + hh
            qh = qp if hh == 0 else pltpu.roll(qp, HEAD_DIM, axis=1)
            cq = pltpu.roll(cq_parts, AUG + 3 - 3 * hd, axis=1)
            ext = jnp.where((lane_q >= AUG + 3) & (lane_q < AUG + 6), cq, ones_lanes)
            qa_ref[hd] = jnp.where(lane_q < HEAD_DIM, qh, ext).astype(bf16)


def _pre_prompt_call(x, shift, scale, g_norm, w_nn, w_nt, b_f, tm):
    n, l, d = x.shape
    row = lambda n_, i: (n_, i, 0)
    col = lambda n_, i: (n_, 0, i)
    const2 = lambda n_, i: (0, 0)
    out_shape = [
        jax.ShapeDtypeStruct((n, ATT_HEADS, l, LANES), bf16),
        jax.ShapeDtypeStruct((n, l, D_ATT), f32),
        jax.ShapeDtypeStruct((n, l, D_SSM), f32),
        jax.ShapeDtypeStruct((n, l, D_SSM), f32),
        jax.ShapeDtypeStruct((n, D_ATT, l), f32),
        jax.ShapeDtypeStruct((n, D_ATT, l), f32),
        jax.ShapeDtypeStruct((n, ATT_HEADS, l), f32),
        jax.ShapeDtypeStruct((n, D_ATT, l), bf16),
        jax.ShapeDtypeStruct((n, D_ATT, l), bf16),
        jax.ShapeDtypeStruct((n, ATT_HEADS, l), f32),
    ]
    out_specs = [
        pl.BlockSpec((None, ATT_HEADS, tm, LANES), lambda n_, i: (n_, 0, i, 0)),
        pl.BlockSpec((None, tm, D_ATT), row),
        pl.BlockSpec((None, tm, D_SSM), row),
        pl.BlockSpec((None, tm, D_SSM), row),
        pl.BlockSpec((None, D_ATT, tm), col),
        pl.BlockSpec((None, D_ATT, tm), col),
        pl.BlockSpec((None, ATT_HEADS, tm), col),
        pl.BlockSpec((None, D_ATT, tm), col),
        pl.BlockSpec((None, D_ATT, tm), col),
        pl.BlockSpec((None, ATT_HEADS, tm), col),
    ]
    in_specs = [
        pl.BlockSpec((None, tm, d), row),
        pl.BlockSpec((None, 1, d), lambda n_, i: (n_, 0, 0)),
        pl.BlockSpec((None, 1, d), lambda n_, i: (n_, 0, 0)),
        pl.BlockSpec((1, d), const2),
        pl.BlockSpec(w_nn.shape, const2),
        pl.BlockSpec(w_nt.shape, const2),
        pl.BlockSpec((ATT_HEADS, tm), const2),
    ]
    return pl.pallas_call(
        _pre_prompt_kernel, grid=(n, l // tm), in_specs=in_specs, out_specs=out_specs,
        out_shape=out_shape,
        scratch_shapes=[pltpu.VMEM((ATT_HEADS, LANES), f32), pltpu.VMEM((LANES, tm), f32)],
        compiler_params=_cparams(("arbitrary", "arbitrary")), name="pre",
    )(x, shift, scale, g_norm, w_nn, w_nt, jnp.broadcast_to(b_f[:, None], (ATT_HEADS, tm)))


def _pre_sample_kernel(x_ref, shift_ref, scale_ref, g_ref, wnn_ref, wnt_ref, bf_ref, seg_ref,
                       q_ref, k_ref, v_ref, zatt_ref, u_ref, zssm_ref, lf_ref, snew_ref):
    hb = _modulated_norm(x_ref[...], g_ref[...], scale_ref[...], shift_ref[...]).astype(bf16)
    p = jnp.dot(hb, wnn_ref[...], preferred_element_type=f32)
    pk = lax.dot_general(hb, wnt_ref[...], _NT, preferred_element_type=f32)
    q = p[:, :D_ATT] * QK_SCALE
    k = pk[:, :D_ATT]
    q_ref[...] = q
    k_ref[...] = k
    v_ref[...] = pk[:, D_ATT:2 * D_ATT]
    zatt_ref[...] = p[:, D_ATT:2 * D_ATT]
    u_ref[...] = p[:, 2 * D_ATT:2 * D_ATT + D_SSM]
    zssm_ref[...] = p[:, 2 * D_ATT + D_SSM:]
    lf_ref[...] = _log_sigmoid(pk[:, 2 * D_ATT:] + bf_ref[...])
    snew_ref[...] = jnp.dot(q * k, seg_ref[...], preferred_element_type=f32, precision=_HI)


def _pre_sample_call(x, shift, scale, g_norm, w_nn, w_nt, b_f_pad, seg):
    r, d = x.shape
    shp = lambda c: jax.ShapeDtypeStruct((r, c), f32)
    return pl.pallas_call(
        _pre_sample_kernel,
        out_shape=[shp(D_ATT), shp(D_ATT), shp(D_ATT), shp(D_ATT), shp(D_SSM), shp(D_SSM),
                   shp(NT_ROWS - 2 * D_ATT), shp(LANES)],
        compiler_params=pltpu.CompilerParams(vmem_limit_bytes=VMEM_LIMIT), name="pre_s",
    )(x, shift, scale, g_norm, w_nn, w_nt, b_f_pad, seg)


def _att_kernel(pt_ref, qa_ref, kt_ref, vt_ref, cum_ref, z_ref,
                snew_ref, lfnew_ref, vnew_ref, tri_ref, qb_hbm, kc_hbm, vc_hbm, lfc_hbm,
                o_ref, od_ref,
                ka, va, m_s, acc_s,
                kbuf, vbuf, lfbuf, qbuf, sc, bias, tot, p_s, dacc, stat, ksem, vsem, lfsem, qsem,
                *, tq, seq, n_tok, n_pages, ppu, ring):
    n = pl.program_id(0)
    qi = pl.program_id(1)
    n_tiles = seq // tq
    units = n_pages // ppu
    ahead = ring - 1
    k_end = n_tok * units
    v_lo, v_hi = units, (n_tok + 1) * units

    def start_pages(src, buf, sem, tok, sub, slot):
        for j in range(ppu):
            pltpu.make_async_copy(src.at[pt_ref[tok, sub * ppu + j]], buf.at[slot, j], sem.at[slot]).start()

    def wait_pages(src, buf, sem, slot):
        for j in range(ppu):
            pltpu.make_async_copy(src.at[0], buf.at[slot, j], sem.at[slot]).wait()

    def request(w):
        tok, sub, slot = w // units, w % units, w % ring

        @pl.when(w < k_end)
        def _():
            start_pages(kc_hbm, kbuf, ksem, tok, sub, slot)

        @pl.when((w >= v_lo) & (w < v_hi))
        def _():
            start_pages(vc_hbm, vbuf, vsem, tok - 1, sub, slot)

    def start_token_inputs(t):
        slot = t % 2

        def body(pg, carry):
            pltpu.make_async_copy(lfc_hbm.at[pt_ref[t, pg]], lfbuf.at[slot, pg], lfsem.at[slot]).start()
            return carry
        lax.fori_loop(0, n_pages, body, 0)
        pltpu.make_async_copy(qb_hbm.at[t], qbuf.at[slot], qsem.at[slot]).start()

    def new_token(t):
        slot = t % 2

        @pl.when(t + 1 < n_tok)
        def _():
            start_token_inputs(t + 1)

        def wait_lf(pg, carry):
            pltpu.make_async_copy(lfc_hbm.at[0], lfbuf.at[slot, pg], lfsem.at[slot]).wait()
            return carry
        lax.fori_loop(0, n_pages, wait_lf, 0)
        pltpu.make_async_copy(qb_hbm.at[0], qbuf.at[slot], qsem.at[slot]).wait()
        lf2 = lfbuf[slot].reshape(n_pages * ATT_HEADS, LANES)
        sums = sum(jnp.dot(part.astype(bf16), tri_ref[...], preferred_element_type=f32) for part in _split3(lf2))
        bias[slot] = sums[:, :LANES].reshape(n_pages, ATT_HEADS, LANES)
        tot[...] = sums[:, LANES:].reshape(n_pages, ATT_HEADS, LANES)
        lfnew = lfnew_ref[t]

        def suffix(i, run):
            pg = n_pages - 1 - i
            bias[slot, pg] = bias[slot, pg] + run + lfnew
            return run + tot[pg]
        lax.fori_loop(0, n_pages, suffix, jnp.zeros((ATT_HEADS, LANES), f32))

    def token_softmax(t):
        s_all = sc[t % 2] + bias[t % 2]
        snew = snew_ref[t]
        m = jnp.max(jnp.max(s_all, axis=0), axis=-1, keepdims=True)
        m = jnp.maximum(m, snew)
        p_all = jnp.exp(s_all - m[None])
        p_s[...] = p_all
        p_new = jnp.exp(snew - m)
        stat[0] = p_new
        stat[1] = jnp.sum(jnp.sum(p_all, axis=0), axis=-1, keepdims=True) + p_new
        dacc[...] = jnp.zeros_like(dacc)

    def token_output(t):
        ones_row = jnp.ones((ATT_HEADS, LANES), f32)
        head_row = lax.broadcasted_iota(jnp.int32, (ATT_HEADS, HEAD_DIM), 0)
        past = jnp.zeros((ATT_HEADS, HEAD_DIM), f32)
        for h in range(ATT_HEADS):
            r = lax.dot_general(ones_row, dacc[h], _NT, preferred_element_type=f32, precision=_HI)
            past = jnp.where(head_row == h, r, past)
        od_ref[t] = (past + stat[0][:, :HEAD_DIM] * vnew_ref[t]) * (1.0 / stat[1][:, :HEAD_DIM])

    def unit_prologue(w):
        t = w // units

        @pl.when(w % units == 0)
        def _():
            @pl.when((t >= 2) & (t <= n_tok + 1))
            def _():
                token_output(t - 2)

            @pl.when((t >= 1) & (t <= n_tok))
            def _():
                token_softmax(t - 1)

            @pl.when(t < n_tok)
            def _():
                new_token(t)

        request(w + ahead)

        @pl.when(w < k_end)
        def _():
            wait_pages(kc_hbm, kbuf, ksem, w % ring)

        @pl.when((w >= v_lo) & (w < v_hi))
        def _():
            wait_pages(vc_hbm, vbuf, vsem, w % ring)

    def unit_compute(w, h, ready):
        tok, sub, slot = w // units, w % units, w % ring
        qh, a = qbuf[jnp.minimum(tok, n_tok - 1) % 2, h], dacc[h]
        if ready is not None:
            qh, a = jnp.where(ready, qh, 0.0), jnp.where(ready, a, 0.0)
        for j in range(ppu):
            sc[tok % 2, sub * ppu + j, pl.ds(h, 1), :] = jnp.sum(kbuf[slot, j, h] * qh, axis=0, keepdims=True)
            pr = p_s[sub * ppu + j, pl.ds(h, 1), :]
            a = a + vbuf[slot, j, h] * jnp.broadcast_to(pr, (HEAD_DIM, LANES))
        dacc[h] = a

    @pl.when((n == 0) & (qi == 0))
    def _():
        vbuf[...] = jnp.zeros_like(vbuf)
        p_s[...] = jnp.zeros_like(p_s)
        dacc[...] = jnp.zeros_like(dacc)
        start_token_inputs(jnp.int32(0))
        for w0 in range(ahead):
            request(jnp.int32(w0))

    @pl.when(qi == 0)
    def _():
        c = cum_ref[...]
        r16 = lax.broadcasted_iota(jnp.int32, (16, seq), 0)
        for h in range(ATT_HEADS):
            ka[h, 0:HEAD_DIM, :] = kt_ref[h * HEAD_DIM:(h + 1) * HEAD_DIM, :]
            va[h, 0:HEAD_DIM, :] = vt_ref[h * HEAD_DIM:(h + 1) * HEAD_DIM, :]
            hi, mid, lo = _split3(c[h:h + 1, :])
            ext = jnp.where(r16 == 0, -hi, jnp.where(r16 == 1, -mid, jnp.where(r16 == 2, -lo,
                            jnp.where(r16 < 6, 1.0, 0.0))))
            ka[h, AUG:AUG + 16, :] = ext.astype(bf16)
            ka[h, AUG + 16:, :] = jnp.zeros((LANES - AUG - 16, seq), bf16)
            va[h, HEAD_DIM:, :] = jnp.ones((LANES - HEAD_DIM, seq), bf16)

    lane_q = lax.broadcasted_iota(jnp.int32, (tq, LANES), 1)
    row = lax.broadcasted_iota(jnp.int32, (tq, tq), 0)
    col = lax.broadcasted_iota(jnp.int32, (tq, tq), 1)
    causal = col <= row

    first_unit = n * (n_tiles * (n_tiles + 1) // 2) + (qi * (qi + 1)) // 2

    def block(kb, diagonal):
        w = first_unit if diagonal else first_unit + 1 + kb
        unit_prologue(w)
        k0 = pl.multiple_of(kb * tq, tq)
        for h in range(ATT_HEADS):
            s = jnp.dot(qa_ref[h], ka[h, :, pl.ds(k0, tq)], preferred_element_type=f32)
            if diagonal:
                s = jnp.where(causal, s, -jnp.inf)
            rowmax = jnp.broadcast_to(jnp.max(s, axis=-1, keepdims=True), (tq, LANES))
            m_new = rowmax if diagonal else jnp.maximum(m_s[h], rowmax)
            p = [jnp.exp2(s[:, j * LANES:(j + 1) * LANES] - m_new) for j in range(tq // LANES)]
            p = jnp.concatenate(p, axis=1)
            pv = lax.dot_general(p.astype(bf16), va[h, :, pl.ds(k0, tq)], _NT, preferred_element_type=f32)
            acc_s[h] = pv if diagonal else jnp.exp2(m_s[h] - m_new) * acc_s[h] + pv
            m_s[h] = m_new
            unit_compute(w, h, m_new[0:HEAD_DIM, :] < jnp.inf)

    def off_diagonal(kb, carry):
        block(kb, False)
        return carry

    block(qi, True)
    lax.fori_loop(0, qi, off_diagonal, 0)

    for pr in range(ATT_HEADS // 2):
        a0, a1 = acc_s[2 * pr], acc_s[2 * pr + 1]
        o0 = a0 * pltpu.roll(1.0 / a0, HEAD_DIM, axis=1)
        o1 = pltpu.roll(a1, HEAD_DIM, axis=1) * (1.0 / a1)
        o = jnp.where(lane_q < HEAD_DIM, o0, o1)
        cols = slice(pr * LANES, (pr + 1) * LANES)
        o_ref[:, cols] = (o * _silu(z_ref[:, cols])).astype(bf16)


def _att_call(qa, ktb, vtb, cum, z_att, page_table, qb, snew_b, lfnew_b, vnew, kc, vc, lfc, tq, ppu, ring=4):
    n, _, l, _ = qa.shape
    ns, n_pages = page_table.shape
    n_tiles = l // tq
    units = n_pages // ppu
    assert n * (n_tiles * (n_tiles + 1) // 2) >= (ns + 1) * units + 1, "too few key-block steps for the page units"
    lane = jnp.arange(LANES)
    tri = jnp.concatenate([lane[:, None] > lane[None, :], jnp.ones((LANES, LANES), bool)], axis=1).astype(bf16)
    row = lambda n_, i, pt: (n_, i, 0)
    whole = lambda n_, i, pt: (n_, 0, 0)
    const = lambda nd: (lambda n_, i, pt: (0,) * nd)
    page = (ppu, ATT_HEADS, HEAD_DIM, LANES)
    grid_spec = pltpu.PrefetchScalarGridSpec(
        num_scalar_prefetch=1,
        grid=(n, n_tiles),
        in_specs=[pl.BlockSpec((None, ATT_HEADS, tq, LANES), lambda n_, i, pt: (n_, 0, i, 0)),
                  pl.BlockSpec((None, D_ATT, l), whole),
                  pl.BlockSpec((None, D_ATT, l), whole),
                  pl.BlockSpec((None, ATT_HEADS, l), whole),
                  pl.BlockSpec((None, tq, D_ATT), row),
                  pl.BlockSpec(snew_b.shape, const(3)),
                  pl.BlockSpec(lfnew_b.shape, const(3)),
                  pl.BlockSpec(vnew.shape, const(3)),
                  pl.BlockSpec((LANES, 2 * LANES), const(2)),
                  pl.BlockSpec(memory_space=pl.ANY),
                  pl.BlockSpec(memory_space=pl.ANY),
                  pl.BlockSpec(memory_space=pl.ANY),
                  pl.BlockSpec(memory_space=pl.ANY)],
        out_specs=[pl.BlockSpec((None, tq, D_ATT), row),
                   pl.BlockSpec((ns, ATT_HEADS, HEAD_DIM), const(3))],
        scratch_shapes=[pltpu.VMEM((ATT_HEADS, LANES, l), bf16), pltpu.VMEM((ATT_HEADS, LANES, l), bf16),
                        pltpu.VMEM((ATT_HEADS, tq, LANES), f32),
                        pltpu.VMEM((ATT_HEADS, tq, LANES), f32),
                        pltpu.VMEM((ring,) + page, f32),
                        pltpu.VMEM((ring,) + page, f32),
                        pltpu.VMEM((2, n_pages, ATT_HEADS, LANES), f32),
                        pltpu.VMEM((2, ATT_HEADS, HEAD_DIM, LANES), f32),
                        pltpu.VMEM((2, n_pages, ATT_HEADS, LANES), f32),
                        pltpu.VMEM((2, n_pages, ATT_HEADS, LANES), f32),
                        pltpu.VMEM((n_pages, ATT_HEADS, LANES), f32),
                        pltpu.VMEM((n_pages, ATT_HEADS, LANES), f32),
                        pltpu.VMEM((ATT_HEADS, HEAD_DIM, LANES), f32),
                        pltpu.VMEM((2, ATT_HEADS, LANES), f32),
                        pltpu.SemaphoreType.DMA((ring,)), pltpu.SemaphoreType.DMA((ring,)),
                        pltpu.SemaphoreType.DMA((2,)), pltpu.SemaphoreType.DMA((2,))],
    )
    return pl.pallas_call(
        functools.partial(_att_kernel, tq=tq, seq=l, n_tok=ns, n_pages=n_pages, ppu=ppu, ring=ring),
        grid_spec=grid_spec,
        out_shape=[jax.ShapeDtypeStruct((n, l, D_ATT), bf16),
                   jax.ShapeDtypeStruct((ns, ATT_HEADS, HEAD_DIM), f32)],
        compiler_params=_cparams(("arbitrary", "arbitrary")), name="att",
    )(page_table, qa, ktb, vtb, cum, z_att, snew_b, lfnew_b, vnew, tri, qb, kc, vc, lfc)


def _post_math(x, att, y, z_ssm, gate, w_glu, b_glu, w_out, g_final):
    g = jax.nn.gelu(y)
    glu = jax.nn.sigmoid(jnp.dot(g.astype(bf16), w_glu, preferred_element_type=f32) + b_glu)
    ssm = g * glu * _silu(z_ssm)
    mix = jnp.concatenate([att.astype(bf16), ssm.astype(bf16)], axis=-1)
    xo = x + gate * jnp.dot(mix, w_out, preferred_element_type=f32)
    return xo * lax.rsqrt(jnp.mean(xo * xo, axis=-1, keepdims=True) + EPS) * g_final


def _ssm_kernel(*refs, nb, tl, time_major, chunk, with_post):
    u_ref, h0r_ref, h0i_ref, ar_ref, ai_ref, bre_ref, bim_ref, cre_ref, cim_ref, d_ref = refs[:10]
    if with_post:
        x_ref, att_ref, zssm_ref, gate_ref, wglu_ref, bglu_ref, wout_ref, gf_ref = refs[10:18]
        refs = refs[18:]
    else:
        refs = refs[10:]
    y_ref, hr_ref, hi_ref, utn, bu_re, bu_im, hs_re, hs_im, st_re, st_im, yscr, perm = refs[:12]
    i = pl.program_id(0)
    nj = D_SSM // LANES
    last = pl.num_programs(0) - (2 if with_post else 1)

    @pl.when(i == 0)
    def _():
        st_re[...] = h0r_ref[...]
        st_im[...] = h0i_ref[...]

    if with_post:
        y_prev = refs[12]

        @pl.when(i == 0)
        def _():
            y_prev[...] = jnp.zeros_like(y_prev)

    if time_major:
        utn[...] = u_ref[...].reshape(tl * nb, D_SSM)
    else:
        for j in range(nj):
            cols = slice(j * LANES, (j + 1) * LANES)
            for b in range(nb):
                perm[j, b * tl:(b + 1) * tl, :] = u_ref[b, :, cols]
            for t in range(tl):
                utn[t * nb:(t + 1) * nb, cols] = perm[j, pl.ds(t, nb, stride=tl), :]

    for c in range(nj):
        cols = slice(c * chunk, (c + 1) * chunk)
        ub = utn[:, c * LANES:(c + 1) * LANES].astype(bf16)
        bu_re[:, cols] = jnp.dot(ub, bre_ref[c], preferred_element_type=f32)
        bu_im[:, cols] = jnp.dot(ub, bim_ref[c], preferred_element_type=f32)

    if with_post:
        rows, d = nb * tl, x_ref.shape[-1]
        gate = jnp.broadcast_to(gate_ref[...], (nb, tl, d)).reshape(rows, d)
        out = _post_math(x_ref[...].reshape(rows, d), att_ref[...].reshape(rows, D_ATT), y_prev[...],
                         zssm_ref[...].reshape(rows, D_SSM), gate, wglu_ref[...], bglu_ref[...],
                         wout_ref[...], gf_ref[...])
        y_ref[...] = out.reshape(nb, tl, d)

    for c in range(nj):
        cols = slice(c * chunk, (c + 1) * chunk)
        ucols = slice(c * LANES, (c + 1) * LANES)
        ar = jnp.broadcast_to(ar_ref[:, cols], (nb, chunk))
        ai = jnp.broadcast_to(ai_ref[:, cols], (nb, chunk))
        hr, hi = st_re[:, cols], st_im[:, cols]
        for t in range(tl):
            rows = slice(t * nb, (t + 1) * nb)
            hr, hi = (ar * hr - ai * hi + bu_re[rows, cols], ar * hi + ai * hr + bu_im[rows, cols])
            hs_re[rows, cols] = hr.astype(bf16)
            hs_im[rows, cols] = hi.astype(bf16)
        st_re[:, cols] = hr
        st_im[:, cols] = hi
        yscr[:, ucols] = (jnp.dot(hs_re[:, cols], cre_ref[c], preferred_element_type=f32)
                          - jnp.dot(hs_im[:, cols], cim_ref[c], preferred_element_type=f32)
                          + d_ref[:, ucols] * utn[:, ucols])
    if time_major:
        y_ref[...] = yscr[...].reshape(tl, nb, D_SSM)
    else:
        for j in range(nj):
            cols = slice(j * LANES, (j + 1) * LANES)
            for t in range(tl):
                perm[j, pl.ds(t, nb, stride=tl), :] = yscr[t * nb:(t + 1) * nb, cols]
            if with_post:
                y_prev[:, cols] = perm[j]
            else:
                for b in range(nb):
                    y_ref[b, :, cols] = perm[j, b * tl:(b + 1) * tl, :]

    @pl.when(i == last)
    def _():
        hr_ref[...] = st_re[...]
        hi_ref[...] = st_im[...]


def _ssm_call(u, h0_re, h0_im, ab_re, ab_im, b_re, b_im, c_re, c_im, d_skip, tl, time_major, post=None):
    nj = D_SSM // LANES
    if time_major:
        l, nb = u.shape[0], u.shape[1]
        block = lambda w: pl.BlockSpec((tl, nb, w), lambda i: (i, 0, 0))
    else:
        nb, l = u.shape[0], u.shape[1]
        block = lambda w: pl.BlockSpec((nb, tl, w), lambda i: (0, i, 0))
    rows = tl * nb
    steps = l // tl
    full = lambda a: pl.BlockSpec(a.shape, lambda i, nd=a.ndim: (0,) * nd)
    ins = [u, h0_re, h0_im, ab_re, ab_im, b_re, b_im, c_re, c_im, d_skip]
    in_specs = [block(D_SSM)] + [full(a) for a in ins[1:]]
    out_spec, out_width, extra_scratch = block(D_SSM), D_SSM, []
    if post is not None:
        assert not time_major
        x, att, z_ssm = post[:3]
        out_width = x.shape[-1]
        behind = lambda w: pl.BlockSpec((nb, tl, w), lambda i: (0, jnp.maximum(i - 1, 0), 0))
        in_specs[0] = pl.BlockSpec((nb, tl, D_SSM), lambda i, last_block=steps - 1: (0, jnp.minimum(i, last_block), 0))
        ins += list(post)
        in_specs += [behind(out_width), behind(D_ATT), behind(D_SSM)] + [full(a) for a in post[3:]]
        out_spec = behind(out_width)
        extra_scratch = [pltpu.VMEM((rows, D_SSM), f32)]
        steps += 1
    return pl.pallas_call(
        functools.partial(_ssm_kernel, nb=nb, tl=tl, time_major=time_major, chunk=N_STATE // nj,
                          with_post=post is not None),
        grid=(steps,),
        in_specs=in_specs,
        out_specs=[out_spec, full(h0_re), full(h0_im)],
        out_shape=[jax.ShapeDtypeStruct(u.shape[:2] + (out_width,), f32),
                   jax.ShapeDtypeStruct(h0_re.shape, f32), jax.ShapeDtypeStruct(h0_im.shape, f32)],
        scratch_shapes=[pltpu.VMEM((rows, D_SSM), f32),
                        pltpu.VMEM((rows, N_STATE), f32), pltpu.VMEM((rows, N_STATE), f32),
                        pltpu.VMEM((rows, N_STATE), bf16), pltpu.VMEM((rows, N_STATE), bf16),
                        pltpu.VMEM((nb, N_STATE), f32), pltpu.VMEM((nb, N_STATE), f32),
                        pltpu.VMEM((rows, D_SSM), f32),
                        pltpu.VMEM((nj, rows, LANES), f32)] + extra_scratch,
        compiler_params=_cparams(("arbitrary",)), name="ssm",
    )(*ins)


def _post_kernel(*refs, gate_att):
    if gate_att:
        x_ref, att_ref, zatt_ref, y_ref, zssm_ref, gate_ref, wglu_ref, bglu_ref, wout_ref, gf_ref, o_ref = refs
        att = att_ref[...] * _silu(zatt_ref[...])
    else:
        x_ref, att_ref, y_ref, zssm_ref, gate_ref, wglu_ref, bglu_ref, wout_ref, gf_ref, o_ref = refs
        att = att_ref[...]
    o_ref[...] = _post_math(x_ref[...], att, y_ref[...], zssm_ref[...], gate_ref[...], wglu_ref[...], bglu_ref[...],
                            wout_ref[...], gf_ref[...])


def _post_call(x, att, z_att, y, z_ssm, gate, w_glu, b_glu, w_out, g_final, tm):
    n, l, d = x.shape
    row = lambda n_, i: (n_, i, 0)
    const2 = lambda n_, i: (0, 0)
    gate_spec = (pl.BlockSpec((None, 1, d), lambda n_, i: (n_, 0, 0)) if gate.shape[1] == 1
                 else pl.BlockSpec((None, tm, d), row))
    ins = [x, att] + ([z_att] if z_att is not None else []) + [y, z_ssm, gate, w_glu, b_glu, w_out, g_final]
    in_specs = ([pl.BlockSpec((None, tm, d), row), pl.BlockSpec((None, tm, D_ATT), row)]
                + ([pl.BlockSpec((None, tm, D_ATT), row)] if z_att is not None else [])
                + [pl.BlockSpec((None, tm, D_SSM), row),
                   pl.BlockSpec((None, tm, D_SSM), row), gate_spec,
                   pl.BlockSpec(w_glu.shape, const2), pl.BlockSpec((1, D_SSM), const2),
                   pl.BlockSpec(w_out.shape, const2), pl.BlockSpec((1, d), const2)])
    return pl.pallas_call(
        functools.partial(_post_kernel, gate_att=z_att is not None),
        grid=(n, l // tm), in_specs=in_specs,
        out_specs=pl.BlockSpec((None, tm, d), row),
        out_shape=jax.ShapeDtypeStruct((n, l, d), f32),
        compiler_params=_cparams(("arbitrary", "arbitrary")), name="post",
    )(*ins)


def kernel(x_prompt, x_sample, c_prompt, c_sample, cache_k, cache_v, cache_logf, state_ssm_re, state_ssm_im,
           page_table, g_norm, w_ada, b_ada, w_in, b_fgate, a_re, a_im, log_dt, b_re, b_im, c_re, c_im, d_skip,
           w_glu, b_glu, w_out, g_final):
    n, l, d = x_prompt.shape
    ns = x_sample.shape[0]
    depth = w_in.shape[0]
    assert depth == 1 and x_sample.shape[1] == 1
    lyr = 0
    xs = x_sample.reshape(ns, d)

    splits = [0, D_ATT, 2 * D_ATT, 3 * D_ATT, 4 * D_ATT, 4 * D_ATT + ATT_HEADS,
              4 * D_ATT + ATT_HEADS + D_SSM, 4 * D_ATT + ATT_HEADS + 2 * D_SSM]
    w_t = jnp.swapaxes(w_in[lyr], 0, 1)
    wq, wk, wv, wz, wf, wu, wzs = [w_t[splits[i]:splits[i + 1]] for i in range(7)]
    w_nn = jnp.concatenate([wq, wz, wu, wzs], axis=0).T.astype(bf16)
    w_nt = jnp.concatenate([wk, wv, wf, jnp.zeros((NT_ROWS - 2 * D_ATT - ATT_HEADS, d), f32)], axis=0).astype(bf16)
    b_f = b_fgate[lyr]
    b_f_pad = jnp.concatenate([b_f, jnp.zeros((NT_ROWS - 2 * D_ATT - ATT_HEADS,), f32)])[None, :]
    seg = (jnp.arange(D_ATT)[:, None] // HEAD_DIM == jnp.arange(LANES)[None, :]).astype(f32)
    gn = g_norm[lyr][None, :]
    gf = g_final[None, :]
    w_glu_b = w_glu[lyr].astype(bf16)
    w_out_b = w_out[lyr].astype(bf16)
    b_glu2 = b_glu[lyr][None, :]

    mod = _ada_call(jnp.concatenate([c_prompt, c_sample], axis=0), w_ada[lyr], b_ada[lyr][None, :])
    shift_p, scale_p, gate_p = [mod[:n, k * d:(k + 1) * d].reshape(n, 1, d) for k in range(3)]
    shift_s, scale_s, gate_s = [mod[n:, k * d:(k + 1) * d] for k in range(3)]

    ab_re, ab_im, bb_re, bb_im = _ssm_param_call(a_re[lyr], a_im[lyr], log_dt[lyr], b_re[lyr], b_im[lyr])
    ab_re_row = ab_re.reshape(1, N_STATE)
    ab_im_row = ab_im.reshape(1, N_STATE)
    slabs = D_SSM // LANES
    slab_diag = lambda w: _block_diag(w.reshape((slabs, SSM_GROUPS // slabs) + w.shape[1:]))
    bd_re = slab_diag(bb_re).astype(bf16)
    bd_im = slab_diag(bb_im).astype(bf16)
    cd_re = slab_diag(jnp.swapaxes(c_re[lyr], 1, 2)).astype(bf16)
    cd_im = slab_diag(jnp.swapaxes(c_im[lyr], 1, 2)).astype(bf16)
    d_row = d_skip[lyr].reshape(1, D_SSM)

    tm = min(512, l)
    qa, z_att, u_p, z_ssm, kt, vt, lft, ktb, vtb, cum = _pre_prompt_call(x_prompt, shift_p, scale_p, gn, w_nn, w_nt,
                                                                         b_f, tm)
    qs, k_s, v_s, zatt_s, u_s, zssm_s, lf_s, snew = _pre_sample_call(xs, shift_s, scale_s, gn, w_nn, w_nt,
                                                                     b_f_pad, seg)

    qb = jnp.broadcast_to(qs.reshape(ns, ATT_HEADS, HEAD_DIM, 1), (ns, ATT_HEADS, HEAD_DIM, LANES))
    snew_b = jnp.broadcast_to(snew[:, :ATT_HEADS, None], (ns, ATT_HEADS, LANES))
    lfnew_b = jnp.broadcast_to(lf_s[:, :ATT_HEADS, None], (ns, ATT_HEADS, LANES))
    kc = jnp.transpose(cache_k[lyr], (0, 2, 3, 1))
    vc = jnp.transpose(cache_v[lyr], (0, 2, 3, 1))
    lfc = jnp.transpose(cache_logf[lyr], (0, 2, 1))
    n_pages = page_table.shape[1]
    att, att_s = _att_call(qa, ktb, vtb, cum, z_att, page_table, qb, snew_b, lfnew_b,
                           v_s.reshape(ns, ATT_HEADS, HEAD_DIM), kc, vc, lfc,
                           tq=min(256, l), ppu=min(8, n_pages // 2))

    zeros_state = jnp.zeros((n, N_STATE), f32)
    y_prompt, hp_re, hp_im = _ssm_call(u_p, zeros_state, zeros_state, ab_re_row, ab_im_row, bd_re, bd_im, cd_re, cd_im,
                                       d_row, tl=min(32, l), time_major=False,
                                       post=(x_prompt, att, z_ssm, gate_p, w_glu_b, b_glu2, w_out_b, gf))

    heads_t = lambda t: jnp.transpose(t.reshape(n, ATT_HEADS, HEAD_DIM, l), (0, 3, 1, 2))[None]
    k_prompt = heads_t(kt)
    v_prompt = heads_t(vt)
    logf_prompt = jnp.transpose(lft, (0, 2, 1))[None]

    h0r = state_ssm_re[lyr].reshape(ns, N_STATE)
    h0i = state_ssm_im[lyr].reshape(ns, N_STATE)
    y_s, hs_re, hs_im = _ssm_call(u_s[None], h0r, h0i, ab_re_row, ab_im_row, bd_re, bd_im, cd_re, cd_im, d_row,
                                  tl=1, time_major=True)
    y_sample = _post_call(xs[None], att_s.reshape(1, ns, D_ATT), zatt_s[None], y_s, zssm_s[None], gate_s[None],
                          w_glu_b, b_glu2, w_out_b, gf, ns)

    st = lambda a, b: a.reshape(1, b, SSM_GROUPS, SSM_STATE)
    return (y_prompt, y_sample.reshape(ns, 1, d),
            k_prompt, v_prompt, logf_prompt, st(hp_re, n), st(hp_im, n),
            k_s.reshape(1, ns, 1, ATT_HEADS, HEAD_DIM), v_s.reshape(1, ns, 1, ATT_HEADS, HEAD_DIM),
            lf_s[:, :ATT_HEADS].reshape(1, ns, 1, ATT_HEADS), st(hs_re, ns), st(hs_im, ns))
```

```python
import functools
import math

import jax
import jax.numpy as jnp
from jax import lax
from jax.experimental import pallas as pl
from jax.experimental.pallas import tpu as pltpu

f32 = jnp.float32
bf16 = jnp.bfloat16

HEAD_DIM = 64
ATT_HEADS = 8
D_ATT = ATT_HEADS * HEAD_DIM
SSM_GROUP = 16
SSM_GROUPS = 32
SSM_STATE = 64
D_SSM = SSM_GROUP * SSM_GROUPS
N_STATE = SSM_GROUPS * SSM_STATE
EPS = 1e-6
QK_SCALE = HEAD_DIM ** -0.5
LOG2E = 1.0 / math.log(2.0)
LANES = 128
NT_ROWS = 2 * D_ATT + 16
VMEM_LIMIT = 56 * 1024 * 1024

_NT = (((1,), (1,)), ((), ()))
_HI = lax.Precision.HIGHEST


def _silu(x):
    return x * jax.nn.sigmoid(x)


def _log_sigmoid(x):
    return jnp.minimum(x, 0.0) - jnp.log1p(jnp.exp(-jnp.abs(x)))


def _modulated_norm(x, g, scale, shift):
    y = x * lax.rsqrt(jnp.mean(x * x, axis=-1, keepdims=True) + EPS)
    return (y * g) * (1.0 + scale) + shift


def _cparams(sem, vmem=VMEM_LIMIT):
    return pltpu.CompilerParams(dimension_semantics=sem, vmem_limit_bytes=vmem)


def _tiles(seq, n_pages):
    return dict(tm=min(512, seq), tq=min(256, seq), tl=min(32, seq), ppu=min(8, n_pages // 2), ring=4)


def _ada_kernel(c_ref, w_ref, b_ref, o_ref):
    s = _silu(c_ref[...]).astype(bf16)
    o_ref[...] = jnp.dot(s, w_ref[...].astype(bf16), preferred_element_type=f32) + b_ref[...]


def _ada_call(c_all, w_ada, b_ada):
    r, d = c_all.shape
    n_out = w_ada.shape[1]
    tn = 768
    return pl.pallas_call(
        _ada_kernel,
        grid=(n_out // tn,),
        in_specs=[pl.BlockSpec((r, d), lambda j: (0, 0)),
                  pl.BlockSpec((d, tn), lambda j: (0, j)),
                  pl.BlockSpec((1, tn), lambda j: (0, j))],
        out_specs=pl.BlockSpec((r, tn), lambda j: (0, j)),
        out_shape=jax.ShapeDtypeStruct((r, n_out), f32),
        compiler_params=_cparams(("arbitrary",)),
        name="ada",
    )(c_all, w_ada, b_ada)


def _ssm_param_kernel(are_ref, aim_ref, ldt_ref, bre_ref, bim_ref,
                      abr_ref, abi_ref, bbr_ref, bbi_ref):
    lr, li = are_ref[...], aim_ref[...]
    dt = jnp.exp(ldt_ref[...])
    mag = jnp.exp(lr * dt)
    abr = mag * jnp.cos(li * dt)
    abi = mag * jnp.sin(li * dt)
    xr, xi = abr - 1.0, abi
    den = lr * lr + li * li
    cr = (xr * lr + xi * li) / den
    ci = (xi * lr - xr * li) / den
    br, bi = bre_ref[...], bim_ref[...]
    abr_ref[...] = abr
    abi_ref[...] = abi
    bbr_ref[...] = cr * br - ci * bi
    bbi_ref[...] = cr * bi + ci * br


def _ssm_param_call(a_re, a_im, log_dt, b_re, b_im):
    g, p = a_re.shape
    hc = b_re.shape[-1]
    rep = lambda a: jnp.broadcast_to(a[:, None, :], (g, hc, p)).reshape(g * hc, p)
    ldt = jnp.broadcast_to(log_dt[:, None, None], (g, hc, p)).reshape(g * hc, p)
    bt = lambda b: jnp.swapaxes(b, 1, 2).reshape(g * hc, p)
    shp = jax.ShapeDtypeStruct((g * hc, p), f32)
    abr, abi, bbr, bbi = pl.pallas_call(
        _ssm_param_kernel, out_shape=[shp] * 4, name="ssm_par",
    )(rep(a_re), rep(a_im), ldt, bt(b_re), bt(b_im))
    first = lambda a: a.reshape(g, hc, p)[:, 0, :]
    return first(abr), first(abi), bbr.reshape(g, hc, p), bbi.reshape(g, hc, p)


def _block_diag(blocks):
    *lead, g, r, c = blocks.shape
    eye = jnp.eye(g, dtype=blocks.dtype)
    return (blocks[..., :, :, None, :] * eye[:, None, :, None]).reshape(*lead, g * r, g * c)


def _split3(x):
    hi = x.astype(bf16).astype(f32)
    r = x - hi
    mid = r.astype(bf16).astype(f32)
    return hi, mid, r - mid


AUG = HEAD_DIM


def _pre_prompt_kernel(x_ref, shift_ref, scale_ref, g_ref, wnn_ref, wnt_ref, bf_ref,
                       qa_ref, zatt_ref, u_ref, zssm_ref, kt_ref, vt_ref, lft_ref, ktb_ref, vtb_ref, cum_ref,
                       carry, parts):
    i = pl.program_id(1)
    tm = x_ref.shape[0]

    @pl.when(i == 0)
    def _():
        carry[...] = jnp.zeros_like(carry)

    @pl.when((pl.program_id(0) == 0) & (i == 0))
    def _():
        parts[...] = jnp.zeros_like(parts)

    h = _modulated_norm(x_ref[...], g_ref[...], scale_ref[...], shift_ref[...])
    hb = h.astype(bf16)
    pt = lax.dot_general(wnt_ref[...], hb, _NT, preferred_element_type=f32)
    kt_ref[...] = pt[:D_ATT]
    vt_ref[...] = pt[D_ATT:2 * D_ATT]
    ktb_ref[...] = pt[:D_ATT].astype(bf16)
    vtb_ref[...] = pt[D_ATT:2 * D_ATT].astype(bf16)
    lf = _log_sigmoid(pt[2 * D_ATT:2 * D_ATT + ATT_HEADS] + bf_ref[...])
    lft_ref[...] = lf
    p = jnp.dot(hb, wnn_ref[...], preferred_element_type=f32)
    zatt_ref[...] = p[:, D_ATT:2 * D_ATT]
    u_ref[...] = p[:, 2 * D_ATT:2 * D_ATT + D_SSM]
    zssm_ref[...] = p[:, 2 * D_ATT + D_SSM:]

    c = lf
    lane = lax.broadcasted_iota(jnp.int32, c.shape, 1)
    d = 1
    while d < tm:
        c = c + jnp.where(lane >= d, pltpu.roll(c, d, axis=1), 0.0)
        d *= 2
    c = c + jnp.concatenate([carry[...]] * (tm // LANES), axis=1)
    carry[...] = jnp.broadcast_to(c[:, tm - 1:tm], carry.shape)
    c = c * LOG2E
    cum_ref[...] = c
    for j, part in enumerate(_split3(c)):
        for hd in range(ATT_HEADS):
            parts[3 * hd + j:3 * hd + j + 1, :] = part[hd:hd + 1, :]
    cq_parts = parts[...].T
    lane_q = lax.broadcasted_iota(jnp.int32, (tm, LANES), 1)
    ones_lanes = jnp.where(lane_q < AUG + 3, 1.0, 0.0)
    for pr in range(ATT_HEADS // 2):
        qp = p[:, pr * LANES:(pr + 1) * LANES] * (QK_SCALE * LOG2E)
        for hh in range(2):
            hd = 2 * pr + hh
            qh = qp if hh == 0 else pltpu.roll(qp, HEAD_DIM, axis=1)
            cq = pltpu.roll(cq_parts, AUG + 3 - 3 * hd, axis=1)
            ext = jnp.where((lane_q >= AUG + 3) & (lane_q < AUG + 6), cq, ones_lanes)
            qa_ref[hd] = jnp.where(lane_q < HEAD_DIM, qh, ext).astype(bf16)


def _pre_prompt_call(x, shift, scale, g_norm, w_nn, w_nt, b_f, tm):
    n, l, d = x.shape
    row = lambda n_, i: (n_, i, 0)
    col = lambda n_, i: (n_, 0, i)
    const2 = lambda n_, i: (0, 0)
    out_shape = [
        jax.ShapeDtypeStruct((n, ATT_HEADS, l, LANES), bf16),
        jax.ShapeDtypeStruct((n, l, D_ATT), f32),
        jax.ShapeDtypeStruct((n, l, D_SSM), f32),
        jax.ShapeDtypeStruct((n, l, D_SSM), f32),
        jax.ShapeDtypeStruct((n, D_ATT, l), f32),
        jax.ShapeDtypeStruct((n, D_ATT, l), f32),
        jax.ShapeDtypeStruct((n, ATT_HEADS, l), f32),
        jax.ShapeDtypeStruct((n, D_ATT, l), bf16),
        jax.ShapeDtypeStruct((n, D_ATT, l), bf16),
        jax.ShapeDtypeStruct((n, ATT_HEADS, l), f32),
    ]
    out_specs = [
        pl.BlockSpec((None, ATT_HEADS, tm, LANES), lambda n_, i: (n_, 0, i, 0)),
        pl.BlockSpec((None, tm, D_ATT), row),
        pl.BlockSpec((None, tm, D_SSM), row),
        pl.BlockSpec((None, tm, D_SSM), row),
        pl.BlockSpec((None, D_ATT, tm), col),
        pl.BlockSpec((None, D_ATT, tm), col),
        pl.BlockSpec((None, ATT_HEADS, tm), col),
        pl.BlockSpec((None, D_ATT, tm), col),
        pl.BlockSpec((None, D_ATT, tm), col),
        pl.BlockSpec((None, ATT_HEADS, tm), col),
    ]
    in_specs = [
        pl.BlockSpec((None, tm, d), row),
        pl.BlockSpec((None, 1, d), lambda n_, i: (n_, 0, 0)),
        pl.BlockSpec((None, 1, d), lambda n_, i: (n_, 0, 0)),
        pl.BlockSpec((1, d), const2),
        pl.BlockSpec(w_nn.shape, const2),
        pl.BlockSpec(w_nt.shape, const2),
        pl.BlockSpec((ATT_HEADS, tm), const2),
    ]
    return pl.pallas_call(
        _pre_prompt_kernel, grid=(n, l // tm), in_specs=in_specs, out_specs=out_specs,
        out_shape=out_shape,
        scratch_shapes=[pltpu.VMEM((ATT_HEADS, LANES), f32), pltpu.VMEM((LANES, tm), f32)],
        compiler_params=_cparams(("arbitrary", "arbitrary")), name="pre",
    )(x, shift, scale, g_norm, w_nn, w_nt, jnp.broadcast_to(b_f[:, None], (ATT_HEADS, tm)))


def _pre_sample_kernel(x_ref, shift_ref, scale_ref, g_ref, wnn_ref, wnt_ref, bf_ref, seg_ref,
                       q_ref, k_ref, v_ref, zatt_ref, u_ref, zssm_ref, lf_ref, snew_ref):
    hb = _modulated_norm(x_ref[...], g_ref[...], scale_ref[...], shift_ref[...]).astype(bf16)
    p = jnp.dot(hb, wnn_ref[...], preferred_element_type=f32)
    pk = lax.dot_general(hb, wnt_ref[...], _NT, preferred_element_type=f32)
    q = p[:, :D_ATT] * QK_SCALE
    k = pk[:, :D_ATT]
    q_ref[...] = q
    k_ref[...] = k
    v_ref[...] = pk[:, D_ATT:2 * D_ATT]
    zatt_ref[...] = p[:, D_ATT:2 * D_ATT]
    u_ref[...] = p[:, 2 * D_ATT:2 * D_ATT + D_SSM]
    zssm_ref[...] = p[:, 2 * D_ATT + D_SSM:]
    lf_ref[...] = _log_sigmoid(pk[:, 2 * D_ATT:] + bf_ref[...])
    snew_ref[...] = jnp.dot(q * k, seg_ref[...], preferred_element_type=f32, precision=_HI)


def _pre_sample_call(x, shift, scale, g_norm, w_nn, w_nt, b_f_pad, seg):
    r, d = x.shape
    shp = lambda c: jax.ShapeDtypeStruct((r, c), f32)
    return pl.pallas_call(
        _pre_sample_kernel,
        out_shape=[shp(D_ATT), shp(D_ATT), shp(D_ATT), shp(D_ATT), shp(D_SSM), shp(D_SSM),
                   shp(NT_ROWS - 2 * D_ATT), shp(LANES)],
        compiler_params=pltpu.CompilerParams(vmem_limit_bytes=VMEM_LIMIT), name="pre_s",
    )(x, shift, scale, g_norm, w_nn, w_nt, b_f_pad, seg)


def _att_kernel(pt_ref, qa_ref, kt_ref, vt_ref, cum_ref, z_ref,
                snew_ref, lfnew_ref, vnew_ref, tri_ref, qb_hbm, kc_hbm, vc_hbm, lfc_hbm,
                o_ref, od_ref,
                ka, va, m_s, acc_s,
                kbuf, vbuf, lfbuf, qbuf, sc, bias, tot, p_s, dacc, stat, ksem, vsem, lfsem, qsem,
                *, tq, seq, n_tok, n_pages, ppu, ring):
    n = pl.program_id(0)
    qi = pl.program_id(1)
    n_tiles = seq // tq
    units = n_pages // ppu
    ahead = ring - 1
    k_end = n_tok * units
    v_lo, v_hi = units, (n_tok + 1) * units

    def start_pages(src, buf, sem, tok, sub, slot):
        for j in range(ppu):
            pltpu.make_async_copy(src.at[pt_ref[tok, sub * ppu + j]], buf.at[slot, j], sem.at[slot]).start()

    def wait_pages(src, buf, sem, slot):
        for j in range(ppu):
            pltpu.make_async_copy(src.at[0], buf.at[slot, j], sem.at[slot]).wait()

    def request(w):
        tok, sub, slot = w // units, w % units, w % ring

        @pl.when(w < k_end)
        def _():
            start_pages(kc_hbm, kbuf, ksem, tok, sub, slot)

        @pl.when((w >= v_lo) & (w < v_hi))
        def _():
            start_pages(vc_hbm, vbuf, vsem, tok - 1, sub, slot)

    def start_token_inputs(t):
        slot = t % 2

        def body(pg, carry):
            pltpu.make_async_copy(lfc_hbm.at[pt_ref[t, pg]], lfbuf.at[slot, pg], lfsem.at[slot]).start()
            return carry
        lax.fori_loop(0, n_pages, body, 0)
        pltpu.make_async_copy(qb_hbm.at[t], qbuf.at[slot], qsem.at[slot]).start()

    def new_token(t):
        slot = t % 2

        @pl.when(t + 1 < n_tok)
        def _():
            start_token_inputs(t + 1)

        def wait_lf(pg, carry):
            pltpu.make_async_copy(lfc_hbm.at[0], lfbuf.at[slot, pg], lfsem.at[slot]).wait()
            return carry
        lax.fori_loop(0, n_pages, wait_lf, 0)
        pltpu.make_async_copy(qb_hbm.at[0], qbuf.at[slot], qsem.at[slot]).wait()
        lf2 = lfbuf[slot].reshape(n_pages * ATT_HEADS, LANES)
        sums = sum(jnp.dot(part.astype(bf16), tri_ref[...], preferred_element_type=f32) for part in _split3(lf2))
        bias[slot] = sums[:, :LANES].reshape(n_pages, ATT_HEADS, LANES)
        tot[...] = sums[:, LANES:].reshape(n_pages, ATT_HEADS, LANES)
        lfnew = lfnew_ref[t]

        def suffix(i, run):
            pg = n_pages - 1 - i
            bias[slot, pg] = bias[slot, pg] + run + lfnew
            return run + tot[pg]
        lax.fori_loop(0, n_pages, suffix, jnp.zeros((ATT_HEADS, LANES), f32))

    def token_softmax(t):
        s_all = sc[t % 2] + bias[t % 2]
        snew = snew_ref[t]
        m = jnp.max(jnp.max(s_all, axis=0), axis=-1, keepdims=True)
        m = jnp.maximum(m, snew)
        p_all = jnp.exp(s_all - m[None])
        p_s[...] = p_all
        p_new = jnp.exp(snew - m)
        stat[0] = p_new
        stat[1] = jnp.sum(jnp.sum(p_all, axis=0), axis=-1, keepdims=True) + p_new
        dacc[...] = jnp.zeros_like(dacc)

    def token_output(t):
        ones_row = jnp.ones((ATT_HEADS, LANES), bf16)
        head_row = lax.broadcasted_iota(jnp.int32, (ATT_HEADS, HEAD_DIM), 0)
        past = jnp.zeros((ATT_HEADS, HEAD_DIM), f32)
        for h in range(ATT_HEADS):
            r = sum(lax.dot_general(ones_row, part.astype(bf16), _NT, preferred_element_type=f32)
                    for part in _split3(dacc[h]))
            past = jnp.where(head_row == h, r, past)
        od_ref[t] = (past + stat[0][:, :HEAD_DIM] * vnew_ref[t]) * (1.0 / stat[1][:, :HEAD_DIM])

    def unit_prologue(w):
        t = w // units

        @pl.when(w % units == 0)
        def _():
            @pl.when((t >= 2) & (t <= n_tok + 1))
            def _():
                token_output(t - 2)

            @pl.when((t >= 1) & (t <= n_tok))
            def _():
                token_softmax(t - 1)

            @pl.when(t < n_tok)
            def _():
                new_token(t)

        request(w + ahead)

        @pl.when(w < k_end)
        def _():
            wait_pages(kc_hbm, kbuf, ksem, w % ring)

        @pl.when((w >= v_lo) & (w < v_hi))
        def _():
            wait_pages(vc_hbm, vbuf, vsem, w % ring)

    def unit_compute(w, h, ready):
        tok, sub, slot = w // units, w % units, w % ring
        qh, a = qbuf[jnp.minimum(tok, n_tok - 1) % 2, h], dacc[h]
        if ready is not None:
            qh, a = jnp.where(ready, qh, 0.0), jnp.where(ready, a, 0.0)
        for j in range(ppu):
            sc[tok % 2, sub * ppu + j, pl.ds(h, 1), :] = jnp.sum(kbuf[slot, j, h] * qh, axis=0, keepdims=True)
            pr = p_s[sub * ppu + j, pl.ds(h, 1), :]
            a = a + vbuf[slot, j, h] * jnp.broadcast_to(pr, (HEAD_DIM, LANES))
        dacc[h] = a

    @pl.when((n == 0) & (qi == 0))
    def _():
        vbuf[...] = jnp.zeros_like(vbuf)
        p_s[...] = jnp.zeros_like(p_s)
        dacc[...] = jnp.zeros_like(dacc)
        start_token_inputs(jnp.int32(0))
        for w0 in range(ahead):
            request(jnp.int32(w0))

    @pl.when(qi == 0)
    def _():
        c = cum_ref[...]
        r16 = lax.broadcasted_iota(jnp.int32, (16, seq), 0)
        for h in range(ATT_HEADS):
            ka[h, 0:HEAD_DIM, :] = kt_ref[h * HEAD_DIM:(h + 1) * HEAD_DIM, :]
            va[h, 0:HEAD_DIM, :] = vt_ref[h * HEAD_DIM:(h + 1) * HEAD_DIM, :]
            hi, mid, lo = _split3(c[h:h + 1, :])
            ext = jnp.where(r16 == 0, -hi, jnp.where(r16 == 1, -mid, jnp.where(r16 == 2, -lo,
                            jnp.where(r16 < 6, 1.0, 0.0))))
            ka[h, AUG:AUG + 16, :] = ext.astype(bf16)
            ka[h, AUG + 16:, :] = jnp.zeros((LANES - AUG - 16, seq), bf16)
            va[h, HEAD_DIM:, :] = jnp.ones((LANES - HEAD_DIM, seq), bf16)

    lane_q = lax.broadcasted_iota(jnp.int32, (tq, LANES), 1)
    row = lax.broadcasted_iota(jnp.int32, (tq, tq), 0)
    col = lax.broadcasted_iota(jnp.int32, (tq, tq), 1)
    causal = col <= row

    first_unit = n * (n_tiles * (n_tiles + 1) // 2) + (qi * (qi + 1)) // 2

    def block(kb, diagonal):
        w = first_unit if diagonal else first_unit + 1 + kb
        unit_prologue(w)
        k0 = pl.multiple_of(kb * tq, tq)
        for h in range(ATT_HEADS):
            s = jnp.dot(qa_ref[h], ka[h, :, pl.ds(k0, tq)], preferred_element_type=f32)
            if diagonal:
                s = jnp.where(causal, s, -jnp.inf)
            rowmax = jnp.broadcast_to(jnp.max(s, axis=-1, keepdims=True), (tq, LANES))
            m_new = rowmax if diagonal else jnp.maximum(m_s[h], rowmax)
            p = [jnp.exp2(s[:, j * LANES:(j + 1) * LANES] - m_new) for j in range(tq // LANES)]
            p = jnp.concatenate(p, axis=1)
            pv = lax.dot_general(p.astype(bf16), va[h, :, pl.ds(k0, tq)], _NT, preferred_element_type=f32)
            acc_s[h] = pv if diagonal else jnp.exp2(m_s[h] - m_new) * acc_s[h] + pv
            m_s[h] = m_new
            unit_compute(w, h, m_new[0:HEAD_DIM, :] < jnp.inf)

    def off_diagonal(kb, carry):
        block(kb, False)
        return carry

    block(qi, True)
    lax.fori_loop(0, qi, off_diagonal, 0)

    for pr in range(ATT_HEADS // 2):
        a0, a1 = acc_s[2 * pr], acc_s[2 * pr + 1]
        o0 = a0 * pltpu.roll(1.0 / a0, HEAD_DIM, axis=1)
        o1 = pltpu.roll(a1, HEAD_DIM, axis=1) * (1.0 / a1)
        o = jnp.where(lane_q < HEAD_DIM, o0, o1)
        cols = slice(pr * LANES, (pr + 1) * LANES)
        o_ref[:, cols] = (o * _silu(z_ref[:, cols])).astype(bf16)


def _att_call(qa, ktb, vtb, cum, z_att, page_table, qb, snew_b, lfnew_b, vnew, kc, vc, lfc, tq, ppu, ring):
    n, _, l, _ = qa.shape
    ns, n_pages = page_table.shape
    n_tiles = l // tq
    units = n_pages // ppu
    assert n * (n_tiles * (n_tiles + 1) // 2) >= (ns + 1) * units + 1, "too few key-block steps for the page units"
    lane = jnp.arange(LANES)
    tri = jnp.concatenate([lane[:, None] > lane[None, :], jnp.ones((LANES, LANES), bool)], axis=1).astype(bf16)
    row = lambda n_, i, pt: (n_, i, 0)
    whole = lambda n_, i, pt: (n_, 0, 0)
    const = lambda nd: (lambda n_, i, pt: (0,) * nd)
    page = (ppu, ATT_HEADS, HEAD_DIM, LANES)
    grid_spec = pltpu.PrefetchScalarGridSpec(
        num_scalar_prefetch=1,
        grid=(n, n_tiles),
        in_specs=[pl.BlockSpec((None, ATT_HEADS, tq, LANES), lambda n_, i, pt: (n_, 0, i, 0)),
                  pl.BlockSpec((None, D_ATT, l), whole),
                  pl.BlockSpec((None, D_ATT, l), whole),
                  pl.BlockSpec((None, ATT_HEADS, l), whole),
                  pl.BlockSpec((None, tq, D_ATT), row),
                  pl.BlockSpec(snew_b.shape, const(3)),
                  pl.BlockSpec(lfnew_b.shape, const(3)),
                  pl.BlockSpec(vnew.shape, const(3)),
                  pl.BlockSpec((LANES, 2 * LANES), const(2)),
                  pl.BlockSpec(memory_space=pl.ANY),
                  pl.BlockSpec(memory_space=pl.ANY),
                  pl.BlockSpec(memory_space=pl.ANY),
                  pl.BlockSpec(memory_space=pl.ANY)],
        out_specs=[pl.BlockSpec((None, tq, D_ATT), row),
                   pl.BlockSpec((ns, ATT_HEADS, HEAD_DIM), const(3))],
        scratch_shapes=[pltpu.VMEM((ATT_HEADS, LANES, l), bf16), pltpu.VMEM((ATT_HEADS, LANES, l), bf16),
                        pltpu.VMEM((ATT_HEADS, tq, LANES), f32),
                        pltpu.VMEM((ATT_HEADS, tq, LANES), f32),
                        pltpu.VMEM((ring,) + page, f32),
                        pltpu.VMEM((ring,) + page, f32),
                        pltpu.VMEM((2, n_pages, ATT_HEADS, LANES), f32),
                        pltpu.VMEM((2, ATT_HEADS, HEAD_DIM, LANES), f32),
                        pltpu.VMEM((2, n_pages, ATT_HEADS, LANES), f32),
                        pltpu.VMEM((2, n_pages, ATT_HEADS, LANES), f32),
                        pltpu.VMEM((n_pages, ATT_HEADS, LANES), f32),
                        pltpu.VMEM((n_pages, ATT_HEADS, LANES), f32),
                        pltpu.VMEM((ATT_HEADS, HEAD_DIM, LANES), f32),
                        pltpu.VMEM((2, ATT_HEADS, LANES), f32),
                        pltpu.SemaphoreType.DMA((ring,)), pltpu.SemaphoreType.DMA((ring,)),
                        pltpu.SemaphoreType.DMA((2,)), pltpu.SemaphoreType.DMA((2,))],
    )
    return pl.pallas_call(
        functools.partial(_att_kernel, tq=tq, seq=l, n_tok=ns, n_pages=n_pages, ppu=ppu, ring=ring),
        grid_spec=grid_spec,
        out_shape=[jax.ShapeDtypeStruct((n, l, D_ATT), bf16),
                   jax.ShapeDtypeStruct((ns, ATT_HEADS, HEAD_DIM), f32)],
        compiler_params=_cparams(("arbitrary", "arbitrary")), name="att",
    )(page_table, qa, ktb, vtb, cum, z_att, snew_b, lfnew_b, vnew, tri, qb, kc, vc, lfc)


def _post_math(x, att, y, z_ssm, gate, w_glu, b_glu, w_out, g_final):
    g = jax.nn.gelu(y)
    glu = jax.nn.sigmoid(jnp.dot(g.astype(bf16), w_glu, preferred_element_type=f32) + b_glu)
    ssm = g * glu * _silu(z_ssm)
    mix = jnp.concatenate([att.astype(bf16), ssm.astype(bf16)], axis=-1)
    xo = x + gate * jnp.dot(mix, w_out, preferred_element_type=f32)
    return xo * lax.rsqrt(jnp.mean(xo * xo, axis=-1, keepdims=True) + EPS) * g_final


def _ssm_kernel(*refs, nb, tl, time_major, chunk, with_post):
    u_ref, h0r_ref, h0i_ref, ar_ref, ai_ref, bre_ref, bim_ref, cre_ref, cim_ref, d_ref = refs[:10]
    if with_post:
        x_ref, att_ref, zssm_ref, gate_ref, wglu_ref, bglu_ref, wout_ref, gf_ref = refs[10:18]
        refs = refs[18:]
    else:
        refs = refs[10:]
    y_ref, hr_ref, hi_ref, utn, bu_re, bu_im, hs_re, hs_im, st_re, st_im, yscr, perm = refs[:12]
    i = pl.program_id(0)
    nj = D_SSM // LANES
    last = pl.num_programs(0) - (2 if with_post else 1)

    @pl.when(i == 0)
    def _():
        st_re[...] = h0r_ref[...]
        st_im[...] = h0i_ref[...]

    if with_post:
        y_prev = refs[12]

        @pl.when(i == 0)
        def _():
            y_prev[...] = jnp.zeros_like(y_prev)

    if time_major:
        utn[...] = u_ref[...].reshape(tl * nb, D_SSM)
    else:
        for j in range(nj):
            cols = slice(j * LANES, (j + 1) * LANES)
            for b in range(nb):
                perm[j, b * tl:(b + 1) * tl, :] = u_ref[b, :, cols]
            for t in range(tl):
                utn[t * nb:(t + 1) * nb, cols] = perm[j, pl.ds(t, nb, stride=tl), :]

    for c in range(nj):
        cols = slice(c * chunk, (c + 1) * chunk)
        ub = utn[:, c * LANES:(c + 1) * LANES].astype(bf16)
        bu_re[:, cols] = jnp.dot(ub, bre_ref[c], preferred_element_type=f32)
        bu_im[:, cols] = jnp.dot(ub, bim_ref[c], preferred_element_type=f32)

    if with_post:
        rows, d = nb * tl, x_ref.shape[-1]
        gate = jnp.broadcast_to(gate_ref[...], (nb, tl, d)).reshape(rows, d)
        out = _post_math(x_ref[...].reshape(rows, d), att_ref[...].reshape(rows, D_ATT), y_prev[...],
                         zssm_ref[...].reshape(rows, D_SSM), gate, wglu_ref[...], bglu_ref[...],
                         wout_ref[...], gf_ref[...])
        y_ref[...] = out.reshape(nb, tl, d)

    for c in range(nj):
        cols = slice(c * chunk, (c + 1) * chunk)
        ucols = slice(c * LANES, (c + 1) * LANES)
        ar = jnp.broadcast_to(ar_ref[:, cols], (nb, chunk))
        ai = jnp.broadcast_to(ai_ref[:, cols], (nb, chunk))
        hr, hi = st_re[:, cols], st_im[:, cols]
        for t in range(tl):
            rows = slice(t * nb, (t + 1) * nb)
            hr, hi = (ar * hr - ai * hi + bu_re[rows, cols], ar * hi + ai * hr + bu_im[rows, cols])
            hs_re[rows, cols] = hr.astype(bf16)
            hs_im[rows, cols] = hi.astype(bf16)
        st_re[:, cols] = hr
        st_im[:, cols] = hi
        yscr[:, ucols] = (jnp.dot(hs_re[:, cols], cre_ref[c], preferred_element_type=f32)
                          - jnp.dot(hs_im[:, cols], cim_ref[c], preferred_element_type=f32)
                          + d_ref[:, ucols] * utn[:, ucols])
    if time_major:
        y_ref[...] = yscr[...].reshape(tl, nb, D_SSM)
    else:
        for j in range(nj):
            cols = slice(j * LANES, (j + 1) * LANES)
            for t in range(tl):
                perm[j, pl.ds(t, nb, stride=tl), :] = yscr[t * nb:(t + 1) * nb, cols]
            if with_post:
                y_prev[:, cols] = perm[j]
            else:
                for b in range(nb):
                    y_ref[b, :, cols] = perm[j, b * tl:(b + 1) * tl, :]

    @pl.when(i == last)
    def _():
        hr_ref[...] = st_re[...]
        hi_ref[...] = st_im[...]


def _ssm_call(u, h0_re, h0_im, ab_re, ab_im, b_re, b_im, c_re, c_im, d_skip, tl, time_major, post=None):
    nj = D_SSM // LANES
    if time_major:
        l, nb = u.shape[0], u.shape[1]
        block = lambda w: pl.BlockSpec((tl, nb, w), lambda i: (i, 0, 0))
    else:
        nb, l = u.shape[0], u.shape[1]
        block = lambda w: pl.BlockSpec((nb, tl, w), lambda i: (0, i, 0))
    rows = tl * nb
    steps = l // tl
    full = lambda a: pl.BlockSpec(a.shape, lambda i, nd=a.ndim: (0,) * nd)
    ins = [u, h0_re, h0_im, ab_re, ab_im, b_re, b_im, c_re, c_im, d_skip]
    in_specs = [block(D_SSM)] + [full(a) for a in ins[1:]]
    out_spec, out_width, extra_scratch = block(D_SSM), D_SSM, []
    if post is not None:
        assert not time_major
        x, att, z_ssm = post[:3]
        out_width = x.shape[-1]
        behind = lambda w: pl.BlockSpec((nb, tl, w), lambda i: (0, jnp.maximum(i - 1, 0), 0))
        in_specs[0] = pl.BlockSpec((nb, tl, D_SSM), lambda i, last_block=steps - 1: (0, jnp.minimum(i, last_block), 0))
        ins += list(post)
        in_specs += [behind(out_width), behind(D_ATT), behind(D_SSM)] + [full(a) for a in post[3:]]
        out_spec = behind(out_width)
        extra_scratch = [pltpu.VMEM((rows, D_SSM), f32)]
        steps += 1
    return pl.pallas_call(
        functools.partial(_ssm_kernel, nb=nb, tl=tl, time_major=time_major, chunk=N_STATE // nj,
                          with_post=post is not None),
        grid=(steps,),
        in_specs=in_specs,
        out_specs=[out_spec, full(h0_re), full(h0_im)],
        out_shape=[jax.ShapeDtypeStruct(u.shape[:2] + (out_width,), f32),
                   jax.ShapeDtypeStruct(h0_re.shape, f32), jax.ShapeDtypeStruct(h0_im.shape, f32)],
        scratch_shapes=[pltpu.VMEM((rows, D_SSM), f32),
                        pltpu.VMEM((rows, N_STATE), f32), pltpu.VMEM((rows, N_STATE), f32),
                        pltpu.VMEM((rows, N_STATE), bf16), pltpu.VMEM((rows, N_STATE), bf16),
                        pltpu.VMEM((nb, N_STATE), f32), pltpu.VMEM((nb, N_STATE), f32),
                        pltpu.VMEM((rows, D_SSM), f32),
                        pltpu.VMEM((nj, rows, LANES), f32)] + extra_scratch,
        compiler_params=_cparams(("arbitrary",)), name="ssm",
    )(*ins)


def _post_kernel(*refs, gate_att):
    if gate_att:
        x_ref, att_ref, zatt_ref, y_ref, zssm_ref, gate_ref, wglu_ref, bglu_ref, wout_ref, gf_ref, o_ref = refs
        att = att_ref[...] * _silu(zatt_ref[...])
    else:
        x_ref, att_ref, y_ref, zssm_ref, gate_ref, wglu_ref, bglu_ref, wout_ref, gf_ref, o_ref = refs
        att = att_ref[...]
    o_ref[...] = _post_math(x_ref[...], att, y_ref[...], zssm_ref[...], gate_ref[...], wglu_ref[...], bglu_ref[...],
                            wout_ref[...], gf_ref[...])


def _post_call(x, att, z_att, y, z_ssm, gate, w_glu, b_glu, w_out, g_final, tm):
    n, l, d = x.shape
    row = lambda n_, i: (n_, i, 0)
    const2 = lambda n_, i: (0, 0)
    gate_spec = (pl.BlockSpec((None, 1, d), lambda n_, i: (n_, 0, 0)) if gate.shape[1] == 1
                 else pl.BlockSpec((None, tm, d), row))
    ins = [x, att] + ([z_att] if z_att is not None else []) + [y, z_ssm, gate, w_glu, b_glu, w_out, g_final]
    in_specs = ([pl.BlockSpec((None, tm, d), row), pl.BlockSpec((None, tm, D_ATT), row)]
                + ([pl.BlockSpec((None, tm, D_ATT), row)] if z_att is not None else [])
                + [pl.BlockSpec((None, tm, D_SSM), row),
                   pl.BlockSpec((None, tm, D_SSM), row), gate_spec,
                   pl.BlockSpec(w_glu.shape, const2), pl.BlockSpec((1, D_SSM), const2),
                   pl.BlockSpec(w_out.shape, const2), pl.BlockSpec((1, d), const2)])
    return pl.pallas_call(
        functools.partial(_post_kernel, gate_att=z_att is not None),
        grid=(n, l // tm), in_specs=in_specs,
        out_specs=pl.BlockSpec((None, tm, d), row),
        out_shape=jax.ShapeDtypeStruct((n, l, d), f32),
        compiler_params=_cparams(("arbitrary", "arbitrary")), name="post",
    )(*ins)


def kernel(x_prompt, x_sample, c_prompt, c_sample, cache_k, cache_v, cache_logf, state_ssm_re, state_ssm_im,
           page_table, g_norm, w_ada, b_ada, w_in, b_fgate, a_re, a_im, log_dt, b_re, b_im, c_re, c_im, d_skip,
           w_glu, b_glu, w_out, g_final):
    n, l, d = x_prompt.shape
    ns = x_sample.shape[0]
    depth = w_in.shape[0]
    assert depth == 1 and x_sample.shape[1] == 1
    lyr = 0
    xs = x_sample.reshape(ns, d)

    splits = [0, D_ATT, 2 * D_ATT, 3 * D_ATT, 4 * D_ATT, 4 * D_ATT + ATT_HEADS,
              4 * D_ATT + ATT_HEADS + D_SSM, 4 * D_ATT + ATT_HEADS + 2 * D_SSM]
    w_t = jnp.swapaxes(w_in[lyr], 0, 1)
    wq, wk, wv, wz, wf, wu, wzs = [w_t[splits[i]:splits[i + 1]] for i in range(7)]
    w_nn = jnp.concatenate([wq, wz, wu, wzs], axis=0).T.astype(bf16)
    w_nt = jnp.concatenate([wk, wv, wf, jnp.zeros((NT_ROWS - 2 * D_ATT - ATT_HEADS, d), f32)], axis=0).astype(bf16)
    b_f = b_fgate[lyr]
    b_f_pad = jnp.concatenate([b_f, jnp.zeros((NT_ROWS - 2 * D_ATT - ATT_HEADS,), f32)])[None, :]
    seg = (jnp.arange(D_ATT)[:, None] // HEAD_DIM == jnp.arange(LANES)[None, :]).astype(f32)
    gn = g_norm[lyr][None, :]
    gf = g_final[None, :]
    w_glu_b = w_glu[lyr].astype(bf16)
    w_out_b = w_out[lyr].astype(bf16)
    b_glu2 = b_glu[lyr][None, :]

    mod = _ada_call(jnp.concatenate([c_prompt, c_sample], axis=0), w_ada[lyr], b_ada[lyr][None, :])
    shift_p, scale_p, gate_p = [mod[:n, k * d:(k + 1) * d].reshape(n, 1, d) for k in range(3)]
    shift_s, scale_s, gate_s = [mod[n:, k * d:(k + 1) * d] for k in range(3)]

    ab_re, ab_im, bb_re, bb_im = _ssm_param_call(a_re[lyr], a_im[lyr], log_dt[lyr], b_re[lyr], b_im[lyr])
    ab_re_row = ab_re.reshape(1, N_STATE)
    ab_im_row = ab_im.reshape(1, N_STATE)
    slabs = D_SSM // LANES
    slab_diag = lambda w: _block_diag(w.reshape((slabs, SSM_GROUPS // slabs) + w.shape[1:]))
    bd_re = slab_diag(bb_re).astype(bf16)
    bd_im = slab_diag(bb_im).astype(bf16)
    cd_re = slab_diag(jnp.swapaxes(c_re[lyr], 1, 2)).astype(bf16)
    cd_im = slab_diag(jnp.swapaxes(c_im[lyr], 1, 2)).astype(bf16)
    d_row = d_skip[lyr].reshape(1, D_SSM)

    n_pages = page_table.shape[1]
    tiles = _tiles(l, n_pages)
    qa, z_att, u_p, z_ssm, kt, vt, lft, ktb, vtb, cum = _pre_prompt_call(x_prompt, shift_p, scale_p, gn, w_nn, w_nt,
                                                                         b_f, tiles["tm"])
    qs, k_s, v_s, zatt_s, u_s, zssm_s, lf_s, snew = _pre_sample_call(xs, shift_s, scale_s, gn, w_nn, w_nt,
                                                                     b_f_pad, seg)

    qb = jnp.broadcast_to(qs.reshape(ns, ATT_HEADS, HEAD_DIM, 1), (ns, ATT_HEADS, HEAD_DIM, LANES))
    snew_b = jnp.broadcast_to(snew[:, :ATT_HEADS, None], (ns, ATT_HEADS, LANES))
    lfnew_b = jnp.broadcast_to(lf_s[:, :ATT_HEADS, None], (ns, ATT_HEADS, LANES))
    kc = jnp.transpose(cache_k[lyr], (0, 2, 3, 1))
    vc = jnp.transpose(cache_v[lyr], (0, 2, 3, 1))
    lfc = jnp.transpose(cache_logf[lyr], (0, 2, 1))
    att, att_s = _att_call(qa, ktb, vtb, cum, z_att, page_table, qb, snew_b, lfnew_b,
                           v_s.reshape(ns, ATT_HEADS, HEAD_DIM), kc, vc, lfc,
                           tq=tiles["tq"], ppu=tiles["ppu"], ring=tiles["ring"])

    zeros_state = jnp.zeros((n, N_STATE), f32)
    y_prompt, hp_re, hp_im = _ssm_call(u_p, zeros_state, zeros_state, ab_re_row, ab_im_row, bd_re, bd_im, cd_re, cd_im,
                                       d_row, tl=tiles["tl"], time_major=False,
                                       post=(x_prompt, att, z_ssm, gate_p, w_glu_b, b_glu2, w_out_b, gf))

    heads_t = lambda t: jnp.transpose(t.reshape(n, ATT_HEADS, HEAD_DIM, l), (0, 3, 1, 2))[None]
    k_prompt = heads_t(kt)
    v_prompt = heads_t(vt)
    logf_prompt = jnp.transpose(lft, (0, 2, 1))[None]

    h0r = state_ssm_re[lyr].reshape(ns, N_STATE)
    h0i = state_ssm_im[lyr].reshape(ns, N_STATE)
    y_s, hs_re, hs_im = _ssm_call(u_s[None], h0r, h0i, ab_re_row, ab_im_row, bd_re, bd_im, cd_re, cd_im, d_row,
                                  tl=1, time_major=True)
    y_sample = _post_call(xs[None], att_s.reshape(1, ns, D_ATT), zatt_s[None], y_s, zssm_s[None], gate_s[None],
                          w_glu_b, b_glu2, w_out_b, gf, ns)

    st = lambda a, b: a.reshape(1, b, SSM_GROUPS, SSM_STATE)
    return (y_prompt, y_sample.reshape(ns, 1, d),
            k_prompt, v_prompt, logf_prompt, st(hp_re, n), st(hp_im, n),
            k_s.reshape(1, ns, 1, ATT_HEADS, HEAD_DIM), v_s.reshape(1, ns, 1, ATT_HEADS, HEAD_DIM),
            lf_s[:, :ATT_HEADS].reshape(1, ns, 1, ATT_HEADS), st(hs_re, ns), st(hs_im, ns))
```

```python
import functools
import math

import jax
import jax.numpy as jnp
from jax import lax
from jax.experimental import pallas as pl
from jax.experimental.pallas import tpu as pltpu

f32 = jnp.float32
bf16 = jnp.bfloat16

HEAD_DIM = 64
ATT_HEADS = 8
D_ATT = ATT_HEADS * HEAD_DIM
SSM_GROUP = 16
SSM_GROUPS = 32
SSM_STATE = 64
D_SSM = SSM_GROUP * SSM_GROUPS
N_STATE = SSM_GROUPS * SSM_STATE
EPS = 1e-6
QK_SCALE = HEAD_DIM ** -0.5
LOG2E = 1.0 / math.log(2.0)
LANES = 128
NT_ROWS = 2 * D_ATT + 16
VMEM_LIMIT = 56 * 1024 * 1024

_NT = (((1,), (1,)), ((), ()))
_HI = lax.Precision.HIGHEST


def _silu(x):
    return x * jax.nn.sigmoid(x)


def _log_sigmoid(x):
    return jnp.minimum(x, 0.0) - jnp.log1p(jnp.exp(-jnp.abs(x)))


def _modulated_norm(x, g, scale, shift):
    y = x * lax.rsqrt(jnp.mean(x * x, axis=-1, keepdims=True) + EPS)
    return (y * g) * (1.0 + scale) + shift


def _cparams(sem, vmem=VMEM_LIMIT):
    return pltpu.CompilerParams(dimension_semantics=sem, vmem_limit_bytes=vmem)


def _tiles(seq, n_pages):
    return dict(tm=min(1024, seq), tq=min(256, seq), tl=min(32, seq), ppu=min(8, n_pages // 2), ring=4)


def _ada_kernel(c_ref, w_ref, b_ref, o_ref):
    s = _silu(c_ref[...]).astype(bf16)
    o_ref[...] = jnp.dot(s, w_ref[...].astype(bf16), preferred_element_type=f32) + b_ref[...]


def _ada_call(c_all, w_ada, b_ada):
    r, d = c_all.shape
    n_out = w_ada.shape[1]
    tn = 768
    return pl.pallas_call(
        _ada_kernel,
        grid=(n_out // tn,),
        in_specs=[pl.BlockSpec((r, d), lambda j: (0, 0)),
                  pl.BlockSpec((d, tn), lambda j: (0, j)),
                  pl.BlockSpec((1, tn), lambda j: (0, j))],
        out_specs=pl.BlockSpec((r, tn), lambda j: (0, j)),
        out_shape=jax.ShapeDtypeStruct((r, n_out), f32),
        compiler_params=_cparams(("arbitrary",)),
        name="ada",
    )(c_all, w_ada, b_ada)


def _ssm_param_kernel(are_ref, aim_ref, ldt_ref, bre_ref, bim_ref,
                      abr_ref, abi_ref, bbr_ref, bbi_ref):
    lr, li = are_ref[...], aim_ref[...]
    dt = jnp.exp(ldt_ref[...])
    mag = jnp.exp(lr * dt)
    abr = mag * jnp.cos(li * dt)
    abi = mag * jnp.sin(li * dt)
    xr, xi = abr - 1.0, abi
    den = lr * lr + li * li
    cr = (xr * lr + xi * li) / den
    ci = (xi * lr - xr * li) / den
    br, bi = bre_ref[...], bim_ref[...]
    abr_ref[...] = abr
    abi_ref[...] = abi
    bbr_ref[...] = cr * br - ci * bi
    bbi_ref[...] = cr * bi + ci * br


def _ssm_param_call(a_re, a_im, log_dt, b_re, b_im):
    g, p = a_re.shape
    hc = b_re.shape[-1]
    rep = lambda a: jnp.broadcast_to(a[:, None, :], (g, hc, p)).reshape(g * hc, p)
    ldt = jnp.broadcast_to(log_dt[:, None, None], (g, hc, p)).reshape(g * hc, p)
    bt = lambda b: jnp.swapaxes(b, 1, 2).reshape(g * hc, p)
    shp = jax.ShapeDtypeStruct((g * hc, p), f32)
    abr, abi, bbr, bbi = pl.pallas_call(
        _ssm_param_kernel, out_shape=[shp] * 4, name="ssm_par",
    )(rep(a_re), rep(a_im), ldt, bt(b_re), bt(b_im))
    first = lambda a: a.reshape(g, hc, p)[:, 0, :]
    return first(abr), first(abi), bbr.reshape(g, hc, p), bbi.reshape(g, hc, p)


def _block_diag(blocks):
    *lead, g, r, c = blocks.shape
    eye = jnp.eye(g, dtype=blocks.dtype)
    return (blocks[..., :, :, None, :] * eye[:, None, :, None]).reshape(*lead, g * r, g * c)


def _split3(x):
    hi = x.astype(bf16).astype(f32)
    r = x - hi
    mid = r.astype(bf16).astype(f32)
    return hi, mid, r - mid


AUG = HEAD_DIM


def _pre_prompt_kernel(x_ref, shift_ref, scale_ref, g_ref, wnn_ref, wnt_ref, bf_ref,
                       qa_ref, zatt_ref, u_ref, zssm_ref, kt_ref, vt_ref, lft_ref, ktb_ref, vtb_ref, cum_ref,
                       carry, parts):
    i = pl.program_id(1)
    tm = x_ref.shape[0]

    @pl.when(i == 0)
    def _():
        carry[...] = jnp.zeros_like(carry)

    @pl.when((pl.program_id(0) == 0) & (i == 0))
    def _():
        parts[...] = jnp.zeros_like(parts)

    h = _modulated_norm(x_ref[...], g_ref[...], scale_ref[...], shift_ref[...])
    hb = h.astype(bf16)
    pt = lax.dot_general(wnt_ref[...], hb, _NT, preferred_element_type=f32)
    kt_ref[...] = pt[:D_ATT]
    vt_ref[...] = pt[D_ATT:2 * D_ATT]
    ktb_ref[...] = pt[:D_ATT].astype(bf16)
    vtb_ref[...] = pt[D_ATT:2 * D_ATT].astype(bf16)
    lf = _log_sigmoid(pt[2 * D_ATT:2 * D_ATT + ATT_HEADS] + bf_ref[...])
    lft_ref[...] = lf
    p = jnp.dot(hb, wnn_ref[...], preferred_element_type=f32)
    zatt_ref[...] = p[:, D_ATT:2 * D_ATT]
    u_ref[...] = p[:, 2 * D_ATT:2 * D_ATT + D_SSM]
    zssm_ref[...] = p[:, 2 * D_ATT + D_SSM:]

    c = lf
    lane = lax.broadcasted_iota(jnp.int32, c.shape, 1)
    d = 1
    while d < tm:
        c = c + jnp.where(lane >= d, pltpu.roll(c, d, axis=1), 0.0)
        d *= 2
    c = c + jnp.concatenate([carry[...]] * (tm // LANES), axis=1)
    carry[...] = jnp.broadcast_to(c[:, tm - 1:tm], carry.shape)
    c = c * LOG2E
    cum_ref[...] = c
    for j, part in enumerate(_split3(c)):
        for hd in range(ATT_HEADS):
            parts[3 * hd + j:3 * hd + j + 1, :] = part[hd:hd + 1, :]
    cq_parts = parts[...].T
    lane_q = lax.broadcasted_iota(jnp.int32, (tm, LANES), 1)
    ones_lanes = jnp.where(lane_q < AUG + 3, 1.0, 0.0)
    for pr in range(ATT_HEADS // 2):
        qp = p[:, pr * LANES:(pr + 1) * LANES] * (QK_SCALE * LOG2E)
        for hh in range(2):
            hd = 2 * pr + hh
            qh = qp if hh == 0 else pltpu.roll(qp, HEAD_DIM, axis=1)
            cq = pltpu.roll(cq_parts, AUG + 3 - 3 * hd, axis=1)
            ext = jnp.where((lane_q >= AUG + 3) & (lane_q < AUG + 6), cq, ones_lanes)
            qa_ref[hd] = jnp.where(lane_q < HEAD_DIM, qh, ext).astype(bf16)


def _pre_prompt_call(x, shift, scale, g_norm, w_nn, w_nt, b_f, tm):
    n, l, d = x.shape
    row = lambda n_, i: (n_, i, 0)
    col = lambda n_, i: (n_, 0, i)
    const2 = lambda n_, i: (0, 0)
    out_shape = [
        jax.ShapeDtypeStruct((n, ATT_HEADS, l, LANES), bf16),
        jax.ShapeDtypeStruct((n, l, D_ATT), f32),
        jax.ShapeDtypeStruct((n, l, D_SSM), f32),
        jax.ShapeDtypeStruct((n, l, D_SSM), f32),
        jax.ShapeDtypeStruct((n, D_ATT, l), f32),
        jax.ShapeDtypeStruct((n, D_ATT, l), f32),
        jax.ShapeDtypeStruct((n, ATT_HEADS, l), f32),
        jax.ShapeDtypeStruct((n, D_ATT, l), bf16),
        jax.ShapeDtypeStruct((n, D_ATT, l), bf16),
        jax.ShapeDtypeStruct((n, ATT_HEADS, l), f32),
    ]
    out_specs = [
        pl.BlockSpec((None, ATT_HEADS, tm, LANES), lambda n_, i: (n_, 0, i, 0)),
        pl.BlockSpec((None, tm, D_ATT), row),
        pl.BlockSpec((None, tm, D_SSM), row),
        pl.BlockSpec((None, tm, D_SSM), row),
        pl.BlockSpec((None, D_ATT, tm), col),
        pl.BlockSpec((None, D_ATT, tm), col),
        pl.BlockSpec((None, ATT_HEADS, tm), col),
        pl.BlockSpec((None, D_ATT, tm), col),
        pl.BlockSpec((None, D_ATT, tm), col),
        pl.BlockSpec((None, ATT_HEADS, tm), col),
    ]
    in_specs = [
        pl.BlockSpec((None, tm, d), row),
        pl.BlockSpec((None, 1, d), lambda n_, i: (n_, 0, 0)),
        pl.BlockSpec((None, 1, d), lambda n_, i: (n_, 0, 0)),
        pl.BlockSpec((1, d), const2),
        pl.BlockSpec(w_nn.shape, const2, pipeline_mode=pl.Buffered(1)),
        pl.BlockSpec(w_nt.shape, const2, pipeline_mode=pl.Buffered(1)),
        pl.BlockSpec((ATT_HEADS, tm), const2),
    ]
    return pl.pallas_call(
        _pre_prompt_kernel, grid=(n, l // tm), in_specs=in_specs, out_specs=out_specs,
        out_shape=out_shape,
        scratch_shapes=[pltpu.VMEM((ATT_HEADS, LANES), f32), pltpu.VMEM((LANES, tm), f32)],
        compiler_params=_cparams(("arbitrary", "arbitrary")), name="pre",
    )(x, shift, scale, g_norm, w_nn, w_nt, jnp.broadcast_to(b_f[:, None], (ATT_HEADS, tm)))


def _pre_sample_kernel(x_ref, shift_ref, scale_ref, g_ref, wnn_ref, wnt_ref, bf_ref, seg_ref,
                       q_ref, k_ref, v_ref, zatt_ref, u_ref, zssm_ref, lf_ref, snew_ref):
    hb = _modulated_norm(x_ref[...], g_ref[...], scale_ref[...], shift_ref[...]).astype(bf16)
    p = jnp.dot(hb, wnn_ref[...], preferred_element_type=f32)
    pk = lax.dot_general(hb, wnt_ref[...], _NT, preferred_element_type=f32)
    q = p[:, :D_ATT] * QK_SCALE
    k = pk[:, :D_ATT]
    q_ref[...] = q
    k_ref[...] = k
    v_ref[...] = pk[:, D_ATT:2 * D_ATT]
    zatt_ref[...] = p[:, D_ATT:2 * D_ATT]
    u_ref[...] = p[:, 2 * D_ATT:2 * D_ATT + D_SSM]
    zssm_ref[...] = p[:, 2 * D_ATT + D_SSM:]
    lf_ref[...] = _log_sigmoid(pk[:, 2 * D_ATT:] + bf_ref[...])
    snew_ref[...] = jnp.dot(q * k, seg_ref[...], preferred_element_type=f32, precision=_HI)


def _pre_sample_call(x, shift, scale, g_norm, w_nn, w_nt, b_f_pad, seg):
    r, d = x.shape
    shp = lambda c: jax.ShapeDtypeStruct((r, c), f32)
    return pl.pallas_call(
        _pre_sample_kernel,
        out_shape=[shp(D_ATT), shp(D_ATT), shp(D_ATT), shp(D_ATT), shp(D_SSM), shp(D_SSM),
                   shp(NT_ROWS - 2 * D_ATT), shp(LANES)],
        compiler_params=pltpu.CompilerParams(vmem_limit_bytes=VMEM_LIMIT), name="pre_s",
    )(x, shift, scale, g_norm, w_nn, w_nt, b_f_pad, seg)


def _att_kernel(pt_ref, qa_ref, kt_ref, vt_ref, cum_ref, z_ref,
                snew_ref, lfnew_ref, vnew_ref, tri_ref, qb_hbm, kc_hbm, vc_hbm, lfc_hbm,
                o_ref, od_ref,
                ka, va, m_s, acc_s,
                kbuf, vbuf, lfbuf, qbuf, sc, bias, tot, p_s, dacc, stat, ksem, vsem, lfsem, qsem,
                *, tq, seq, n_tok, n_pages, ppu, ring):
    n = pl.program_id(0)
    qi = pl.program_id(1)
    n_tiles = seq // tq
    units = n_pages // ppu
    ahead = ring - 1
    k_end = n_tok * units
    v_lo, v_hi = units, (n_tok + 1) * units

    def start_pages(src, buf, sem, tok, sub, slot):
        for j in range(ppu):
            pltpu.make_async_copy(src.at[pt_ref[tok, sub * ppu + j]], buf.at[slot, j], sem.at[slot]).start()

    def wait_pages(src, buf, sem, slot):
        for j in range(ppu):
            pltpu.make_async_copy(src.at[0], buf.at[slot, j], sem.at[slot]).wait()

    def request(w):
        tok, sub, slot = w // units, w % units, w % ring

        @pl.when(w < k_end)
        def _():
            start_pages(kc_hbm, kbuf, ksem, tok, sub, slot)

        @pl.when((w >= v_lo) & (w < v_hi))
        def _():
            start_pages(vc_hbm, vbuf, vsem, tok - 1, sub, slot)

    def start_token_inputs(t):
        slot = t % 2

        def body(pg, carry):
            pltpu.make_async_copy(lfc_hbm.at[pt_ref[t, pg]], lfbuf.at[slot, pg], lfsem.at[slot]).start()
            return carry
        lax.fori_loop(0, n_pages, body, 0, unroll=min(8, n_pages))
        pltpu.make_async_copy(qb_hbm.at[t], qbuf.at[slot], qsem.at[slot]).start()

    def new_token(t):
        slot = t % 2

        @pl.when(t + 1 < n_tok)
        def _():
            start_token_inputs(t + 1)

        def wait_lf(pg, carry):
            pltpu.make_async_copy(lfc_hbm.at[0], lfbuf.at[slot, pg], lfsem.at[slot]).wait()
            return carry
        lax.fori_loop(0, n_pages, wait_lf, 0, unroll=min(8, n_pages))
        pltpu.make_async_copy(qb_hbm.at[0], qbuf.at[slot], qsem.at[slot]).wait()
        lf2 = lfbuf[slot].reshape(n_pages * ATT_HEADS, LANES)
        sums = sum(jnp.dot(part.astype(bf16), tri_ref[...], preferred_element_type=f32) for part in _split3(lf2))
        bias[slot] = sums[:, :LANES].reshape(n_pages, ATT_HEADS, LANES)
        tot[...] = sums[:, LANES:].reshape(n_pages, ATT_HEADS, LANES)
        lfnew = lfnew_ref[t]

        def suffix(i, run):
            pg = n_pages - 1 - i
            bias[slot, pg] = bias[slot, pg] + run + lfnew
            return run + tot[pg]
        lax.fori_loop(0, n_pages, suffix, jnp.zeros((ATT_HEADS, LANES), f32))

    def token_softmax(t):
        s_all = sc[t % 2] + bias[t % 2]
        snew = snew_ref[t]
        m = jnp.max(jnp.max(s_all, axis=0), axis=-1, keepdims=True)
        m = jnp.maximum(m, snew)
        p_all = jnp.exp(s_all - m[None])
        p_s[...] = p_all
        p_new = jnp.exp(snew - m)
        stat[0] = p_new
        stat[1] = jnp.sum(jnp.sum(p_all, axis=0), axis=-1, keepdims=True) + p_new
        dacc[...] = jnp.zeros_like(dacc)

    def token_output(t):
        ones_row = jnp.ones((ATT_HEADS, LANES), bf16)
        head_row = lax.broadcasted_iota(jnp.int32, (ATT_HEADS, HEAD_DIM), 0)
        past = jnp.zeros((ATT_HEADS, HEAD_DIM), f32)
        for h in range(ATT_HEADS):
            r = sum(lax.dot_general(ones_row, part.astype(bf16), _NT, preferred_element_type=f32)
                    for part in _split3(dacc[h]))
            past = jnp.where(head_row == h, r, past)
        od_ref[t] = (past + stat[0][:, :HEAD_DIM] * vnew_ref[t]) * (1.0 / stat[1][:, :HEAD_DIM])

    def unit_prologue(w):
        t = w // units

        @pl.when(w % units == 0)
        def _():
            @pl.when((t >= 2) & (t <= n_tok + 1))
            def _():
                token_output(t - 2)

            @pl.when((t >= 1) & (t <= n_tok))
            def _():
                token_softmax(t - 1)

            @pl.when(t < n_tok)
            def _():
                new_token(t)

        request(w + ahead)

        @pl.when(w < k_end)
        def _():
            wait_pages(kc_hbm, kbuf, ksem, w % ring)

        @pl.when((w >= v_lo) & (w < v_hi))
        def _():
            wait_pages(vc_hbm, vbuf, vsem, w % ring)

    def unit_compute(w, h, ready):
        tok, sub, slot = w // units, w % units, w % ring
        qh, a = qbuf[jnp.minimum(tok, n_tok - 1) % 2, h], dacc[h]
        if ready is not None:
            qh, a = jnp.where(ready, qh, 0.0), jnp.where(ready, a, 0.0)
        for j in range(ppu):
            sc[tok % 2, sub * ppu + j, pl.ds(h, 1), :] = jnp.sum(kbuf[slot, j, h] * qh, axis=0, keepdims=True)
            pr = p_s[sub * ppu + j, pl.ds(h, 1), :]
            a = a + vbuf[slot, j, h] * jnp.broadcast_to(pr, (HEAD_DIM, LANES))
        dacc[h] = a

    @pl.when((n == 0) & (qi == 0))
    def _():
        vbuf[...] = jnp.zeros_like(vbuf)
        p_s[...] = jnp.zeros_like(p_s)
        dacc[...] = jnp.zeros_like(dacc)
        start_token_inputs(jnp.int32(0))
        for w0 in range(ahead):
            request(jnp.int32(w0))

    @pl.when(qi == 0)
    def _():
        c = cum_ref[...]
        r16 = lax.broadcasted_iota(jnp.int32, (16, seq), 0)
        for h in range(ATT_HEADS):
            ka[h, 0:HEAD_DIM, :] = kt_ref[h * HEAD_DIM:(h + 1) * HEAD_DIM, :]
            va[h, 0:HEAD_DIM, :] = vt_ref[h * HEAD_DIM:(h + 1) * HEAD_DIM, :]
            hi, mid, lo = _split3(c[h:h + 1, :])
            ext = jnp.where(r16 == 0, -hi, jnp.where(r16 == 1, -mid, jnp.where(r16 == 2, -lo,
                            jnp.where(r16 < 6, 1.0, 0.0))))
            ka[h, AUG:AUG + 16, :] = ext.astype(bf16)
            ka[h, AUG + 16:, :] = jnp.zeros((LANES - AUG - 16, seq), bf16)
            va[h, HEAD_DIM:, :] = jnp.ones((LANES - HEAD_DIM, seq), bf16)

    lane_q = lax.broadcasted_iota(jnp.int32, (tq, LANES), 1)
    row = lax.broadcasted_iota(jnp.int32, (tq, tq), 0)
    col = lax.broadcasted_iota(jnp.int32, (tq, tq), 1)
    causal = col <= row

    first_unit = n * (n_tiles * (n_tiles + 1) // 2) + (qi * (qi + 1)) // 2

    def block(kb, diagonal):
        w = first_unit if diagonal else first_unit + 1 + kb
        unit_prologue(w)
        k0 = pl.multiple_of(kb * tq, tq)
        for h in range(ATT_HEADS):
            s = jnp.dot(qa_ref[h], ka[h, :, pl.ds(k0, tq)], preferred_element_type=f32)
            if diagonal:
                s = jnp.where(causal, s, -jnp.inf)
            rowmax = jnp.broadcast_to(jnp.max(s, axis=-1, keepdims=True), (tq, LANES))
            m_new = rowmax if diagonal else jnp.maximum(m_s[h], rowmax)
            p = [jnp.exp2(s[:, j * LANES:(j + 1) * LANES] - m_new) for j in range(tq // LANES)]
            p = jnp.concatenate(p, axis=1)
            pv = lax.dot_general(p.astype(bf16), va[h, :, pl.ds(k0, tq)], _NT, preferred_element_type=f32)
            acc_s[h] = pv if diagonal else jnp.exp2(m_s[h] - m_new) * acc_s[h] + pv
            m_s[h] = m_new
            unit_compute(w, h, m_new[0:HEAD_DIM, :] < jnp.inf)

    def off_diagonal(kb, carry):
        block(kb, False)
        return carry

    block(qi, True)
    lax.fori_loop(0, qi, off_diagonal, 0)

    for pr in range(ATT_HEADS // 2):
        a0, a1 = acc_s[2 * pr], acc_s[2 * pr + 1]
        o0 = a0 * pltpu.roll(1.0 / a0, HEAD_DIM, axis=1)
        o1 = pltpu.roll(a1, HEAD_DIM, axis=1) * (1.0 / a1)
        o = jnp.where(lane_q < HEAD_DIM, o0, o1)
        cols = slice(pr * LANES, (pr + 1) * LANES)
        o_ref[:, cols] = (o * _silu(z_ref[:, cols])).astype(bf16)


def _att_call(qa, ktb, vtb, cum, z_att, page_table, qb, snew_b, lfnew_b, vnew, kc, vc, lfc, tq, ppu, ring):
    n, _, l, _ = qa.shape
    ns, n_pages = page_table.shape
    n_tiles = l // tq
    units = n_pages // ppu
    assert n * (n_tiles * (n_tiles + 1) // 2) >= (ns + 1) * units + 1, "too few key-block steps for the page units"
    lane = jnp.arange(LANES)
    tri = jnp.concatenate([lane[:, None] > lane[None, :], jnp.ones((LANES, LANES), bool)], axis=1).astype(bf16)
    row = lambda n_, i, pt: (n_, i, 0)
    whole = lambda n_, i, pt: (n_, 0, 0)
    const = lambda nd: (lambda n_, i, pt: (0,) * nd)
    page = (ppu, ATT_HEADS, HEAD_DIM, LANES)
    grid_spec = pltpu.PrefetchScalarGridSpec(
        num_scalar_prefetch=1,
        grid=(n, n_tiles),
        in_specs=[pl.BlockSpec((None, ATT_HEADS, tq, LANES), lambda n_, i, pt: (n_, 0, i, 0)),
                  pl.BlockSpec((None, D_ATT, l), whole),
                  pl.BlockSpec((None, D_ATT, l), whole),
                  pl.BlockSpec((None, ATT_HEADS, l), whole),
                  pl.BlockSpec((None, tq, D_ATT), row),
                  pl.BlockSpec(snew_b.shape, const(3)),
                  pl.BlockSpec(lfnew_b.shape, const(3)),
                  pl.BlockSpec(vnew.shape, const(3)),
                  pl.BlockSpec((LANES, 2 * LANES), const(2)),
                  pl.BlockSpec(memory_space=pl.ANY),
                  pl.BlockSpec(memory_space=pl.ANY),
                  pl.BlockSpec(memory_space=pl.ANY),
                  pl.BlockSpec(memory_space=pl.ANY)],
        out_specs=[pl.BlockSpec((None, tq, D_ATT), row),
                   pl.BlockSpec((ns, ATT_HEADS, HEAD_DIM), const(3))],
        scratch_shapes=[pltpu.VMEM((ATT_HEADS, LANES, l), bf16), pltpu.VMEM((ATT_HEADS, LANES, l), bf16),
                        pltpu.VMEM((ATT_HEADS, tq, LANES), f32),
                        pltpu.VMEM((ATT_HEADS, tq, LANES), f32),
                        pltpu.VMEM((ring,) + page, f32),
                        pltpu.VMEM((ring,) + page, f32),
                        pltpu.VMEM((2, n_pages, ATT_HEADS, LANES), f32),
                        pltpu.VMEM((2, ATT_HEADS, HEAD_DIM, LANES), f32),
                        pltpu.VMEM((2, n_pages, ATT_HEADS, LANES), f32),
                        pltpu.VMEM((2, n_pages, ATT_HEADS, LANES), f32),
                        pltpu.VMEM((n_pages, ATT_HEADS, LANES), f32),
                        pltpu.VMEM((n_pages, ATT_HEADS, LANES), f32),
                        pltpu.VMEM((ATT_HEADS, HEAD_DIM, LANES), f32),
                        pltpu.VMEM((2, ATT_HEADS, LANES), f32),
                        pltpu.SemaphoreType.DMA((ring,)), pltpu.SemaphoreType.DMA((ring,)),
                        pltpu.SemaphoreType.DMA((2,)), pltpu.SemaphoreType.DMA((2,))],
    )
    return pl.pallas_call(
        functools.partial(_att_kernel, tq=tq, seq=l, n_tok=ns, n_pages=n_pages, ppu=ppu, ring=ring),
        grid_spec=grid_spec,
        out_shape=[jax.ShapeDtypeStruct((n, l, D_ATT), bf16),
                   jax.ShapeDtypeStruct((ns, ATT_HEADS, HEAD_DIM), f32)],
        compiler_params=_cparams(("arbitrary", "arbitrary")), name="att",
    )(page_table, qa, ktb, vtb, cum, z_att, snew_b, lfnew_b, vnew, tri, qb, kc, vc, lfc)


def _post_math(x, att, y, z_ssm, gate, w_glu, b_glu, w_out, g_final):
    g = jax.nn.gelu(y)
    glu = jax.nn.sigmoid(jnp.dot(g.astype(bf16), w_glu, preferred_element_type=f32) + b_glu)
    ssm = g * glu * _silu(z_ssm)
    mix = jnp.concatenate([att.astype(bf16), ssm.astype(bf16)], axis=-1)
    xo = x + gate * jnp.dot(mix, w_out, preferred_element_type=f32)
    return xo * lax.rsqrt(jnp.mean(xo * xo, axis=-1, keepdims=True) + EPS) * g_final


def _ssm_kernel(*refs, nb, tl, time_major, chunk, with_post):
    u_ref, h0r_ref, h0i_ref, ar_ref, ai_ref, bre_ref, bim_ref, cre_ref, cim_ref, d_ref = refs[:10]
    if with_post:
        x_ref, att_ref, zssm_ref, gate_ref, wglu_ref, bglu_ref, wout_ref, gf_ref = refs[10:18]
        refs = refs[18:]
    else:
        refs = refs[10:]
    y_ref, hr_ref, hi_ref, utn, bu_re, bu_im, hs_re, hs_im, st_re, st_im, yscr, perm = refs[:12]
    i = pl.program_id(0)
    nj = D_SSM // LANES
    last = pl.num_programs(0) - (2 if with_post else 1)

    @pl.when(i == 0)
    def _():
        st_re[...] = h0r_ref[...]
        st_im[...] = h0i_ref[...]

    if with_post:
        y_prev = refs[12]

        @pl.when(i == 0)
        def _():
            y_prev[...] = jnp.zeros_like(y_prev)

    if time_major:
        utn[...] = u_ref[...].reshape(tl * nb, D_SSM)
    else:
        for j in range(nj):
            cols = slice(j * LANES, (j + 1) * LANES)
            for b in range(nb):
                perm[j, b * tl:(b + 1) * tl, :] = u_ref[b, :, cols]
            for t in range(tl):
                utn[t * nb:(t + 1) * nb, cols] = perm[j, pl.ds(t, nb, stride=tl), :]

    for c in range(nj):
        cols = slice(c * chunk, (c + 1) * chunk)
        ub = utn[:, c * LANES:(c + 1) * LANES].astype(bf16)
        bu_re[:, cols] = jnp.dot(ub, bre_ref[c], preferred_element_type=f32)
        bu_im[:, cols] = jnp.dot(ub, bim_ref[c], preferred_element_type=f32)

    if with_post:
        rows, d = nb * tl, x_ref.shape[-1]
        gate = jnp.broadcast_to(gate_ref[...], (nb, tl, d)).reshape(rows, d)
        out = _post_math(x_ref[...].reshape(rows, d), att_ref[...].reshape(rows, D_ATT), y_prev[...],
                         zssm_ref[...].reshape(rows, D_SSM), gate, wglu_ref[...], bglu_ref[...],
                         wout_ref[...], gf_ref[...])
        y_ref[...] = out.reshape(nb, tl, d)

    for c in range(nj):
        cols = slice(c * chunk, (c + 1) * chunk)
        ucols = slice(c * LANES, (c + 1) * LANES)
        ar = jnp.broadcast_to(ar_ref[:, cols], (nb, chunk))
        ai = jnp.broadcast_to(ai_ref[:, cols], (nb, chunk))
        hr, hi = st_re[:, cols], st_im[:, cols]
        for t in range(tl):
            rows = slice(t * nb, (t + 1) * nb)
            hr, hi = (ar * hr - ai * hi + bu_re[rows, cols], ar * hi + ai * hr + bu_im[rows, cols])
            hs_re[rows, cols] = hr.astype(bf16)
            hs_im[rows, cols] = hi.astype(bf16)
        st_re[:, cols] = hr
        st_im[:, cols] = hi
        yscr[:, ucols] = (jnp.dot(hs_re[:, cols], cre_ref[c], preferred_element_type=f32)
                          - jnp.dot(hs_im[:, cols], cim_ref[c], preferred_element_type=f32)
                          + d_ref[:, ucols] * utn[:, ucols])
    if time_major:
        y_ref[...] = yscr[...].reshape(tl, nb, D_SSM)
    else:
        for j in range(nj):
            cols = slice(j * LANES, (j + 1) * LANES)
            for t in range(tl):
                perm[j, pl.ds(t, nb, stride=tl), :] = yscr[t * nb:(t + 1) * nb, cols]
            if with_post:
                y_prev[:, cols] = perm[j]
            else:
                for b in range(nb):
                    y_ref[b, :, cols] = perm[j, b * tl:(b + 1) * tl, :]

    @pl.when(i == last)
    def _():
        hr_ref[...] = st_re[...]
        hi_ref[...] = st_im[...]


def _ssm_call(u, h0_re, h0_im, ab_re, ab_im, b_re, b_im, c_re, c_im, d_skip, tl, time_major, post=None):
    nj = D_SSM // LANES
    if time_major:
        l, nb = u.shape[0], u.shape[1]
        block = lambda w: pl.BlockSpec((tl, nb, w), lambda i: (i, 0, 0))
    else:
        nb, l = u.shape[0], u.shape[1]
        block = lambda w: pl.BlockSpec((nb, tl, w), lambda i: (0, i, 0))
    rows = tl * nb
    steps = l // tl
    full = lambda a: pl.BlockSpec(a.shape, lambda i, nd=a.ndim: (0,) * nd)
    ins = [u, h0_re, h0_im, ab_re, ab_im, b_re, b_im, c_re, c_im, d_skip]
    in_specs = [block(D_SSM)] + [full(a) for a in ins[1:]]
    out_spec, out_width, extra_scratch = block(D_SSM), D_SSM, []
    if post is not None:
        assert not time_major
        x, att, z_ssm = post[:3]
        out_width = x.shape[-1]
        behind = lambda w: pl.BlockSpec((nb, tl, w), lambda i: (0, jnp.maximum(i - 1, 0), 0))
        in_specs[0] = pl.BlockSpec((nb, tl, D_SSM), lambda i, last_block=steps - 1: (0, jnp.minimum(i, last_block), 0))
        ins += list(post)
        in_specs += [behind(out_width), behind(D_ATT), behind(D_SSM)] + [full(a) for a in post[3:]]
        out_spec = behind(out_width)
        extra_scratch = [pltpu.VMEM((rows, D_SSM), f32)]
        steps += 1
    return pl.pallas_call(
        functools.partial(_ssm_kernel, nb=nb, tl=tl, time_major=time_major, chunk=N_STATE // nj,
                          with_post=post is not None),
        grid=(steps,),
        in_specs=in_specs,
        out_specs=[out_spec, full(h0_re), full(h0_im)],
        out_shape=[jax.ShapeDtypeStruct(u.shape[:2] + (out_width,), f32),
                   jax.ShapeDtypeStruct(h0_re.shape, f32), jax.ShapeDtypeStruct(h0_im.shape, f32)],
        scratch_shapes=[pltpu.VMEM((rows, D_SSM), f32),
                        pltpu.VMEM((rows, N_STATE), f32), pltpu.VMEM((rows, N_STATE), f32),
                        pltpu.VMEM((rows, N_STATE), bf16), pltpu.VMEM((rows, N_STATE), bf16),
                        pltpu.VMEM((nb, N_STATE), f32), pltpu.VMEM((nb, N_STATE), f32),
                        pltpu.VMEM((rows, D_SSM), f32),
                        pltpu.VMEM((nj, rows, LANES), f32)] + extra_scratch,
        compiler_params=_cparams(("arbitrary",)), name="ssm",
    )(*ins)


def _post_kernel(*refs, gate_att):
    if gate_att:
        x_ref, att_ref, zatt_ref, y_ref, zssm_ref, gate_ref, wglu_ref, bglu_ref, wout_ref, gf_ref, o_ref = refs
        att = att_ref[...] * _silu(zatt_ref[...])
    else:
        x_ref, att_ref, y_ref, zssm_ref, gate_ref, wglu_ref, bglu_ref, wout_ref, gf_ref, o_ref = refs
        att = att_ref[...]
    o_ref[...] = _post_math(x_ref[...], att, y_ref[...], zssm_ref[...], gate_ref[...], wglu_ref[...], bglu_ref[...],
                            wout_ref[...], gf_ref[...])


def _post_call(x, att, z_att, y, z_ssm, gate, w_glu, b_glu, w_out, g_final, tm):
    n, l, d = x.shape
    row = lambda n_, i: (n_, i, 0)
    const2 = lambda n_, i: (0, 0)
    gate_spec = (pl.BlockSpec((None, 1, d), lambda n_, i: (n_, 0, 0)) if gate.shape[1] == 1
                 else pl.BlockSpec((None, tm, d), row))
    ins = [x, att] + ([z_att] if z_att is not None else []) + [y, z_ssm, gate, w_glu, b_glu, w_out, g_final]
    in_specs = ([pl.BlockSpec((None, tm, d), row), pl.BlockSpec((None, tm, D_ATT), row)]
                + ([pl.BlockSpec((None, tm, D_ATT), row)] if z_att is not None else [])
                + [pl.BlockSpec((None, tm, D_SSM), row),
                   pl.BlockSpec((None, tm, D_SSM), row), gate_spec,
                   pl.BlockSpec(w_glu.shape, const2), pl.BlockSpec((1, D_SSM), const2),
                   pl.BlockSpec(w_out.shape, const2), pl.BlockSpec((1, d), const2)])
    return pl.pallas_call(
        functools.partial(_post_kernel, gate_att=z_att is not None),
        grid=(n, l // tm), in_specs=in_specs,
        out_specs=pl.BlockSpec((None, tm, d), row),
        out_shape=jax.ShapeDtypeStruct((n, l, d), f32),
        compiler_params=_cparams(("arbitrary", "arbitrary")), name="post",
    )(*ins)


def kernel(x_prompt, x_sample, c_prompt, c_sample, cache_k, cache_v, cache_logf, state_ssm_re, state_ssm_im,
           page_table, g_norm, w_ada, b_ada, w_in, b_fgate, a_re, a_im, log_dt, b_re, b_im, c_re, c_im, d_skip,
           w_glu, b_glu, w_out, g_final):
    n, l, d = x_prompt.shape
    ns = x_sample.shape[0]
    depth = w_in.shape[0]
    assert depth == 1 and x_sample.shape[1] == 1
    lyr = 0
    xs = x_sample.reshape(ns, d)

    splits = [0, D_ATT, 2 * D_ATT, 3 * D_ATT, 4 * D_ATT, 4 * D_ATT + ATT_HEADS,
              4 * D_ATT + ATT_HEADS + D_SSM, 4 * D_ATT + ATT_HEADS + 2 * D_SSM]
    w_t = jnp.swapaxes(w_in[lyr], 0, 1)
    wq, wk, wv, wz, wf, wu, wzs = [w_t[splits[i]:splits[i + 1]] for i in range(7)]
    w_nn = jnp.concatenate([wq, wz, wu, wzs], axis=0).T.astype(bf16)
    w_nt = jnp.concatenate([wk, wv, wf, jnp.zeros((NT_ROWS - 2 * D_ATT - ATT_HEADS, d), f32)], axis=0).astype(bf16)
    b_f = b_fgate[lyr]
    b_f_pad = jnp.concatenate([b_f, jnp.zeros((NT_ROWS - 2 * D_ATT - ATT_HEADS,), f32)])[None, :]
    seg = (jnp.arange(D_ATT)[:, None] // HEAD_DIM == jnp.arange(LANES)[None, :]).astype(f32)
    gn = g_norm[lyr][None, :]
    gf = g_final[None, :]
    w_glu_b = w_glu[lyr].astype(bf16)
    w_out_b = w_out[lyr].astype(bf16)
    b_glu2 = b_glu[lyr][None, :]

    mod = _ada_call(jnp.concatenate([c_prompt, c_sample], axis=0), w_ada[lyr], b_ada[lyr][None, :])
    shift_p, scale_p, gate_p = [mod[:n, k * d:(k + 1) * d].reshape(n, 1, d) for k in range(3)]
    shift_s, scale_s, gate_s = [mod[n:, k * d:(k + 1) * d] for k in range(3)]

    ab_re, ab_im, bb_re, bb_im = _ssm_param_call(a_re[lyr], a_im[lyr], log_dt[lyr], b_re[lyr], b_im[lyr])
    ab_re_row = ab_re.reshape(1, N_STATE)
    ab_im_row = ab_im.reshape(1, N_STATE)
    slabs = D_SSM // LANES
    slab_diag = lambda w: _block_diag(w.reshape((slabs, SSM_GROUPS // slabs) + w.shape[1:]))
    bd_re = slab_diag(bb_re).astype(bf16)
    bd_im = slab_diag(bb_im).astype(bf16)
    cd_re = slab_diag(jnp.swapaxes(c_re[lyr], 1, 2)).astype(bf16)
    cd_im = slab_diag(jnp.swapaxes(c_im[lyr], 1, 2)).astype(bf16)
    d_row = d_skip[lyr].reshape(1, D_SSM)

    n_pages = page_table.shape[1]
    tiles = _tiles(l, n_pages)
    qa, z_att, u_p, z_ssm, kt, vt, lft, ktb, vtb, cum = _pre_prompt_call(x_prompt, shift_p, scale_p, gn, w_nn, w_nt,
                                                                         b_f, tiles["tm"])
    qs, k_s, v_s, zatt_s, u_s, zssm_s, lf_s, snew = _pre_sample_call(xs, shift_s, scale_s, gn, w_nn, w_nt,
                                                                     b_f_pad, seg)

    qb = jnp.broadcast_to(qs.reshape(ns, ATT_HEADS, HEAD_DIM, 1), (ns, ATT_HEADS, HEAD_DIM, LANES))
    snew_b = jnp.broadcast_to(snew[:, :ATT_HEADS, None], (ns, ATT_HEADS, LANES))
    lfnew_b = jnp.broadcast_to(lf_s[:, :ATT_HEADS, None], (ns, ATT_HEADS, LANES))
    kc = jnp.transpose(cache_k[lyr], (0, 2, 3, 1))
    vc = jnp.transpose(cache_v[lyr], (0, 2, 3, 1))
    lfc = jnp.transpose(cache_logf[lyr], (0, 2, 1))
    att, att_s = _att_call(qa, ktb, vtb, cum, z_att, page_table, qb, snew_b, lfnew_b,
                           v_s.reshape(ns, ATT_HEADS, HEAD_DIM), kc, vc, lfc,
                           tq=tiles["tq"], ppu=tiles["ppu"], ring=tiles["ring"])

    zeros_state = jnp.zeros((n, N_STATE), f32)
    y_prompt, hp_re, hp_im = _ssm_call(u_p, zeros_state, zeros_state, ab_re_row, ab_im_row, bd_re, bd_im, cd_re, cd_im,
                                       d_row, tl=tiles["tl"], time_major=False,
                                       post=(x_prompt, att, z_ssm, gate_p, w_glu_b, b_glu2, w_out_b, gf))

    heads_t = lambda t: jnp.transpose(t.reshape(n, ATT_HEADS, HEAD_DIM, l), (0, 3, 1, 2))[None]
    k_prompt = heads_t(kt)
    v_prompt = heads_t(vt)
    logf_prompt = jnp.transpose(lft, (0, 2, 1))[None]

    h0r = state_ssm_re[lyr].reshape(ns, N_STATE)
    h0i = state_ssm_im[lyr].reshape(ns, N_STATE)
    y_s, hs_re, hs_im = _ssm_call(u_s[None], h0r, h0i, ab_re_row, ab_im_row, bd_re, bd_im, cd_re, cd_im, d_row,
                                  tl=1, time_major=True)
    y_sample = _post_call(xs[None], att_s.reshape(1, ns, D_ATT), zatt_s[None], y_s, zssm_s[None], gate_s[None],
                          w_glu_b, b_glu2, w_out_b, gf, ns)

    st = lambda a, b: a.reshape(1, b, SSM_GROUPS, SSM_STATE)
    return (y_prompt, y_sample.reshape(ns, 1, d),
            k_prompt, v_prompt, logf_prompt, st(hp_re, n), st(hp_im, n),
            k_s.reshape(1, ns, 1, ATT_HEADS, HEAD_DIM), v_s.reshape(1, ns, 1, ATT_HEADS, HEAD_DIM),
            lf_s[:, :ATT_HEADS].reshape(1, ns, 1, ATT_HEADS), st(hs_re, ns), st(hs_im, ns))
```

```python
import functools
import math

import jax
import jax.numpy as jnp
from jax import lax
from jax.experimental import pallas as pl
from jax.experimental.pallas import tpu as pltpu

f32 = jnp.float32
bf16 = jnp.bfloat16

HEAD_DIM = 64
ATT_HEADS = 8
D_ATT = ATT_HEADS * HEAD_DIM
SSM_GROUP = 16
SSM_GROUPS = 32
SSM_STATE = 64
D_SSM = SSM_GROUP * SSM_GROUPS
N_STATE = SSM_GROUPS * SSM_STATE
EPS = 1e-6
QK_SCALE = HEAD_DIM ** -0.5
LOG2E = 1.0 / math.log(2.0)
LANES = 128
NT_ROWS = 2 * D_ATT + 16
VMEM_LIMIT = 56 * 1024 * 1024

_NT = (((1,), (1,)), ((), ()))
_HI = lax.Precision.HIGHEST


def _silu(x):
    return x * jax.nn.sigmoid(x)


def _log_sigmoid(x):
    return jnp.minimum(x, 0.0) - jnp.log1p(jnp.exp(-jnp.abs(x)))


def _modulated_norm(x, g, scale, shift):
    y = x * lax.rsqrt(jnp.mean(x * x, axis=-1, keepdims=True) + EPS)
    return (y * g) * (1.0 + scale) + shift


def _cparams(sem, vmem=VMEM_LIMIT):
    return pltpu.CompilerParams(dimension_semantics=sem, vmem_limit_bytes=vmem)


def _tiles(seq, n_pages):
    return dict(tm=min(1024, seq), tq=min(256, seq), tl=min(32, seq), ppu=min(8, n_pages // 2), ring=5)


def _ada_kernel(c_ref, w_ref, b_ref, o_ref):
    s = _silu(c_ref[...]).astype(bf16)
    o_ref[...] = jnp.dot(s, w_ref[...].astype(bf16), preferred_element_type=f32) + b_ref[...]


def _ada_call(c_all, w_ada, b_ada):
    r, d = c_all.shape
    n_out = w_ada.shape[1]
    tn = 768
    return pl.pallas_call(
        _ada_kernel,
        grid=(n_out // tn,),
        in_specs=[pl.BlockSpec((r, d), lambda j: (0, 0)),
                  pl.BlockSpec((d, tn), lambda j: (0, j)),
                  pl.BlockSpec((1, tn), lambda j: (0, j))],
        out_specs=pl.BlockSpec((r, tn), lambda j: (0, j)),
        out_shape=jax.ShapeDtypeStruct((r, n_out), f32),
        compiler_params=_cparams(("arbitrary",)),
        name="ada",
    )(c_all, w_ada, b_ada)


def _ssm_param_kernel(are_ref, aim_ref, ldt_ref, bre_ref, bim_ref,
                      abr_ref, abi_ref, bbr_ref, bbi_ref):
    lr, li = are_ref[...], aim_ref[...]
    dt = jnp.exp(ldt_ref[...])
    mag = jnp.exp(lr * dt)
    abr = mag * jnp.cos(li * dt)
    abi = mag * jnp.sin(li * dt)
    xr, xi = abr - 1.0, abi
    den = lr * lr + li * li
    cr = (xr * lr + xi * li) / den
    ci = (xi * lr - xr * li) / den
    br, bi = bre_ref[...], bim_ref[...]
    abr_ref[...] = abr
    abi_ref[...] = abi
    bbr_ref[...] = cr * br - ci * bi
    bbi_ref[...] = cr * bi + ci * br


def _ssm_param_call(a_re, a_im, log_dt, b_re, b_im):
    g, p = a_re.shape
    hc = b_re.shape[-1]
    rep = lambda a: jnp.broadcast_to(a[:, None, :], (g, hc, p)).reshape(g * hc, p)
    ldt = jnp.broadcast_to(log_dt[:, None, None], (g, hc, p)).reshape(g * hc, p)
    bt = lambda b: jnp.swapaxes(b, 1, 2).reshape(g * hc, p)
    shp = jax.ShapeDtypeStruct((g * hc, p), f32)
    abr, abi, bbr, bbi = pl.pallas_call(
        _ssm_param_kernel, out_shape=[shp] * 4, name="ssm_par",
    )(rep(a_re), rep(a_im), ldt, bt(b_re), bt(b_im))
    first = lambda a: a.reshape(g, hc, p)[:, 0, :]
    return first(abr), first(abi), bbr.reshape(g, hc, p), bbi.reshape(g, hc, p)


def _block_diag(blocks):
    *lead, g, r, c = blocks.shape
    eye = jnp.eye(g, dtype=blocks.dtype)
    return (blocks[..., :, :, None, :] * eye[:, None, :, None]).reshape(*lead, g * r, g * c)


def _split3(x):
    hi = x.astype(bf16).astype(f32)
    r = x - hi
    mid = r.astype(bf16).astype(f32)
    return hi, mid, r - mid


AUG = HEAD_DIM


def _pre_prompt_kernel(x_ref, shift_ref, scale_ref, g_ref, wnn_ref, wnt_ref, bf_ref,
                       qa_ref, zatt_ref, u_ref, zssm_ref, kt_ref, vt_ref, lft_ref, ktb_ref, vtb_ref, cum_ref,
                       carry, parts):
    i = pl.program_id(1)
    tm = x_ref.shape[0]

    @pl.when(i == 0)
    def _():
        carry[...] = jnp.zeros_like(carry)

    @pl.when((pl.program_id(0) == 0) & (i == 0))
    def _():
        parts[...] = jnp.zeros_like(parts)

    h = _modulated_norm(x_ref[...], g_ref[...], scale_ref[...], shift_ref[...])
    hb = h.astype(bf16)
    pt = lax.dot_general(wnt_ref[...], hb, _NT, preferred_element_type=f32)
    kt_ref[...] = pt[:D_ATT]
    vt_ref[...] = pt[D_ATT:2 * D_ATT]
    ktb_ref[...] = pt[:D_ATT].astype(bf16)
    vtb_ref[...] = pt[D_ATT:2 * D_ATT].astype(bf16)
    lf = _log_sigmoid(pt[2 * D_ATT:2 * D_ATT + ATT_HEADS] + bf_ref[...])
    lft_ref[...] = lf
    p = jnp.dot(hb, wnn_ref[...], preferred_element_type=f32)
    zatt_ref[...] = p[:, D_ATT:2 * D_ATT]
    u_ref[...] = p[:, 2 * D_ATT:2 * D_ATT + D_SSM]
    zssm_ref[...] = p[:, 2 * D_ATT + D_SSM:]

    c = lf
    lane = lax.broadcasted_iota(jnp.int32, c.shape, 1)
    d = 1
    while d < tm:
        c = c + jnp.where(lane >= d, pltpu.roll(c, d, axis=1), 0.0)
        d *= 2
    c = c + jnp.concatenate([carry[...]] * (tm // LANES), axis=1)
    carry[...] = jnp.broadcast_to(c[:, tm - 1:tm], carry.shape)
    c = c * LOG2E
    cum_ref[...] = c
    for j, part in enumerate(_split3(c)):
        for hd in range(ATT_HEADS):
            parts[3 * hd + j:3 * hd + j + 1, :] = part[hd:hd + 1, :]
    cq_parts = parts[...].T
    lane_q = lax.broadcasted_iota(jnp.int32, (tm, LANES), 1)
    ones_lanes = jnp.where(lane_q < AUG + 3, 1.0, 0.0)
    for pr in range(ATT_HEADS // 2):
        qp = p[:, pr * LANES:(pr + 1) * LANES] * (QK_SCALE * LOG2E)
        for hh in range(2):
            hd = 2 * pr + hh
            qh = qp if hh == 0 else pltpu.roll(qp, HEAD_DIM, axis=1)
            cq = pltpu.roll(cq_parts, AUG + 3 - 3 * hd, axis=1)
            ext = jnp.where((lane_q >= AUG + 3) & (lane_q < AUG + 6), cq, ones_lanes)
            qa_ref[hd] = jnp.where(lane_q < HEAD_DIM, qh, ext).astype(bf16)


def _pre_prompt_call(x, shift, scale, g_norm, w_nn, w_nt, b_f, tm):
    n, l, d = x.shape
    row = lambda n_, i: (n_, i, 0)
    col = lambda n_, i: (n_, 0, i)
    const2 = lambda n_, i: (0, 0)
    out_shape = [
        jax.ShapeDtypeStruct((n, ATT_HEADS, l, LANES), bf16),
        jax.ShapeDtypeStruct((n, l, D_ATT), f32),
        jax.ShapeDtypeStruct((n, l, D_SSM), f32),
        jax.ShapeDtypeStruct((n, l, D_SSM), f32),
        jax.ShapeDtypeStruct((n, D_ATT, l), f32),
        jax.ShapeDtypeStruct((n, D_ATT, l), f32),
        jax.ShapeDtypeStruct((n, ATT_HEADS, l), f32),
        jax.ShapeDtypeStruct((n, D_ATT, l), bf16),
        jax.ShapeDtypeStruct((n, D_ATT, l), bf16),
        jax.ShapeDtypeStruct((n, ATT_HEADS, l), f32),
    ]
    out_specs = [
        pl.BlockSpec((None, ATT_HEADS, tm, LANES), lambda n_, i: (n_, 0, i, 0)),
        pl.BlockSpec((None, tm, D_ATT), row),
        pl.BlockSpec((None, tm, D_SSM), row),
        pl.BlockSpec((None, tm, D_SSM), row),
        pl.BlockSpec((None, D_ATT, tm), col),
        pl.BlockSpec((None, D_ATT, tm), col),
        pl.BlockSpec((None, ATT_HEADS, tm), col),
        pl.BlockSpec((None, D_ATT, tm), col),
        pl.BlockSpec((None, D_ATT, tm), col),
        pl.BlockSpec((None, ATT_HEADS, tm), col),
    ]
    in_specs = [
        pl.BlockSpec((None, tm, d), row),
        pl.BlockSpec((None, 1, d), lambda n_, i: (n_, 0, 0)),
        pl.BlockSpec((None, 1, d), lambda n_, i: (n_, 0, 0)),
        pl.BlockSpec((1, d), const2),
        pl.BlockSpec(w_nn.shape, const2, pipeline_mode=pl.Buffered(1)),
        pl.BlockSpec(w_nt.shape, const2, pipeline_mode=pl.Buffered(1)),
        pl.BlockSpec((ATT_HEADS, tm), const2),
    ]
    return pl.pallas_call(
        _pre_prompt_kernel, grid=(n, l // tm), in_specs=in_specs, out_specs=out_specs,
        out_shape=out_shape,
        scratch_shapes=[pltpu.VMEM((ATT_HEADS, LANES), f32), pltpu.VMEM((LANES, tm), f32)],
        compiler_params=_cparams(("arbitrary", "arbitrary")), name="pre",
    )(x, shift, scale, g_norm, w_nn, w_nt, jnp.broadcast_to(b_f[:, None], (ATT_HEADS, tm)))


def _pre_sample_kernel(x_ref, shift_ref, scale_ref, g_ref, wnn_ref, wnt_ref, bf_ref, seg_ref,
                       q_ref, k_ref, v_ref, zatt_ref, u_ref, zssm_ref, lf_ref, snew_ref):
    hb = _modulated_norm(x_ref[...], g_ref[...], scale_ref[...], shift_ref[...]).astype(bf16)
    p = jnp.dot(hb, wnn_ref[...], preferred_element_type=f32)
    pk = lax.dot_general(hb, wnt_ref[...], _NT, preferred_element_type=f32)
    q = p[:, :D_ATT] * QK_SCALE
    k = pk[:, :D_ATT]
    q_ref[...] = q
    k_ref[...] = k
    v_ref[...] = pk[:, D_ATT:2 * D_ATT]
    zatt_ref[...] = p[:, D_ATT:2 * D_ATT]
    u_ref[...] = p[:, 2 * D_ATT:2 * D_ATT + D_SSM]
    zssm_ref[...] = p[:, 2 * D_ATT + D_SSM:]
    lf_ref[...] = _log_sigmoid(pk[:, 2 * D_ATT:] + bf_ref[...])
    snew_ref[...] = jnp.dot(q * k, seg_ref[...], preferred_element_type=f32, precision=_HI)


def _pre_sample_call(x, shift, scale, g_norm, w_nn, w_nt, b_f_pad, seg):
    r, d = x.shape
    shp = lambda c: jax.ShapeDtypeStruct((r, c), f32)
    return pl.pallas_call(
        _pre_sample_kernel,
        out_shape=[shp(D_ATT), shp(D_ATT), shp(D_ATT), shp(D_ATT), shp(D_SSM), shp(D_SSM),
                   shp(NT_ROWS - 2 * D_ATT), shp(LANES)],
        compiler_params=pltpu.CompilerParams(vmem_limit_bytes=VMEM_LIMIT), name="pre_s",
    )(x, shift, scale, g_norm, w_nn, w_nt, b_f_pad, seg)


def _att_kernel(pt_ref, qa_ref, kt_ref, vt_ref, cum_ref, z_ref,
                snew_ref, lfnew_ref, vnew_ref, tri_ref, qb_hbm, kc_hbm, vc_hbm, lfc_hbm,
                o_ref, od_ref,
                ka, va, m_s, acc_s,
                kbuf, vbuf, lfbuf, qbuf, sc, bias, tot, p_s, dacc, stat, ksem, vsem, lfsem, qsem,
                *, tq, seq, n_tok, n_pages, ppu, ring):
    n = pl.program_id(0)
    qi = pl.program_id(1)
    n_tiles = seq // tq
    units = n_pages // ppu
    ahead = ring - 1
    k_end = n_tok * units
    v_lo, v_hi = units, (n_tok + 1) * units

    def start_pages(src, buf, sem, tok, sub, slot):
        for j in range(ppu):
            pltpu.make_async_copy(src.at[pt_ref[tok, sub * ppu + j]], buf.at[slot, j], sem.at[slot]).start()

    def wait_pages(src, buf, sem, slot):
        for j in range(ppu):
            pltpu.make_async_copy(src.at[0], buf.at[slot, j], sem.at[slot]).wait()

    def request(w):
        tok, sub, slot = w // units, w % units, w % ring

        @pl.when(w < k_end)
        def _():
            start_pages(kc_hbm, kbuf, ksem, tok, sub, slot)

        @pl.when((w >= v_lo) & (w < v_hi))
        def _():
            start_pages(vc_hbm, vbuf, vsem, tok - 1, sub, slot)

    def start_token_inputs(t):
        slot = t % 2

        def body(pg, carry):
            pltpu.make_async_copy(lfc_hbm.at[pt_ref[t, pg]], lfbuf.at[slot, pg], lfsem.at[slot]).start()
            return carry
        lax.fori_loop(0, n_pages, body, 0, unroll=min(16, n_pages))
        pltpu.make_async_copy(qb_hbm.at[t], qbuf.at[slot], qsem.at[slot]).start()

    def new_token(t):
        slot = t % 2

        @pl.when(t + 1 < n_tok)
        def _():
            start_token_inputs(t + 1)

        def wait_lf(pg, carry):
            pltpu.make_async_copy(lfc_hbm.at[0], lfbuf.at[slot, pg], lfsem.at[slot]).wait()
            return carry
        lax.fori_loop(0, n_pages, wait_lf, 0, unroll=min(16, n_pages))
        pltpu.make_async_copy(qb_hbm.at[0], qbuf.at[slot], qsem.at[slot]).wait()
        lf2 = lfbuf[slot].reshape(n_pages * ATT_HEADS, LANES)
        sums = sum(jnp.dot(part.astype(bf16), tri_ref[...], preferred_element_type=f32) for part in _split3(lf2))
        bias[slot] = sums[:, :LANES].reshape(n_pages, ATT_HEADS, LANES)
        tot[...] = sums[:, LANES:].reshape(n_pages, ATT_HEADS, LANES)
        lfnew = lfnew_ref[t]

        def suffix(i, run):
            pg = n_pages - 1 - i
            bias[slot, pg] = bias[slot, pg] + run + lfnew
            return run + tot[pg]
        lax.fori_loop(0, n_pages, suffix, jnp.zeros((ATT_HEADS, LANES), f32))

    def token_softmax(t):
        s_all = sc[t % 2] + bias[t % 2]
        snew = snew_ref[t]
        m = jnp.max(jnp.max(s_all, axis=0), axis=-1, keepdims=True)
        m = jnp.maximum(m, snew)
        p_all = jnp.exp(s_all - m[None])
        p_s[...] = p_all
        p_new = jnp.exp(snew - m)
        stat[0] = p_new
        stat[1] = jnp.sum(jnp.sum(p_all, axis=0), axis=-1, keepdims=True) + p_new
        dacc[...] = jnp.zeros_like(dacc)

    def token_output(t):
        ones_row = jnp.ones((ATT_HEADS, LANES), bf16)
        head_row = lax.broadcasted_iota(jnp.int32, (ATT_HEADS, HEAD_DIM), 0)
        past = jnp.zeros((ATT_HEADS, HEAD_DIM), f32)
        for h in range(ATT_HEADS):
            r = sum(lax.dot_general(ones_row, part.astype(bf16), _NT, preferred_element_type=f32)
                    for part in _split3(dacc[h]))
            past = jnp.where(head_row == h, r, past)
        od_ref[t] = (past + stat[0][:, :HEAD_DIM] * vnew_ref[t]) * (1.0 / stat[1][:, :HEAD_DIM])

    def unit_prologue(w):
        t = w // units

        @pl.when(w % units == 0)
        def _():
            @pl.when((t >= 2) & (t <= n_tok + 1))
            def _():
                token_output(t - 2)

            @pl.when((t >= 1) & (t <= n_tok))
            def _():
                token_softmax(t - 1)

            @pl.when(t < n_tok)
            def _():
                new_token(t)

        request(w + ahead)

        @pl.when(w < k_end)
        def _():
            wait_pages(kc_hbm, kbuf, ksem, w % ring)

        @pl.when((w >= v_lo) & (w < v_hi))
        def _():
            wait_pages(vc_hbm, vbuf, vsem, w % ring)

    def unit_compute(w, h, ready):
        tok, sub, slot = w // units, w % units, w % ring
        qh, a = qbuf[jnp.minimum(tok, n_tok - 1) % 2, h], dacc[h]
        if ready is not None:
            qh, a = jnp.where(ready, qh, 0.0), jnp.where(ready, a, 0.0)
        for j in range(ppu):
            sc[tok % 2, sub * ppu + j, pl.ds(h, 1), :] = jnp.sum(kbuf[slot, j, h] * qh, axis=0, keepdims=True)
            pr = p_s[sub * ppu + j, pl.ds(h, 1), :]
            a = a + vbuf[slot, j, h] * jnp.broadcast_to(pr, (HEAD_DIM, LANES))
        dacc[h] = a

    @pl.when((n == 0) & (qi == 0))
    def _():
        vbuf[...] = jnp.zeros_like(vbuf)
        p_s[...] = jnp.zeros_like(p_s)
        dacc[...] = jnp.zeros_like(dacc)
        start_token_inputs(jnp.int32(0))
        for w0 in range(ahead):
            request(jnp.int32(w0))

    @pl.when(qi == 0)
    def _():
        c = cum_ref[...]
        r16 = lax.broadcasted_iota(jnp.int32, (16, seq), 0)
        for h in range(ATT_HEADS):
            ka[h, 0:HEAD_DIM, :] = kt_ref[h * HEAD_DIM:(h + 1) * HEAD_DIM, :]
            va[h, 0:HEAD_DIM, :] = vt_ref[h * HEAD_DIM:(h + 1) * HEAD_DIM, :]
            hi, mid, lo = _split3(c[h:h + 1, :])
            ext = jnp.where(r16 == 0, -hi, jnp.where(r16 == 1, -mid, jnp.where(r16 == 2, -lo,
                            jnp.where(r16 < 6, 1.0, 0.0))))
            ka[h, AUG:AUG + 16, :] = ext.astype(bf16)
            ka[h, AUG + 16:, :] = jnp.zeros((LANES - AUG - 16, seq), bf16)
            va[h, HEAD_DIM:, :] = jnp.ones((LANES - HEAD_DIM, seq), bf16)

    lane_q = lax.broadcasted_iota(jnp.int32, (tq, LANES), 1)
    row = lax.broadcasted_iota(jnp.int32, (tq, tq), 0)
    col = lax.broadcasted_iota(jnp.int32, (tq, tq), 1)
    causal = col <= row

    first_unit = n * (n_tiles * (n_tiles + 1) // 2) + (qi * (qi + 1)) // 2

    def block(kb, diagonal):
        w = first_unit if diagonal else first_unit + 1 + kb
        unit_prologue(w)
        k0 = pl.multiple_of(kb * tq, tq)
        for h in range(ATT_HEADS):
            s = jnp.dot(qa_ref[h], ka[h, :, pl.ds(k0, tq)], preferred_element_type=f32)
            if diagonal:
                s = jnp.where(causal, s, -jnp.inf)
            rowmax = jnp.broadcast_to(jnp.max(s, axis=-1, keepdims=True), (tq, LANES))
            m_new = rowmax if diagonal else jnp.maximum(m_s[h], rowmax)
            p = [jnp.exp2(s[:, j * LANES:(j + 1) * LANES] - m_new) for j in range(tq // LANES)]
            p = jnp.concatenate(p, axis=1)
            pv = lax.dot_general(p.astype(bf16), va[h, :, pl.ds(k0, tq)], _NT, preferred_element_type=f32)
            acc_s[h] = pv if diagonal else jnp.exp2(m_s[h] - m_new) * acc_s[h] + pv
            m_s[h] = m_new
            unit_compute(w, h, m_new[0:HEAD_DIM, :] < jnp.inf)

    def off_diagonal(kb, carry):
        block(kb, False)
        return carry

    block(qi, True)
    lax.fori_loop(0, qi, off_diagonal, 0)

    for pr in range(ATT_HEADS // 2):
        a0, a1 = acc_s[2 * pr], acc_s[2 * pr + 1]
        o0 = a0 * pltpu.roll(1.0 / a0, HEAD_DIM, axis=1)
        o1 = pltpu.roll(a1, HEAD_DIM, axis=1) * (1.0 / a1)
        o = jnp.where(lane_q < HEAD_DIM, o0, o1)
        cols = slice(pr * LANES, (pr + 1) * LANES)
        o_ref[:, cols] = (o * _silu(z_ref[:, cols])).astype(bf16)


def _att_call(qa, ktb, vtb, cum, z_att, page_table, qb, snew_b, lfnew_b, vnew, kc, vc, lfc, tq, ppu, ring):
    n, _, l, _ = qa.shape
    ns, n_pages = page_table.shape
    n_tiles = l // tq
    units = n_pages // ppu
    assert n * (n_tiles * (n_tiles + 1) // 2) >= (ns + 1) * units + 1, "too few key-block steps for the page units"
    lane = jnp.arange(LANES)
    tri = jnp.concatenate([lane[:, None] > lane[None, :], jnp.ones((LANES, LANES), bool)], axis=1).astype(bf16)
    row = lambda n_, i, pt: (n_, i, 0)
    whole = lambda n_, i, pt: (n_, 0, 0)
    const = lambda nd: (lambda n_, i, pt: (0,) * nd)
    page = (ppu, ATT_HEADS, HEAD_DIM, LANES)
    grid_spec = pltpu.PrefetchScalarGridSpec(
        num_scalar_prefetch=1,
        grid=(n, n_tiles),
        in_specs=[pl.BlockSpec((None, ATT_HEADS, tq, LANES), lambda n_, i, pt: (n_, 0, i, 0)),
                  pl.BlockSpec((None, D_ATT, l), whole),
                  pl.BlockSpec((None, D_ATT, l), whole),
                  pl.BlockSpec((None, ATT_HEADS, l), whole),
                  pl.BlockSpec((None, tq, D_ATT), row),
                  pl.BlockSpec(snew_b.shape, const(3)),
                  pl.BlockSpec(lfnew_b.shape, const(3)),
                  pl.BlockSpec(vnew.shape, const(3)),
                  pl.BlockSpec((LANES, 2 * LANES), const(2)),
                  pl.BlockSpec(memory_space=pl.ANY),
                  pl.BlockSpec(memory_space=pl.ANY),
                  pl.BlockSpec(memory_space=pl.ANY),
                  pl.BlockSpec(memory_space=pl.ANY)],
        out_specs=[pl.BlockSpec((None, tq, D_ATT), row),
                   pl.BlockSpec((ns, ATT_HEADS, HEAD_DIM), const(3))],
        scratch_shapes=[pltpu.VMEM((ATT_HEADS, LANES, l), bf16), pltpu.VMEM((ATT_HEADS, LANES, l), bf16),
                        pltpu.VMEM((ATT_HEADS, tq, LANES), f32),
                        pltpu.VMEM((ATT_HEADS, tq, LANES), f32),
                        pltpu.VMEM((ring,) + page, f32),
                        pltpu.VMEM((ring,) + page, f32),
                        pltpu.VMEM((2, n_pages, ATT_HEADS, LANES), f32),
                        pltpu.VMEM((2, ATT_HEADS, HEAD_DIM, LANES), f32),
                        pltpu.VMEM((2, n_pages, ATT_HEADS, LANES), f32),
                        pltpu.VMEM((2, n_pages, ATT_HEADS, LANES), f32),
                        pltpu.VMEM((n_pages, ATT_HEADS, LANES), f32),
                        pltpu.VMEM((n_pages, ATT_HEADS, LANES), f32),
                        pltpu.VMEM((ATT_HEADS, HEAD_DIM, LANES), f32),
                        pltpu.VMEM((2, ATT_HEADS, LANES), f32),
                        pltpu.SemaphoreType.DMA((ring,)), pltpu.SemaphoreType.DMA((ring,)),
                        pltpu.SemaphoreType.DMA((2,)), pltpu.SemaphoreType.DMA((2,))],
    )
    return pl.pallas_call(
        functools.partial(_att_kernel, tq=tq, seq=l, n_tok=ns, n_pages=n_pages, ppu=ppu, ring=ring),
        grid_spec=grid_spec,
        out_shape=[jax.ShapeDtypeStruct((n, l, D_ATT), bf16),
                   jax.ShapeDtypeStruct((ns, ATT_HEADS, HEAD_DIM), f32)],
        compiler_params=_cparams(("arbitrary", "arbitrary")), name="att",
    )(page_table, qa, ktb, vtb, cum, z_att, snew_b, lfnew_b, vnew, tri, qb, kc, vc, lfc)


def _post_math(x, att, y, z_ssm, gate, w_glu, b_glu, w_out, g_final):
    g = jax.nn.gelu(y)
    glu = jax.nn.sigmoid(jnp.dot(g.astype(bf16), w_glu, preferred_element_type=f32) + b_glu)
    ssm = g * glu * _silu(z_ssm)
    mix = jnp.concatenate([att.astype(bf16), ssm.astype(bf16)], axis=-1)
    xo = x + gate * jnp.dot(mix, w_out, preferred_element_type=f32)
    return xo * lax.rsqrt(jnp.mean(xo * xo, axis=-1, keepdims=True) + EPS) * g_final


def _ssm_kernel(*refs, nb, tl, time_major, chunk, with_post):
    u_ref, h0r_ref, h0i_ref, ar_ref, ai_ref, bre_ref, bim_ref, cre_ref, cim_ref, d_ref = refs[:10]
    if with_post:
        x_ref, att_ref, zssm_ref, gate_ref, wglu_ref, bglu_ref, wout_ref, gf_ref = refs[10:18]
        refs = refs[18:]
    else:
        refs = refs[10:]
    y_ref, hr_ref, hi_ref, utn, bu_re, bu_im, hs_re, hs_im, st_re, st_im, yscr, perm = refs[:12]
    i = pl.program_id(0)
    nj = D_SSM // LANES
    last = pl.num_programs(0) - (2 if with_post else 1)

    @pl.when(i == 0)
    def _():
        st_re[...] = h0r_ref[...]
        st_im[...] = h0i_ref[...]

    if with_post:
        y_prev = refs[12]

        @pl.when(i == 0)
        def _():
            y_prev[...] = jnp.zeros_like(y_prev)

    if time_major:
        utn[...] = u_ref[...].reshape(tl * nb, D_SSM)
    else:
        for j in range(nj):
            cols = slice(j * LANES, (j + 1) * LANES)
            for b in range(nb):
                perm[j, b * tl:(b + 1) * tl, :] = u_ref[b, :, cols]
            for t in range(tl):
                utn[t * nb:(t + 1) * nb, cols] = perm[j, pl.ds(t, nb, stride=tl), :]

    for c in range(nj):
        cols = slice(c * chunk, (c + 1) * chunk)
        ub = utn[:, c * LANES:(c + 1) * LANES].astype(bf16)
        bu_re[:, cols] = jnp.dot(ub, bre_ref[c], preferred_element_type=f32)
        bu_im[:, cols] = jnp.dot(ub, bim_ref[c], preferred_element_type=f32)

    if with_post:
        rows, d = nb * tl, x_ref.shape[-1]
        gate = jnp.broadcast_to(gate_ref[...], (nb, tl, d)).reshape(rows, d)
        out = _post_math(x_ref[...].reshape(rows, d), att_ref[...].reshape(rows, D_ATT), y_prev[...],
                         zssm_ref[...].reshape(rows, D_SSM), gate, wglu_ref[...], bglu_ref[...],
                         wout_ref[...], gf_ref[...])
        y_ref[...] = out.reshape(nb, tl, d)

    for c in range(nj):
        cols = slice(c * chunk, (c + 1) * chunk)
        ucols = slice(c * LANES, (c + 1) * LANES)
        ar = jnp.broadcast_to(ar_ref[:, cols], (nb, chunk))
        ai = jnp.broadcast_to(ai_ref[:, cols], (nb, chunk))
        hr, hi = st_re[:, cols], st_im[:, cols]
        for t in range(tl):
            rows = slice(t * nb, (t + 1) * nb)
            hr, hi = (ar * hr - ai * hi + bu_re[rows, cols], ar * hi + ai * hr + bu_im[rows, cols])
            hs_re[rows, cols] = hr.astype(bf16)
            hs_im[rows, cols] = hi.astype(bf16)
        st_re[:, cols] = hr
        st_im[:, cols] = hi
        yscr[:, ucols] = (jnp.dot(hs_re[:, cols], cre_ref[c], preferred_element_type=f32)
                          - jnp.dot(hs_im[:, cols], cim_ref[c], preferred_element_type=f32)
                          + d_ref[:, ucols] * utn[:, ucols])
    if time_major:
        y_ref[...] = yscr[...].reshape(tl, nb, D_SSM)
    else:
        for j in range(nj):
            cols = slice(j * LANES, (j + 1) * LANES)
            for t in range(tl):
                perm[j, pl.ds(t, nb, stride=tl), :] = yscr[t * nb:(t + 1) * nb, cols]
            if with_post:
                y_prev[:, cols] = perm[j]
            else:
                for b in range(nb):
                    y_ref[b, :, cols] = perm[j, b * tl:(b + 1) * tl, :]

    @pl.when(i == last)
    def _():
        hr_ref[...] = st_re[...]
        hi_ref[...] = st_im[...]


def _ssm_call(u, h0_re, h0_im, ab_re, ab_im, b_re, b_im, c_re, c_im, d_skip, tl, time_major, post=None):
    nj = D_SSM // LANES
    if time_major:
        l, nb = u.shape[0], u.shape[1]
        block = lambda w: pl.BlockSpec((tl, nb, w), lambda i: (i, 0, 0))
    else:
        nb, l = u.shape[0], u.shape[1]
        block = lambda w: pl.BlockSpec((nb, tl, w), lambda i: (0, i, 0))
    rows = tl * nb
    steps = l // tl
    full = lambda a: pl.BlockSpec(a.shape, lambda i, nd=a.ndim: (0,) * nd)
    ins = [u, h0_re, h0_im, ab_re, ab_im, b_re, b_im, c_re, c_im, d_skip]
    in_specs = [block(D_SSM)] + [full(a) for a in ins[1:]]
    out_spec, out_width, extra_scratch = block(D_SSM), D_SSM, []
    if post is not None:
        assert not time_major
        x, att, z_ssm = post[:3]
        out_width = x.shape[-1]
        behind = lambda w: pl.BlockSpec((nb, tl, w), lambda i: (0, jnp.maximum(i - 1, 0), 0))
        in_specs[0] = pl.BlockSpec((nb, tl, D_SSM), lambda i, last_block=steps - 1: (0, jnp.minimum(i, last_block), 0))
        ins += list(post)
        in_specs += [behind(out_width), behind(D_ATT), behind(D_SSM)] + [full(a) for a in post[3:]]
        out_spec = behind(out_width)
        extra_scratch = [pltpu.VMEM((rows, D_SSM), f32)]
        steps += 1
    return pl.pallas_call(
        functools.partial(_ssm_kernel, nb=nb, tl=tl, time_major=time_major, chunk=N_STATE // nj,
                          with_post=post is not None),
        grid=(steps,),
        in_specs=in_specs,
        out_specs=[out_spec, full(h0_re), full(h0_im)],
        out_shape=[jax.ShapeDtypeStruct(u.shape[:2] + (out_width,), f32),
                   jax.ShapeDtypeStruct(h0_re.shape, f32), jax.ShapeDtypeStruct(h0_im.shape, f32)],
        scratch_shapes=[pltpu.VMEM((rows, D_SSM), f32),
                        pltpu.VMEM((rows, N_STATE), f32), pltpu.VMEM((rows, N_STATE), f32),
                        pltpu.VMEM((rows, N_STATE), bf16), pltpu.VMEM((rows, N_STATE), bf16),
                        pltpu.VMEM((nb, N_STATE), f32), pltpu.VMEM((nb, N_STATE), f32),
                        pltpu.VMEM((rows, D_SSM), f32),
                        pltpu.VMEM((nj, rows, LANES), f32)] + extra_scratch,
        compiler_params=_cparams(("arbitrary",)), name="ssm",
    )(*ins)


def _post_kernel(*refs, gate_att):
    if gate_att:
        x_ref, att_ref, zatt_ref, y_ref, zssm_ref, gate_ref, wglu_ref, bglu_ref, wout_ref, gf_ref, o_ref = refs
        att = att_ref[...] * _silu(zatt_ref[...])
    else:
        x_ref, att_ref, y_ref, zssm_ref, gate_ref, wglu_ref, bglu_ref, wout_ref, gf_ref, o_ref = refs
        att = att_ref[...]
    o_ref[...] = _post_math(x_ref[...], att, y_ref[...], zssm_ref[...], gate_ref[...], wglu_ref[...], bglu_ref[...],
                            wout_ref[...], gf_ref[...])


def _post_call(x, att, z_att, y, z_ssm, gate, w_glu, b_glu, w_out, g_final, tm):
    n, l, d = x.shape
    row = lambda n_, i: (n_, i, 0)
    const2 = lambda n_, i: (0, 0)
    gate_spec = (pl.BlockSpec((None, 1, d), lambda n_, i: (n_, 0, 0)) if gate.shape[1] == 1
                 else pl.BlockSpec((None, tm, d), row))
    ins = [x, att] + ([z_att] if z_att is not None else []) + [y, z_ssm, gate, w_glu, b_glu, w_out, g_final]
    in_specs = ([pl.BlockSpec((None, tm, d), row), pl.BlockSpec((None, tm, D_ATT), row)]
                + ([pl.BlockSpec((None, tm, D_ATT), row)] if z_att is not None else [])
                + [pl.BlockSpec((None, tm, D_SSM), row),
                   pl.BlockSpec((None, tm, D_SSM), row), gate_spec,
                   pl.BlockSpec(w_glu.shape, const2), pl.BlockSpec((1, D_SSM), const2),
                   pl.BlockSpec(w_out.shape, const2), pl.BlockSpec((1, d), const2)])
    return pl.pallas_call(
        functools.partial(_post_kernel, gate_att=z_att is not None),
        grid=(n, l // tm), in_specs=in_specs,
        out_specs=pl.BlockSpec((None, tm, d), row),
        out_shape=jax.ShapeDtypeStruct((n, l, d), f32),
        compiler_params=_cparams(("arbitrary", "arbitrary")), name="post",
    )(*ins)


def kernel(x_prompt, x_sample, c_prompt, c_sample, cache_k, cache_v, cache_logf, state_ssm_re, state_ssm_im,
           page_table, g_norm, w_ada, b_ada, w_in, b_fgate, a_re, a_im, log_dt, b_re, b_im, c_re, c_im, d_skip,
           w_glu, b_glu, w_out, g_final):
    n, l, d = x_prompt.shape
    ns = x_sample.shape[0]
    depth = w_in.shape[0]
    assert depth == 1 and x_sample.shape[1] == 1
    lyr = 0
    xs = x_sample.reshape(ns, d)

    splits = [0, D_ATT, 2 * D_ATT, 3 * D_ATT, 4 * D_ATT, 4 * D_ATT + ATT_HEADS,
              4 * D_ATT + ATT_HEADS + D_SSM, 4 * D_ATT + ATT_HEADS + 2 * D_SSM]
    w_t = jnp.swapaxes(w_in[lyr], 0, 1)
    wq, wk, wv, wz, wf, wu, wzs = [w_t[splits[i]:splits[i + 1]] for i in range(7)]
    w_nn = jnp.concatenate([wq, wz, wu, wzs], axis=0).T.astype(bf16)
    w_nt = jnp.concatenate([wk, wv, wf, jnp.zeros((NT_ROWS - 2 * D_ATT - ATT_HEADS, d), f32)], axis=0).astype(bf16)
    b_f = b_fgate[lyr]
    b_f_pad = jnp.concatenate([b_f, jnp.zeros((NT_ROWS - 2 * D_ATT - ATT_HEADS,), f32)])[None, :]
    seg = (jnp.arange(D_ATT)[:, None] // HEAD_DIM == jnp.arange(LANES)[None, :]).astype(f32)
    gn = g_norm[lyr][None, :]
    gf = g_final[None, :]
    w_glu_b = w_glu[lyr].astype(bf16)
    w_out_b = w_out[lyr].astype(bf16)
    b_glu2 = b_glu[lyr][None, :]

    mod = _ada_call(jnp.concatenate([c_prompt, c_sample], axis=0), w_ada[lyr], b_ada[lyr][None, :])
    shift_p, scale_p, gate_p = [mod[:n, k * d:(k + 1) * d].reshape(n, 1, d) for k in range(3)]
    shift_s, scale_s, gate_s = [mod[n:, k * d:(k + 1) * d] for k in range(3)]

    ab_re, ab_im, bb_re, bb_im = _ssm_param_call(a_re[lyr], a_im[lyr], log_dt[lyr], b_re[lyr], b_im[lyr])
    ab_re_row = ab_re.reshape(1, N_STATE)
    ab_im_row = ab_im.reshape(1, N_STATE)
    slabs = D_SSM // LANES
    slab_diag = lambda w: _block_diag(w.reshape((slabs, SSM_GROUPS // slabs) + w.shape[1:]))
    bd_re = slab_diag(bb_re).astype(bf16)
    bd_im = slab_diag(bb_im).astype(bf16)
    cd_re = slab_diag(jnp.swapaxes(c_re[lyr], 1, 2)).astype(bf16)
    cd_im = slab_diag(jnp.swapaxes(c_im[lyr], 1, 2)).astype(bf16)
    d_row = d_skip[lyr].reshape(1, D_SSM)

    n_pages = page_table.shape[1]
    tiles = _tiles(l, n_pages)
    qa, z_att, u_p, z_ssm, kt, vt, lft, ktb, vtb, cum = _pre_prompt_call(x_prompt, shift_p, scale_p, gn, w_nn, w_nt,
                                                                         b_f, tiles["tm"])
    qs, k_s, v_s, zatt_s, u_s, zssm_s, lf_s, snew = _pre_sample_call(xs, shift_s, scale_s, gn, w_nn, w_nt,
                                                                     b_f_pad, seg)

    qb = jnp.broadcast_to(qs.reshape(ns, ATT_HEADS, HEAD_DIM, 1), (ns, ATT_HEADS, HEAD_DIM, LANES))
    snew_b = jnp.broadcast_to(snew[:, :ATT_HEADS, None], (ns, ATT_HEADS, LANES))
    lfnew_b = jnp.broadcast_to(lf_s[:, :ATT_HEADS, None], (ns, ATT_HEADS, LANES))
    kc = jnp.transpose(cache_k[lyr], (0, 2, 3, 1))
    vc = jnp.transpose(cache_v[lyr], (0, 2, 3, 1))
    lfc = jnp.transpose(cache_logf[lyr], (0, 2, 1))
    att, att_s = _att_call(qa, ktb, vtb, cum, z_att, page_table, qb, snew_b, lfnew_b,
                           v_s.reshape(ns, ATT_HEADS, HEAD_DIM), kc, vc, lfc,
                           tq=tiles["tq"], ppu=tiles["ppu"], ring=tiles["ring"])

    zeros_state = jnp.zeros((n, N_STATE), f32)
    y_prompt, hp_re, hp_im = _ssm_call(u_p, zeros_state, zeros_state, ab_re_row, ab_im_row, bd_re, bd_im, cd_re, cd_im,
                                       d_row, tl=tiles["tl"], time_major=False,
                                       post=(x_prompt, att, z_ssm, gate_p, w_glu_b, b_glu2, w_out_b, gf))

    heads_t = lambda t: jnp.transpose(t.reshape(n, ATT_HEADS, HEAD_DIM, l), (0, 3, 1, 2))[None]
    k_prompt = heads_t(kt)
    v_prompt = heads_t(vt)
    logf_prompt = jnp.transpose(lft, (0, 2, 1))[None]

    h0r = state_ssm_re[lyr].reshape(ns, N_STATE)
    h0i = state_ssm_im[lyr].reshape(ns, N_STATE)
    y_s, hs_re, hs_im = _ssm_call(u_s[None], h0r, h0i, ab_re_row, ab_im_row, bd_re, bd_im, cd_re, cd_im, d_row,
                                  tl=1, time_major=True)
    y_sample = _post_call(xs[None], att_s.reshape(1, ns, D_ATT), zatt_s[None], y_s, zssm_s[None], gate_s[None],
                          w_glu_b, b_glu2, w_out_b, gf, ns)

    st = lambda a, b: a.reshape(1, b, SSM_GROUPS, SSM_STATE)
    return (y_prompt, y_sample.reshape(ns, 1, d),
            k_prompt, v_prompt, logf_prompt, st(hp_re, n), st(hp_im, n),
            k_s.reshape(1, ns, 1, ATT_HEADS, HEAD_DIM), v_s.reshape(1, ns, 1, ATT_HEADS, HEAD_DIM),
            lf_s[:, :ATT_HEADS].reshape(1, ns, 1, ATT_HEADS), st(hs_re, ns), st(hs_im, ns))
```

```python
import functools
import math

import jax
import jax.numpy as jnp
from jax import lax
from jax.experimental import pallas as pl
from jax.experimental.pallas import tpu as pltpu

f32 = jnp.float32
bf16 = jnp.bfloat16

HEAD_DIM = 64
ATT_HEADS = 8
D_ATT = ATT_HEADS * HEAD_DIM
SSM_GROUP = 16
SSM_GROUPS = 32
SSM_STATE = 64
D_SSM = SSM_GROUP * SSM_GROUPS
N_STATE = SSM_GROUPS * SSM_STATE
EPS = 1e-6
QK_SCALE = HEAD_DIM ** -0.5
LOG2E = 1.0 / math.log(2.0)
LANES = 128
NT_ROWS = 2 * D_ATT + 16
VMEM_LIMIT = 56 * 1024 * 1024

_NT = (((1,), (1,)), ((), ()))
_HI = lax.Precision.HIGHEST


def _silu(x):
    return x * jax.nn.sigmoid(x)


def _log_sigmoid(x):
    return jnp.minimum(x, 0.0) - jnp.log1p(jnp.exp(-jnp.abs(x)))


def _modulated_norm(x, g, scale, shift):
    y = x * lax.rsqrt(jnp.mean(x * x, axis=-1, keepdims=True) + EPS)
    return (y * g) * (1.0 + scale) + shift


def _cparams(sem, vmem=VMEM_LIMIT):
    return pltpu.CompilerParams(dimension_semantics=sem, vmem_limit_bytes=vmem)


def _tiles(seq, n_pages):
    return dict(tm=min(1024, seq), tq=min(256, seq), tl=min(32, seq), ppu=min(8, n_pages // 2), ring=5)


def _ada_kernel(c_ref, w_ref, b_ref, o_ref):
    s = _silu(c_ref[...]).astype(bf16)
    o_ref[...] = jnp.dot(s, w_ref[...].astype(bf16), preferred_element_type=f32) + b_ref[...]


def _ada_call(c_all, w_ada, b_ada):
    r, d = c_all.shape
    n_out = w_ada.shape[1]
    tn = 768
    return pl.pallas_call(
        _ada_kernel,
        grid=(n_out // tn,),
        in_specs=[pl.BlockSpec((r, d), lambda j: (0, 0)),
                  pl.BlockSpec((d, tn), lambda j: (0, j)),
                  pl.BlockSpec((1, tn), lambda j: (0, j))],
        out_specs=pl.BlockSpec((r, tn), lambda j: (0, j)),
        out_shape=jax.ShapeDtypeStruct((r, n_out), f32),
        compiler_params=_cparams(("arbitrary",)),
        name="ada",
    )(c_all, w_ada, b_ada)


def _ssm_param_kernel(are_ref, aim_ref, ldt_ref, bre_ref, bim_ref,
                      abr_ref, abi_ref, bbr_ref, bbi_ref):
    lr, li = are_ref[...], aim_ref[...]
    dt = jnp.exp(ldt_ref[...])
    mag = jnp.exp(lr * dt)
    abr = mag * jnp.cos(li * dt)
    abi = mag * jnp.sin(li * dt)
    xr, xi = abr - 1.0, abi
    den = lr * lr + li * li
    cr = (xr * lr + xi * li) / den
    ci = (xi * lr - xr * li) / den
    br, bi = bre_ref[...], bim_ref[...]
    abr_ref[...] = abr
    abi_ref[...] = abi
    bbr_ref[...] = cr * br - ci * bi
    bbi_ref[...] = cr * bi + ci * br


def _ssm_param_call(a_re, a_im, log_dt, b_re, b_im):
    g, p = a_re.shape
    hc = b_re.shape[-1]
    rep = lambda a: jnp.broadcast_to(a[:, None, :], (g, hc, p)).reshape(g * hc, p)
    ldt = jnp.broadcast_to(log_dt[:, None, None], (g, hc, p)).reshape(g * hc, p)
    bt = lambda b: jnp.swapaxes(b, 1, 2).reshape(g * hc, p)
    shp = jax.ShapeDtypeStruct((g * hc, p), f32)
    abr, abi, bbr, bbi = pl.pallas_call(
        _ssm_param_kernel, out_shape=[shp] * 4, name="ssm_par",
    )(rep(a_re), rep(a_im), ldt, bt(b_re), bt(b_im))
    first = lambda a: a.reshape(g, hc, p)[:, 0, :]
    return first(abr), first(abi), bbr.reshape(g, hc, p), bbi.reshape(g, hc, p)


def _block_diag(blocks):
    *lead, g, r, c = blocks.shape
    eye = jnp.eye(g, dtype=blocks.dtype)
    return (blocks[..., :, :, None, :] * eye[:, None, :, None]).reshape(*lead, g * r, g * c)


def _split3(x):
    hi = x.astype(bf16).astype(f32)
    r = x - hi
    mid = r.astype(bf16).astype(f32)
    return hi, mid, r - mid


AUG = HEAD_DIM
PAGE_DMA_PRIORITY = 1


def _pre_prompt_kernel(x_ref, shift_ref, scale_ref, g_ref, wnn_ref, wnt_ref, bf_ref,
                       qa_ref, zatt_ref, u_ref, zssm_ref, kt_ref, vt_ref, lft_ref, ktb_ref, vtb_ref, cum_ref,
                       carry, parts):
    i = pl.program_id(1)
    tm = x_ref.shape[0]

    @pl.when(i == 0)
    def _():
        carry[...] = jnp.zeros_like(carry)

    @pl.when((pl.program_id(0) == 0) & (i == 0))
    def _():
        parts[...] = jnp.zeros_like(parts)

    h = _modulated_norm(x_ref[...], g_ref[...], scale_ref[...], shift_ref[...])
    hb = h.astype(bf16)
    pt = lax.dot_general(wnt_ref[...], hb, _NT, preferred_element_type=f32)
    kt_ref[...] = pt[:D_ATT]
    vt_ref[...] = pt[D_ATT:2 * D_ATT]
    ktb_ref[...] = pt[:D_ATT].astype(bf16)
    vtb_ref[...] = pt[D_ATT:2 * D_ATT].astype(bf16)
    lf = _log_sigmoid(pt[2 * D_ATT:2 * D_ATT + ATT_HEADS] + bf_ref[...])
    lft_ref[...] = lf
    p = jnp.dot(hb, wnn_ref[...], preferred_element_type=f32)
    zatt_ref[...] = p[:, D_ATT:2 * D_ATT]
    u_ref[...] = p[:, 2 * D_ATT:2 * D_ATT + D_SSM]
    zssm_ref[...] = p[:, 2 * D_ATT + D_SSM:]

    c = lf
    lane = lax.broadcasted_iota(jnp.int32, c.shape, 1)
    d = 1
    while d < tm:
        c = c + jnp.where(lane >= d, pltpu.roll(c, d, axis=1), 0.0)
        d *= 2
    c = c + jnp.concatenate([carry[...]] * (tm // LANES), axis=1)
    carry[...] = jnp.broadcast_to(c[:, tm - 1:tm], carry.shape)
    c = c * LOG2E
    cum_ref[...] = c
    for j, part in enumerate(_split3(c)):
        for hd in range(ATT_HEADS):
            parts[3 * hd + j:3 * hd + j + 1, :] = part[hd:hd + 1, :]
    cq_parts = parts[...].T
    lane_q = lax.broadcasted_iota(jnp.int32, (tm, LANES), 1)
    ones_lanes = jnp.where(lane_q < AUG + 3, 1.0, 0.0)
    for pr in range(ATT_HEADS // 2):
        qp = p[:, pr * LANES:(pr + 1) * LANES] * (QK_SCALE * LOG2E)
        for hh in range(2):
            hd = 2 * pr + hh
            qh = qp if hh == 0 else pltpu.roll(qp, HEAD_DIM, axis=1)
            cq = pltpu.roll(cq_parts, AUG + 3 - 3 * hd, axis=1)
            ext = jnp.where((lane_q >= AUG + 3) & (lane_q < AUG + 6), cq, ones_lanes)
            qa_ref[hd] = jnp.where(lane_q < HEAD_DIM, qh, ext).astype(bf16)


def _pre_prompt_call(x, shift, scale, g_norm, w_nn, w_nt, b_f, tm):
    n, l, d = x.shape
    row = lambda n_, i: (n_, i, 0)
    col = lambda n_, i: (n_, 0, i)
    const2 = lambda n_, i: (0, 0)
    out_shape = [
        jax.ShapeDtypeStruct((n, ATT_HEADS, l, LANES), bf16),
        jax.ShapeDtypeStruct((n, l, D_ATT), f32),
        jax.ShapeDtypeStruct((n, l, D_SSM), f32),
        jax.ShapeDtypeStruct((n, l, D_SSM), f32),
        jax.ShapeDtypeStruct((n, D_ATT, l), f32),
        jax.ShapeDtypeStruct((n, D_ATT, l), f32),
        jax.ShapeDtypeStruct((n, ATT_HEADS, l), f32),
        jax.ShapeDtypeStruct((n, D_ATT, l), bf16),
        jax.ShapeDtypeStruct((n, D_ATT, l), bf16),
        jax.ShapeDtypeStruct((n, ATT_HEADS, l), f32),
    ]
    out_specs = [
        pl.BlockSpec((None, ATT_HEADS, tm, LANES), lambda n_, i: (n_, 0, i, 0)),
        pl.BlockSpec((None, tm, D_ATT), row),
        pl.BlockSpec((None, tm, D_SSM), row),
        pl.BlockSpec((None, tm, D_SSM), row),
        pl.BlockSpec((None, D_ATT, tm), col),
        pl.BlockSpec((None, D_ATT, tm), col),
        pl.BlockSpec((None, ATT_HEADS, tm), col),
        pl.BlockSpec((None, D_ATT, tm), col),
        pl.BlockSpec((None, D_ATT, tm), col),
        pl.BlockSpec((None, ATT_HEADS, tm), col),
    ]
    in_specs = [
        pl.BlockSpec((None, tm, d), row),
        pl.BlockSpec((None, 1, d), lambda n_, i: (n_, 0, 0)),
        pl.BlockSpec((None, 1, d), lambda n_, i: (n_, 0, 0)),
        pl.BlockSpec((1, d), const2),
        pl.BlockSpec(w_nn.shape, const2, pipeline_mode=pl.Buffered(1)),
        pl.BlockSpec(w_nt.shape, const2, pipeline_mode=pl.Buffered(1)),
        pl.BlockSpec((ATT_HEADS, tm), const2),
    ]
    return pl.pallas_call(
        _pre_prompt_kernel, grid=(n, l // tm), in_specs=in_specs, out_specs=out_specs,
        out_shape=out_shape,
        scratch_shapes=[pltpu.VMEM((ATT_HEADS, LANES), f32), pltpu.VMEM((LANES, tm), f32)],
        compiler_params=_cparams(("arbitrary", "arbitrary")), name="pre",
    )(x, shift, scale, g_norm, w_nn, w_nt, jnp.broadcast_to(b_f[:, None], (ATT_HEADS, tm)))


def _pre_sample_kernel(x_ref, shift_ref, scale_ref, g_ref, wnn_ref, wnt_ref, bf_ref, seg_ref,
                       q_ref, k_ref, v_ref, zatt_ref, u_ref, zssm_ref, lf_ref, snew_ref):
    hb = _modulated_norm(x_ref[...], g_ref[...], scale_ref[...], shift_ref[...]).astype(bf16)
    p = jnp.dot(hb, wnn_ref[...], preferred_element_type=f32)
    pk = lax.dot_general(hb, wnt_ref[...], _NT, preferred_element_type=f32)
    q = p[:, :D_ATT] * QK_SCALE
    k = pk[:, :D_ATT]
    q_ref[...] = q
    k_ref[...] = k
    v_ref[...] = pk[:, D_ATT:2 * D_ATT]
    zatt_ref[...] = p[:, D_ATT:2 * D_ATT]
    u_ref[...] = p[:, 2 * D_ATT:2 * D_ATT + D_SSM]
    zssm_ref[...] = p[:, 2 * D_ATT + D_SSM:]
    lf_ref[...] = _log_sigmoid(pk[:, 2 * D_ATT:] + bf_ref[...])
    snew_ref[...] = jnp.dot(q * k, seg_ref[...], preferred_element_type=f32, precision=_HI)


def _pre_sample_call(x, shift, scale, g_norm, w_nn, w_nt, b_f_pad, seg):
    r, d = x.shape
    shp = lambda c: jax.ShapeDtypeStruct((r, c), f32)
    return pl.pallas_call(
        _pre_sample_kernel,
        out_shape=[shp(D_ATT), shp(D_ATT), shp(D_ATT), shp(D_ATT), shp(D_SSM), shp(D_SSM),
                   shp(NT_ROWS - 2 * D_ATT), shp(LANES)],
        compiler_params=pltpu.CompilerParams(vmem_limit_bytes=VMEM_LIMIT), name="pre_s",
    )(x, shift, scale, g_norm, w_nn, w_nt, b_f_pad, seg)


def _att_kernel(pt_ref, qa_ref, kt_ref, vt_ref, cum_ref, z_ref,
                snew_ref, lfnew_ref, vnew_ref, tri_ref, qb_hbm, kc_hbm, vc_hbm, lfc_hbm,
                o_ref, od_ref,
                ka, va, m_s, acc_s,
                kbuf, vbuf, lfbuf, qbuf, sc, bias, tot, p_s, dacc, stat, ksem, vsem, lfsem, qsem,
                *, tq, seq, n_tok, n_pages, ppu, ring):
    n = pl.program_id(0)
    qi = pl.program_id(1)
    n_tiles = seq // tq
    units = n_pages // ppu
    ahead = ring - 1
    k_end = n_tok * units
    v_lo, v_hi = units, (n_tok + 1) * units

    def start_pages(src, buf, sem, tok, sub, slot):
        for j in range(ppu):
            pltpu.make_async_copy(src.at[pt_ref[tok, sub * ppu + j]], buf.at[slot, j],
                                  sem.at[slot]).start(priority=PAGE_DMA_PRIORITY)

    def wait_pages(src, buf, sem, slot):
        for j in range(ppu):
            pltpu.make_async_copy(src.at[0], buf.at[slot, j], sem.at[slot]).wait()

    def request(w):
        tok, sub, slot = w // units, w % units, w % ring

        @pl.when(w < k_end)
        def _():
            start_pages(kc_hbm, kbuf, ksem, tok, sub, slot)

        @pl.when((w >= v_lo) & (w < v_hi))
        def _():
            start_pages(vc_hbm, vbuf, vsem, tok - 1, sub, slot)

    def start_token_inputs(t):
        slot = t % 2

        def body(pg, carry):
            pltpu.make_async_copy(lfc_hbm.at[pt_ref[t, pg]], lfbuf.at[slot, pg], lfsem.at[slot]).start()
            return carry
        lax.fori_loop(0, n_pages, body, 0, unroll=min(16, n_pages))
        pltpu.make_async_copy(qb_hbm.at[t], qbuf.at[slot], qsem.at[slot]).start()

    def new_token(t):
        slot = t % 2

        @pl.when(t + 1 < n_tok)
        def _():
            start_token_inputs(t + 1)

        def wait_lf(pg, carry):
            pltpu.make_async_copy(lfc_hbm.at[0], lfbuf.at[slot, pg], lfsem.at[slot]).wait()
            return carry
        lax.fori_loop(0, n_pages, wait_lf, 0, unroll=min(16, n_pages))
        pltpu.make_async_copy(qb_hbm.at[0], qbuf.at[slot], qsem.at[slot]).wait()
        lf2 = lfbuf[slot].reshape(n_pages * ATT_HEADS, LANES)
        sums = sum(jnp.dot(part.astype(bf16), tri_ref[...], preferred_element_type=f32) for part in _split3(lf2))
        bias[slot] = sums[:, :LANES].reshape(n_pages, ATT_HEADS, LANES)
        tot[...] = sums[:, LANES:].reshape(n_pages, ATT_HEADS, LANES)
        lfnew = lfnew_ref[t]

        def suffix(i, run):
            pg = n_pages - 1 - i
            bias[slot, pg] = bias[slot, pg] + run + lfnew
            return run + tot[pg]
        lax.fori_loop(0, n_pages, suffix, jnp.zeros((ATT_HEADS, LANES), f32))

    def token_softmax(t):
        s_all = sc[t % 2] + bias[t % 2]
        snew = snew_ref[t]
        m = jnp.max(jnp.max(s_all, axis=0), axis=-1, keepdims=True)
        m = jnp.maximum(m, snew)
        p_all = jnp.exp(s_all - m[None])
        p_s[...] = p_all
        p_new = jnp.exp(snew - m)
        stat[0] = p_new
        stat[1] = jnp.sum(jnp.sum(p_all, axis=0), axis=-1, keepdims=True) + p_new
        dacc[...] = jnp.zeros_like(dacc)

    def token_output(t):
        ones_row = jnp.ones((ATT_HEADS, LANES), bf16)
        head_row = lax.broadcasted_iota(jnp.int32, (ATT_HEADS, HEAD_DIM), 0)
        past = jnp.zeros((ATT_HEADS, HEAD_DIM), f32)
        for h in range(ATT_HEADS):
            r = sum(lax.dot_general(ones_row, part.astype(bf16), _NT, preferred_element_type=f32)
                    for part in _split3(dacc[h]))
            past = jnp.where(head_row == h, r, past)
        od_ref[t] = (past + stat[0][:, :HEAD_DIM] * vnew_ref[t]) * (1.0 / stat[1][:, :HEAD_DIM])

    def unit_prologue(w):
        t = w // units

        @pl.when(w % units == 0)
        def _():
            @pl.when((t >= 2) & (t <= n_tok + 1))
            def _():
                token_output(t - 2)

            @pl.when((t >= 1) & (t <= n_tok))
            def _():
                token_softmax(t - 1)

            @pl.when(t < n_tok)
            def _():
                new_token(t)

        request(w + ahead)

        @pl.when(w < k_end)
        def _():
            wait_pages(kc_hbm, kbuf, ksem, w % ring)

        @pl.when((w >= v_lo) & (w < v_hi))
        def _():
            wait_pages(vc_hbm, vbuf, vsem, w % ring)

    def unit_compute(w, h, ready):
        tok, sub, slot = w // units, w % units, w % ring
        qh, a = qbuf[jnp.minimum(tok, n_tok - 1) % 2, h], dacc[h]
        if ready is not None:
            qh, a = jnp.where(ready, qh, 0.0), jnp.where(ready, a, 0.0)
        for j in range(ppu):
            sc[tok % 2, sub * ppu + j, pl.ds(h, 1), :] = jnp.sum(kbuf[slot, j, h] * qh, axis=0, keepdims=True)
            pr = p_s[sub * ppu + j, pl.ds(h, 1), :]
            a = a + vbuf[slot, j, h] * jnp.broadcast_to(pr, (HEAD_DIM, LANES))
        dacc[h] = a

    @pl.when((n == 0) & (qi == 0))
    def _():
        vbuf[...] = jnp.zeros_like(vbuf)
        p_s[...] = jnp.zeros_like(p_s)
        dacc[...] = jnp.zeros_like(dacc)
        start_token_inputs(jnp.int32(0))
        for w0 in range(ahead):
            request(jnp.int32(w0))

    @pl.when(qi == 0)
    def _():
        c = cum_ref[...]
        r16 = lax.broadcasted_iota(jnp.int32, (16, seq), 0)
        for h in range(ATT_HEADS):
            ka[h, 0:HEAD_DIM, :] = kt_ref[h * HEAD_DIM:(h + 1) * HEAD_DIM, :]
            va[h, 0:HEAD_DIM, :] = vt_ref[h * HEAD_DIM:(h + 1) * HEAD_DIM, :]
            hi, mid, lo = _split3(c[h:h + 1, :])
            ext = jnp.where(r16 == 0, -hi, jnp.where(r16 == 1, -mid, jnp.where(r16 == 2, -lo,
                            jnp.where(r16 < 6, 1.0, 0.0))))
            ka[h, AUG:AUG + 16, :] = ext.astype(bf16)
            ka[h, AUG + 16:, :] = jnp.zeros((LANES - AUG - 16, seq), bf16)
            va[h, HEAD_DIM:, :] = jnp.ones((LANES - HEAD_DIM, seq), bf16)

    lane_q = lax.broadcasted_iota(jnp.int32, (tq, LANES), 1)
    row = lax.broadcasted_iota(jnp.int32, (tq, tq), 0)
    col = lax.broadcasted_iota(jnp.int32, (tq, tq), 1)
    causal = col <= row

    first_unit = n * (n_tiles * (n_tiles + 1) // 2) + (qi * (qi + 1)) // 2

    def block(kb, diagonal):
        w = first_unit if diagonal else first_unit + 1 + kb
        unit_prologue(w)
        k0 = pl.multiple_of(kb * tq, tq)
        for h in range(ATT_HEADS):
            s = jnp.dot(qa_ref[h], ka[h, :, pl.ds(k0, tq)], preferred_element_type=f32)
            if diagonal:
                s = jnp.where(causal, s, -jnp.inf)
            rowmax = jnp.broadcast_to(jnp.max(s, axis=-1, keepdims=True), (tq, LANES))
            m_new = rowmax if diagonal else jnp.maximum(m_s[h], rowmax)
            p = [jnp.exp2(s[:, j * LANES:(j + 1) * LANES] - m_new) for j in range(tq // LANES)]
            p = jnp.concatenate(p, axis=1)
            pv = lax.dot_general(p.astype(bf16), va[h, :, pl.ds(k0, tq)], _NT, preferred_element_type=f32)
            acc_s[h] = pv if diagonal else jnp.exp2(m_s[h] - m_new) * acc_s[h] + pv
            m_s[h] = m_new
            unit_compute(w, h, m_new[0:HEAD_DIM, :] < jnp.inf)

    def off_diagonal(kb, carry):
        block(kb, False)
        return carry

    block(qi, True)
    lax.fori_loop(0, qi, off_diagonal, 0)

    for pr in range(ATT_HEADS // 2):
        a0, a1 = acc_s[2 * pr], acc_s[2 * pr + 1]
        o0 = a0 * pltpu.roll(1.0 / a0, HEAD_DIM, axis=1)
        o1 = pltpu.roll(a1, HEAD_DIM, axis=1) * (1.0 / a1)
        o = jnp.where(lane_q < HEAD_DIM, o0, o1)
        cols = slice(pr * LANES, (pr + 1) * LANES)
        o_ref[:, cols] = (o * _silu(z_ref[:, cols])).astype(bf16)


def _att_call(qa, ktb, vtb, cum, z_att, page_table, qb, snew_b, lfnew_b, vnew, kc, vc, lfc, tq, ppu, ring):
    n, _, l, _ = qa.shape
    ns, n_pages = page_table.shape
    n_tiles = l // tq
    units = n_pages // ppu
    assert n * (n_tiles * (n_tiles + 1) // 2) >= (ns + 1) * units + 1, "too few key-block steps for the page units"
    lane = jnp.arange(LANES)
    tri = jnp.concatenate([lane[:, None] > lane[None, :], jnp.ones((LANES, LANES), bool)], axis=1).astype(bf16)
    row = lambda n_, i, pt: (n_, i, 0)
    whole = lambda n_, i, pt: (n_, 0, 0)
    const = lambda nd: (lambda n_, i, pt: (0,) * nd)
    page = (ppu, ATT_HEADS, HEAD_DIM, LANES)
    grid_spec = pltpu.PrefetchScalarGridSpec(
        num_scalar_prefetch=1,
        grid=(n, n_tiles),
        in_specs=[pl.BlockSpec((None, ATT_HEADS, tq, LANES), lambda n_, i, pt: (n_, 0, i, 0)),
                  pl.BlockSpec((None, D_ATT, l), whole),
                  pl.BlockSpec((None, D_ATT, l), whole),
                  pl.BlockSpec((None, ATT_HEADS, l), whole),
                  pl.BlockSpec((None, tq, D_ATT), row),
                  pl.BlockSpec(snew_b.shape, const(3)),
                  pl.BlockSpec(lfnew_b.shape, const(3)),
                  pl.BlockSpec(vnew.shape, const(3)),
                  pl.BlockSpec((LANES, 2 * LANES), const(2)),
                  pl.BlockSpec(memory_space=pl.ANY),
                  pl.BlockSpec(memory_space=pl.ANY),
                  pl.BlockSpec(memory_space=pl.ANY),
                  pl.BlockSpec(memory_space=pl.ANY)],
        out_specs=[pl.BlockSpec((None, tq, D_ATT), row),
                   pl.BlockSpec((ns, ATT_HEADS, HEAD_DIM), const(3))],
        scratch_shapes=[pltpu.VMEM((ATT_HEADS, LANES, l), bf16), pltpu.VMEM((ATT_HEADS, LANES, l), bf16),
                        pltpu.VMEM((ATT_HEADS, tq, LANES), f32),
                        pltpu.VMEM((ATT_HEADS, tq, LANES), f32),
                        pltpu.VMEM((ring,) + page, f32),
                        pltpu.VMEM((ring,) + page, f32),
                        pltpu.VMEM((2, n_pages, ATT_HEADS, LANES), f32),
                        pltpu.VMEM((2, ATT_HEADS, HEAD_DIM, LANES), f32),
                        pltpu.VMEM((2, n_pages, ATT_HEADS, LANES), f32),
                        pltpu.VMEM((2, n_pages, ATT_HEADS, LANES), f32),
                        pltpu.VMEM((n_pages, ATT_HEADS, LANES), f32),
                        pltpu.VMEM((n_pages, ATT_HEADS, LANES), f32),
                        pltpu.VMEM((ATT_HEADS, HEAD_DIM, LANES), f32),
                        pltpu.VMEM((2, ATT_HEADS, LANES), f32),
                        pltpu.SemaphoreType.DMA((ring,)), pltpu.SemaphoreType.DMA((ring,)),
                        pltpu.SemaphoreType.DMA((2,)), pltpu.SemaphoreType.DMA((2,))],
    )
    return pl.pallas_call(
        functools.partial(_att_kernel, tq=tq, seq=l, n_tok=ns, n_pages=n_pages, ppu=ppu, ring=ring),
        grid_spec=grid_spec,
        out_shape=[jax.ShapeDtypeStruct((n, l, D_ATT), bf16),
                   jax.ShapeDtypeStruct((ns, ATT_HEADS, HEAD_DIM), f32)],
        compiler_params=_cparams(("arbitrary", "arbitrary")), name="att",
    )(page_table, qa, ktb, vtb, cum, z_att, snew_b, lfnew_b, vnew, tri, qb, kc, vc, lfc)


def _post_math(x, att, y, z_ssm, gate, w_glu, b_glu, w_out, g_final):
    g = jax.nn.gelu(y)
    glu = jax.nn.sigmoid(jnp.dot(g.astype(bf16), w_glu, preferred_element_type=f32) + b_glu)
    ssm = g * glu * _silu(z_ssm)
    mix = jnp.concatenate([att.astype(bf16), ssm.astype(bf16)], axis=-1)
    xo = x + gate * jnp.dot(mix, w_out, preferred_element_type=f32)
    return xo * lax.rsqrt(jnp.mean(xo * xo, axis=-1, keepdims=True) + EPS) * g_final


def _ssm_kernel(*refs, nb, tl, time_major, chunk, with_post):
    u_ref, h0r_ref, h0i_ref, ar_ref, ai_ref, bre_ref, bim_ref, cre_ref, cim_ref, d_ref = refs[:10]
    if with_post:
        x_ref, att_ref, zssm_ref, gate_ref, wglu_ref, bglu_ref, wout_ref, gf_ref = refs[10:18]
        refs = refs[18:]
    else:
        refs = refs[10:]
    y_ref, hr_ref, hi_ref, utn, bu_re, bu_im, hs_re, hs_im, st_re, st_im, yscr, perm = refs[:12]
    i = pl.program_id(0)
    nj = D_SSM // LANES
    last = pl.num_programs(0) - (2 if with_post else 1)

    @pl.when(i == 0)
    def _():
        st_re[...] = h0r_ref[...]
        st_im[...] = h0i_ref[...]

    if with_post:
        y_prev = refs[12]

        @pl.when(i == 0)
        def _():
            y_prev[...] = jnp.zeros_like(y_prev)

    if time_major:
        utn[...] = u_ref[...].reshape(tl * nb, D_SSM)
    else:
        for j in range(nj):
            cols = slice(j * LANES, (j + 1) * LANES)
            for b in range(nb):
                perm[j, b * tl:(b + 1) * tl, :] = u_ref[b, :, cols]
            for t in range(tl):
                utn[t * nb:(t + 1) * nb, cols] = perm[j, pl.ds(t, nb, stride=tl), :]

    for c in range(nj):
        cols = slice(c * chunk, (c + 1) * chunk)
        ub = utn[:, c * LANES:(c + 1) * LANES].astype(bf16)
        bu_re[:, cols] = jnp.dot(ub, bre_ref[c], preferred_element_type=f32)
        bu_im[:, cols] = jnp.dot(ub, bim_ref[c], preferred_element_type=f32)

    if with_post:
        rows, d = nb * tl, x_ref.shape[-1]
        gate = jnp.broadcast_to(gate_ref[...], (nb, tl, d)).reshape(rows, d)
        out = _post_math(x_ref[...].reshape(rows, d), att_ref[...].reshape(rows, D_ATT), y_prev[...],
                         zssm_ref[...].reshape(rows, D_SSM), gate, wglu_ref[...], bglu_ref[...],
                         wout_ref[...], gf_ref[...])
        y_ref[...] = out.reshape(nb, tl, d)

    for c in range(nj):
        cols = slice(c * chunk, (c + 1) * chunk)
        ucols = slice(c * LANES, (c + 1) * LANES)
        ar = jnp.broadcast_to(ar_ref[:, cols], (nb, chunk))
        ai = jnp.broadcast_to(ai_ref[:, cols], (nb, chunk))
        hr, hi = st_re[:, cols], st_im[:, cols]
        for t in range(tl):
            rows = slice(t * nb, (t + 1) * nb)
            hr, hi = (ar * hr - ai * hi + bu_re[rows, cols], ar * hi + ai * hr + bu_im[rows, cols])
            hs_re[rows, cols] = hr.astype(bf16)
            hs_im[rows, cols] = hi.astype(bf16)
        st_re[:, cols] = hr
        st_im[:, cols] = hi
        yscr[:, ucols] = (jnp.dot(hs_re[:, cols], cre_ref[c], preferred_element_type=f32)
                          - jnp.dot(hs_im[:, cols], cim_ref[c], preferred_element_type=f32)
                          + d_ref[:, ucols] * utn[:, ucols])
    if time_major:
        y_ref[...] = yscr[...].reshape(tl, nb, D_SSM)
    else:
        for j in range(nj):
            cols = slice(j * LANES, (j + 1) * LANES)
            for t in range(tl):
                perm[j, pl.ds(t, nb, stride=tl), :] = yscr[t * nb:(t + 1) * nb, cols]
            if with_post:
                y_prev[:, cols] = perm[j]
            else:
                for b in range(nb):
                    y_ref[b, :, cols] = perm[j, b * tl:(b + 1) * tl, :]

    @pl.when(i == last)
    def _():
        hr_ref[...] = st_re[...]
        hi_ref[...] = st_im[...]


def _ssm_call(u, h0_re, h0_im, ab_re, ab_im, b_re, b_im, c_re, c_im, d_skip, tl, time_major, post=None):
    nj = D_SSM // LANES
    if time_major:
        l, nb = u.shape[0], u.shape[1]
        block = lambda w: pl.BlockSpec((tl, nb, w), lambda i: (i, 0, 0))
    else:
        nb, l = u.shape[0], u.shape[1]
        block = lambda w: pl.BlockSpec((nb, tl, w), lambda i: (0, i, 0))
    rows = tl * nb
    steps = l // tl
    full = lambda a: pl.BlockSpec(a.shape, lambda i, nd=a.ndim: (0,) * nd)
    ins = [u, h0_re, h0_im, ab_re, ab_im, b_re, b_im, c_re, c_im, d_skip]
    in_specs = [block(D_SSM)] + [full(a) for a in ins[1:]]
    out_spec, out_width, extra_scratch = block(D_SSM), D_SSM, []
    if post is not None:
        assert not time_major
        x, att, z_ssm = post[:3]
        out_width = x.shape[-1]
        behind = lambda w: pl.BlockSpec((nb, tl, w), lambda i: (0, jnp.maximum(i - 1, 0), 0))
        in_specs[0] = pl.BlockSpec((nb, tl, D_SSM), lambda i, last_block=steps - 1: (0, jnp.minimum(i, last_block), 0))
        ins += list(post)
        in_specs += [behind(out_width), behind(D_ATT), behind(D_SSM)] + [full(a) for a in post[3:]]
        out_spec = behind(out_width)
        extra_scratch = [pltpu.VMEM((rows, D_SSM), f32)]
        steps += 1
    return pl.pallas_call(
        functools.partial(_ssm_kernel, nb=nb, tl=tl, time_major=time_major, chunk=N_STATE // nj,
                          with_post=post is not None),
        grid=(steps,),
        in_specs=in_specs,
        out_specs=[out_spec, full(h0_re), full(h0_im)],
        out_shape=[jax.ShapeDtypeStruct(u.shape[:2] + (out_width,), f32),
                   jax.ShapeDtypeStruct(h0_re.shape, f32), jax.ShapeDtypeStruct(h0_im.shape, f32)],
        scratch_shapes=[pltpu.VMEM((rows, D_SSM), f32),
                        pltpu.VMEM((rows, N_STATE), f32), pltpu.VMEM((rows, N_STATE), f32),
                        pltpu.VMEM((rows, N_STATE), bf16), pltpu.VMEM((rows, N_STATE), bf16),
                        pltpu.VMEM((nb, N_STATE), f32), pltpu.VMEM((nb, N_STATE), f32),
                        pltpu.VMEM((rows, D_SSM), f32),
                        pltpu.VMEM((nj, rows, LANES), f32)] + extra_scratch,
        compiler_params=_cparams(("arbitrary",)), name="ssm",
    )(*ins)


def _post_kernel(*refs, gate_att):
    if gate_att:
        x_ref, att_ref, zatt_ref, y_ref, zssm_ref, gate_ref, wglu_ref, bglu_ref, wout_ref, gf_ref, o_ref = refs
        att = att_ref[...] * _silu(zatt_ref[...])
    else:
        x_ref, att_ref, y_ref, zssm_ref, gate_ref, wglu_ref, bglu_ref, wout_ref, gf_ref, o_ref = refs
        att = att_ref[...]
    o_ref[...] = _post_math(x_ref[...], att, y_ref[...], zssm_ref[...], gate_ref[...], wglu_ref[...], bglu_ref[...],
                            wout_ref[...], gf_ref[...])


def _post_call(x, att, z_att, y, z_ssm, gate, w_glu, b_glu, w_out, g_final, tm):
    n, l, d = x.shape
    row = lambda n_, i: (n_, i, 0)
    const2 = lambda n_, i: (0, 0)
    gate_spec = (pl.BlockSpec((None, 1, d), lambda n_, i: (n_, 0, 0)) if gate.shape[1] == 1
                 else pl.BlockSpec((None, tm, d), row))
    ins = [x, att] + ([z_att] if z_att is not None else []) + [y, z_ssm, gate, w_glu, b_glu, w_out, g_final]
    in_specs = ([pl.BlockSpec((None, tm, d), row), pl.BlockSpec((None, tm, D_ATT), row)]
                + ([pl.BlockSpec((None, tm, D_ATT), row)] if z_att is not None else [])
                + [pl.BlockSpec((None, tm, D_SSM), row),
                   pl.BlockSpec((None, tm, D_SSM), row), gate_spec,
                   pl.BlockSpec(w_glu.shape, const2), pl.BlockSpec((1, D_SSM), const2),
                   pl.BlockSpec(w_out.shape, const2), pl.BlockSpec((1, d), const2)])
    return pl.pallas_call(
        functools.partial(_post_kernel, gate_att=z_att is not None),
        grid=(n, l // tm), in_specs=in_specs,
        out_specs=pl.BlockSpec((None, tm, d), row),
        out_shape=jax.ShapeDtypeStruct((n, l, d), f32),
        compiler_params=_cparams(("arbitrary", "arbitrary")), name="post",
    )(*ins)


def kernel(x_prompt, x_sample, c_prompt, c_sample, cache_k, cache_v, cache_logf, state_ssm_re, state_ssm_im,
           page_table, g_norm, w_ada, b_ada, w_in, b_fgate, a_re, a_im, log_dt, b_re, b_im, c_re, c_im, d_skip,
           w_glu, b_glu, w_out, g_final):
    n, l, d = x_prompt.shape
    ns = x_sample.shape[0]
    depth = w_in.shape[0]
    assert depth == 1 and x_sample.shape[1] == 1
    lyr = 0
    xs = x_sample.reshape(ns, d)

    splits = [0, D_ATT, 2 * D_ATT, 3 * D_ATT, 4 * D_ATT, 4 * D_ATT + ATT_HEADS,
              4 * D_ATT + ATT_HEADS + D_SSM, 4 * D_ATT + ATT_HEADS + 2 * D_SSM]
    w_t = jnp.swapaxes(w_in[lyr], 0, 1)
    wq, wk, wv, wz, wf, wu, wzs = [w_t[splits[i]:splits[i + 1]] for i in range(7)]
    w_nn = jnp.concatenate([wq, wz, wu, wzs], axis=0).T.astype(bf16)
    w_nt = jnp.concatenate([wk, wv, wf, jnp.zeros((NT_ROWS - 2 * D_ATT - ATT_HEADS, d), f32)], axis=0).astype(bf16)
    b_f = b_fgate[lyr]
    b_f_pad = jnp.concatenate([b_f, jnp.zeros((NT_ROWS - 2 * D_ATT - ATT_HEADS,), f32)])[None, :]
    seg = (jnp.arange(D_ATT)[:, None] // HEAD_DIM == jnp.arange(LANES)[None, :]).astype(f32)
    gn = g_norm[lyr][None, :]
    gf = g_final[None, :]
    w_glu_b = w_glu[lyr].astype(bf16)
    w_out_b = w_out[lyr].astype(bf16)
    b_glu2 = b_glu[lyr][None, :]

    mod = _ada_call(jnp.concatenate([c_prompt, c_sample], axis=0), w_ada[lyr], b_ada[lyr][None, :])
    shift_p, scale_p, gate_p = [mod[:n, k * d:(k + 1) * d].reshape(n, 1, d) for k in range(3)]
    shift_s, scale_s, gate_s = [mod[n:, k * d:(k + 1) * d] for k in range(3)]

    ab_re, ab_im, bb_re, bb_im = _ssm_param_call(a_re[lyr], a_im[lyr], log_dt[lyr], b_re[lyr], b_im[lyr])
    ab_re_row = ab_re.reshape(1, N_STATE)
    ab_im_row = ab_im.reshape(1, N_STATE)
    slabs = D_SSM // LANES
    slab_diag = lambda w: _block_diag(w.reshape((slabs, SSM_GROUPS // slabs) + w.shape[1:]))
    bd_re = slab_diag(bb_re).astype(bf16)
    bd_im = slab_diag(bb_im).astype(bf16)
    cd_re = slab_diag(jnp.swapaxes(c_re[lyr], 1, 2)).astype(bf16)
    cd_im = slab_diag(jnp.swapaxes(c_im[lyr], 1, 2)).astype(bf16)
    d_row = d_skip[lyr].reshape(1, D_SSM)

    n_pages = page_table.shape[1]
    tiles = _tiles(l, n_pages)
    qa, z_att, u_p, z_ssm, kt, vt, lft, ktb, vtb, cum = _pre_prompt_call(x_prompt, shift_p, scale_p, gn, w_nn, w_nt,
                                                                         b_f, tiles["tm"])
    qs, k_s, v_s, zatt_s, u_s, zssm_s, lf_s, snew = _pre_sample_call(xs, shift_s, scale_s, gn, w_nn, w_nt,
                                                                     b_f_pad, seg)

    qb = jnp.broadcast_to(qs.reshape(ns, ATT_HEADS, HEAD_DIM, 1), (ns, ATT_HEADS, HEAD_DIM, LANES))
    snew_b = jnp.broadcast_to(snew[:, :ATT_HEADS, None], (ns, ATT_HEADS, LANES))
    lfnew_b = jnp.broadcast_to(lf_s[:, :ATT_HEADS, None], (ns, ATT_HEADS, LANES))
    kc = jnp.transpose(cache_k[lyr], (0, 2, 3, 1))
    vc = jnp.transpose(cache_v[lyr], (0, 2, 3, 1))
    lfc = jnp.transpose(cache_logf[lyr], (0, 2, 1))
    att, att_s = _att_call(qa, ktb, vtb, cum, z_att, page_table, qb, snew_b, lfnew_b,
                           v_s.reshape(ns, ATT_HEADS, HEAD_DIM), kc, vc, lfc,
                           tq=tiles["tq"], ppu=tiles["ppu"], ring=tiles["ring"])

    zeros_state = jnp.zeros((n, N_STATE), f32)
    y_prompt, hp_re, hp_im = _ssm_call(u_p, zeros_state, zeros_state, ab_re_row, ab_im_row, bd_re, bd_im, cd_re, cd_im,
                                       d_row, tl=tiles["tl"], time_major=False,
                                       post=(x_prompt, att, z_ssm, gate_p, w_glu_b, b_glu2, w_out_b, gf))

    heads_t = lambda t: jnp.transpose(t.reshape(n, ATT_HEADS, HEAD_DIM, l), (0, 3, 1, 2))[None]
    k_prompt = heads_t(kt)
    v_prompt = heads_t(vt)
    logf_prompt = jnp.transpose(lft, (0, 2, 1))[None]

    h0r = state_ssm_re[lyr].reshape(ns, N_STATE)
    h0i = state_ssm_im[lyr].reshape(ns, N_STATE)
    y_s, hs_re, hs_im = _ssm_call(u_s[None], h0r, h0i, ab_re_row, ab_im_row, bd_re, bd_im, cd_re, cd_im, d_row,
                                  tl=1, time_major=True)
    y_sample = _post_call(xs[None], att_s.reshape(1, ns, D_ATT), zatt_s[None], y_s, zssm_s[None], gate_s[None],
                          w_glu_b, b_glu2, w_out_b, gf, ns)

    st = lambda a, b: a.reshape(1, b, SSM_GROUPS, SSM_STATE)
    return (y_prompt, y_sample.reshape(ns, 1, d),
            k_prompt, v_prompt, logf_prompt, st(hp_re, n), st(hp_im, n),
            k_s.reshape(1, ns, 1, ATT_HEADS, HEAD_DIM), v_s.reshape(1, ns, 1, ATT_HEADS, HEAD_DIM),
            lf_s[:, :ATT_HEADS].reshape(1, ns, 1, ATT_HEADS), st(hs_re, ns), st(hs_im, ns))
```
